```python
import math
import jax
import jax.numpy as jnp
from jax import lax
import numpy as np

D_MODEL = 1024
BATCH = 2
SEQ = 8192
DEPTH = 2
DEC_BATCH = 4
DEC_SEQ = 8192
PAST_LEN = 128

D_PLE = 256
N_MIX_GROUPS = 4
BRANCH = D_MODEL // 2
D_INNER = N_MIX_GROUPS * BRANCH

RET_HEADS = 4
RET_DK = BRANCH // RET_HEADS
RET_DV = BRANCH // RET_HEADS
HGRN_HEADS = 4
HGRN_DK = BRANCH // HGRN_HEADS
HGRN_DV = BRANCH // HGRN_HEADS
ML_HEADS = 4
ML_DK = BRANCH // ML_HEADS
ML_DV = BRANCH // ML_HEADS
SSD_HEADDIM = 64
SSD_HEADS = BRANCH // SSD_HEADDIM
SSD_GROUPS = 2
SSD_STATE = 128
SSD_CONV = 5
SSD_CONV_DIM = BRANCH + 2 * SSD_GROUPS * SSD_STATE

RET_W = 4 * BRANCH
HGRN_W = 5 * BRANCH
ML_W = 5 * BRANCH + 4 * ML_HEADS
SSD_W = BRANCH + SSD_CONV_DIM + 2 * SSD_HEADS
D_PROJ = RET_W + HGRN_W + ML_W + SSD_W
SPLITS = (RET_W, RET_W + HGRN_W, RET_W + HGRN_W + ML_W)

CHUNK = 64
HGRN_CHUNK = 16
ROPE_BASE = 10000.0
NORM_EPS = 1e-5
NEG = -1e30
DN_ALPHA = (2 * DEPTH) ** 0.25
DN_BETA = (8 * DEPTH) ** -0.25

kernel_name = 'hybrid_bidir_parallel_heads_encoder'


def _layer_norm(x, g, b):
    xf = x.astype(jnp.float32)
    mu = jnp.mean(xf, axis=-1, keepdims=True)
    xc = xf - mu
    var = jnp.mean(xc * xc, axis=-1, keepdims=True)
    return (xc * lax.rsqrt(var + NORM_EPS) * g + b).astype(x.dtype)


def _group_norm(x, g, groups, center):
    b, t, w = x.shape
    xf = x.astype(jnp.float32).reshape(b, t, groups, w // groups)
    if center:
        xf = xf - jnp.mean(xf, axis=-1, keepdims=True)
    y = xf * lax.rsqrt(jnp.mean(xf * xf, axis=-1, keepdims=True) + NORM_EPS)
    return y.reshape(b, t, w) * g


def _rope(x):
    t, d = x.shape[1], x.shape[-1]
    inv = ROPE_BASE ** (-jnp.arange(0, d, 2, dtype=jnp.float32) / d)
    ang = jnp.arange(t, dtype=jnp.float32)[:, None] * inv[None, :]
    cos = jnp.cos(ang)[None, :, None, :]
    sin = jnp.sin(ang)[None, :, None, :]
    xf = x.astype(jnp.float32)
    x1, x2 = xf[..., : d // 2], xf[..., d // 2:]
    return jnp.concatenate([x1 * cos - x2 * sin, x2 * cos + x1 * sin], axis=-1).astype(x.dtype)


def _bidirectional(core, fwd_args, bwd_args):
    fwd = core(*fwd_args)
    bwd = core(*[jnp.flip(a, axis=1) for a in bwd_args])
    return fwd + jnp.flip(bwd, axis=1)


def _linear_recurrence(q, k, v, log_f, chunk):
    b, t, h, dk = q.shape
    dv = v.shape[-1]
    n = t // chunk

    def blocks(a):
        return a.astype(jnp.float32).reshape((b, n, chunk) + a.shape[2:])

    q, k, v, log_f = blocks(q), blocks(k), blocks(v), blocks(log_f)
    cum = jnp.cumsum(log_f, axis=2)
    last = cum[:, :, -1]
    tri = jnp.tril(jnp.ones((chunk, chunk), dtype=bool))
    if log_f.shape[-1] == 1:
        seg = cum[..., 0]
        diff = seg[:, :, :, None, :] - seg[:, :, None, :, :]
        decay = jnp.exp(jnp.where(tri[None, None, :, :, None], diff, NEG))
        scores = jnp.einsum('bnihd,bnjhd->bnijh', q, k) * decay
    else:
        diff = cum[:, :, :, None] - cum[:, :, None, :]
        decay = jnp.exp(jnp.where(tri[None, None, :, :, None, None], diff, NEG))
        scores = jnp.einsum('bnihd,bnjhd,bnijhd->bnijh', q, k, decay)
    intra = jnp.einsum('bnijh,bnjhe->bnihe', scores, v)
    q_in = q * jnp.exp(cum)
    k_in = k * jnp.exp(last[:, :, None] - cum)

    def step(state, xs):
        qc, kc, vc, dc = xs
        out = jnp.einsum('bihd,bhde->bihe', qc, state)
        state = dc[..., None] * state + jnp.einsum('bjhd,bjhe->bhde', kc, vc)
        return state, out

    s0 = jnp.zeros((b, h, dk, dv), jnp.float32)
    xs = tuple(jnp.moveaxis(a, 1, 0) for a in (q_in, k_in, v, jnp.exp(last)))
    _, inter = lax.scan(step, s0, xs)
    return (intra + jnp.moveaxis(inter, 0, 1)).reshape(b, t, h, dv)


def _mlstm_recurrence(q, k, v, i_pre, log_f, chunk):
    b, t, h, dk = q.shape
    dv = v.shape[-1]
    n = t // chunk

    def blocks(a):
        return jnp.moveaxis(a.astype(jnp.float32).reshape((b, n, chunk) + a.shape[2:]), 1, 0)

    qc, kc, vc, ic, fc = blocks(q), blocks(k) * (dk ** -0.5), blocks(v), blocks(i_pre), blocks(log_f)
    causal = jnp.tril(jnp.ones((chunk, chunk), dtype=bool))[None, :, :, None]

    def step(carry, xs):
        c_s, n_s, m = carry
        qx, kx, vx, ix, fx = xs
        bcum = jnp.cumsum(fx, axis=1)
        logw = bcum[:, :, None, :] - bcum[:, None, :, :] + ix[:, None, :, :]
        logw = jnp.where(causal, logw, NEG)
        log_inter = bcum + m[:, None, :]
        m_i = jnp.maximum(log_inter, jnp.max(logw, axis=2))
        w = jnp.exp(logw - m_i[:, :, None, :])
        s_inter = jnp.exp(log_inter - m_i)
        qk = jnp.einsum('bihd,bjhd->bijh', qx, kx) * w
        num = s_inter[..., None] * jnp.einsum('bihd,bhde->bihe', qx, c_s) + jnp.einsum('bijh,bjhe->bihe', qk, vx)
        den = s_inter * jnp.einsum('bihd,bhd->bih', qx, n_s) + jnp.sum(qk, axis=2)
        out = num / jnp.maximum(jnp.abs(den), jnp.exp(-m_i))[..., None]
        b_last = bcum[:, -1]
        log_k = b_last[:, None, :] - bcum + ix
        m_new = jnp.maximum(b_last + m, jnp.max(log_k, axis=1))
        wk = jnp.exp(log_k - m_new[:, None, :])
        sc = jnp.exp(b_last + m - m_new)
        kw = kx * wk[..., None]
        c_new = sc[..., None, None] * c_s + jnp.einsum('bjhd,bjhe->bhde', kw, vx)
        n_new = sc[..., None] * n_s + jnp.sum(kw, axis=1)
        return (c_new, n_new, m_new), out

    init = (jnp.zeros((b, h, dk, dv), jnp.float32), jnp.zeros((b, h, dk), jnp.float32),
            jnp.zeros((b, h), jnp.float32))
    _, out = lax.scan(step, init, (qc, kc, vc, ic, fc))
    return jnp.moveaxis(out, 0, 1).reshape(b, t, h, dv)


def _centred_conv(x, w, bias):
    pad = (SSD_CONV - 1) // 2
    y = lax.conv_general_dilated(x, w[:, None, :].astype(x.dtype), window_strides=(1,),
                                 padding=[(pad, pad)], dimension_numbers=('NWC', 'WIO', 'NWC'),
                                 feature_group_count=x.shape[-1])
    return y + bias.astype(x.dtype)


def _retention(pr, log_rate, norm_g):
    b, t, _ = pr.shape
    q, k, v, g = jnp.split(pr, 4, axis=-1)
    q = _rope(q.reshape(b, t, RET_HEADS, RET_DK))
    k = _rope(k.reshape(b, t, RET_HEADS, RET_DK)) * (RET_DK ** -0.5)
    v = v.reshape(b, t, RET_HEADS, RET_DV)
    lg = -jnp.exp(log_rate.astype(jnp.float32))
    lg_f = jnp.broadcast_to(lg[0][:, None], (b, t, RET_HEADS, 1))
    lg_b = jnp.broadcast_to(lg[1][:, None], (b, t, RET_HEADS, 1))
    o = _bidirectional(lambda *a: _linear_recurrence(*a, CHUNK), (q, k, v, lg_f), (q, k, v, lg_b))
    return _group_norm(o.reshape(b, t, BRANCH), norm_g, RET_HEADS, True) * jax.nn.silu(g)


def _hgrn2(ph, lb, norm_g):
    b, t, _ = ph.shape
    q, zf, zb, i, g = jnp.split(ph, 5, axis=-1)
    shp = (b, t, HGRN_HEADS, HGRN_DK)

    def gates(z, low):
        s = jax.nn.sigmoid(z.astype(jnp.float32))
        log_f = jnp.log(low + (1.0 - low) * s)
        k = (1.0 - low) * (1.0 - s)
        return k.reshape(shp), log_f.reshape(shp)

    k_f, lf_f = gates(zf, lb[0])
    k_b, lf_b = gates(zb, lb[1])
    q = q.reshape(shp)
    v = i.reshape(b, t, HGRN_HEADS, HGRN_DV)
    o = _bidirectional(lambda *a: _linear_recurrence(*a, HGRN_CHUNK), (q, k_f, v, lf_f), (q, k_b, v, lf_b))
    return _group_norm(o.reshape(b, t, BRANCH), norm_g, HGRN_HEADS, False) * jax.nn.silu(g)


def _mlstm(pm, i_bias, f_bias, norm_g):
    b, t, _ = pm.shape
    q, k, v, o, g, gts = jnp.split(pm, [BRANCH, 2 * BRANCH, 3 * BRANCH, 4 * BRANCH, 5 * BRANCH], axis=-1)
    shp = (b, t, ML_HEADS, ML_DK)
    gts = gts.astype(jnp.float32).reshape(b, t, 4, ML_HEADS)
    i_pre = gts[:, :, 0:2] + i_bias
    log_f = jax.nn.log_sigmoid(gts[:, :, 2:4] + f_bias)
    q, k = q.reshape(shp), k.reshape(shp)
    v = v.reshape(b, t, ML_HEADS, ML_DV)
    h = _bidirectional(lambda *a: _mlstm_recurrence(*a, CHUNK),
                       (q, k, v, i_pre[:, :, 0], log_f[:, :, 0]),
                       (q, k, v, i_pre[:, :, 1], log_f[:, :, 1]))
    h = jax.nn.sigmoid(o.astype(jnp.float32)) * h.reshape(b, t, BRANCH)
    return _group_norm(h, norm_g, ML_HEADS, True) * jax.nn.silu(g)


def _ssd(ps, conv_w, conv_b, a_log, dt_bias, d_skip, norm_g):
    b, t, _ = ps.shape
    z, xbc, dt_raw = jnp.split(ps, [BRANCH, BRANCH + SSD_CONV_DIM], axis=-1)
    xbc = jax.nn.silu(_centred_conv(xbc, conv_w, conv_b))
    xs, bm, cm = jnp.split(xbc, [BRANCH, BRANCH + SSD_GROUPS * SSD_STATE], axis=-1)
    rep = SSD_HEADS // SSD_GROUPS
    xs = xs.reshape(b, t, SSD_HEADS, SSD_HEADDIM)
    bm = jnp.repeat(bm.reshape(b, t, SSD_GROUPS, SSD_STATE), rep, axis=2)
    cm = jnp.repeat(cm.reshape(b, t, SSD_GROUPS, SSD_STATE), rep, axis=2)
    dt = jax.nn.softplus(dt_raw.astype(jnp.float32).reshape(b, t, 2, SSD_HEADS) + dt_bias)
    a = -jnp.exp(a_log.astype(jnp.float32))
    log_decay = dt * a
    y = _bidirectional(lambda *args: _linear_recurrence(*args, CHUNK),
                       (cm, bm * dt[:, :, 0, :, None], xs, log_decay[:, :, 0, :, None]),
                       (cm, bm * dt[:, :, 1, :, None], xs, log_decay[:, :, 1, :, None]))
    y = y + d_skip[:, None] * xs
    return _group_norm(y.reshape(b, t, BRANCH) * jax.nn.silu(z), norm_g, SSD_GROUPS, False)


def _layer(x, p, w_in, w_out, ln_g, ln_b, ret_log_rate, ret_norm_g, lb, hgrn_norm_g,
           ml_i_bias, ml_f_bias, ml_norm_g, ssd_conv_w, ssd_conv_b, ssd_a_log, ssd_dt_bias,
           ssd_d, ssd_norm_g, w_ple_gate, w_ple_proj):
    proj = jnp.einsum('btd,de->bte', x, w_in)
    pr, ph, pm, ps = jnp.split(proj, SPLITS, axis=-1)
    mixed = jnp.concatenate([
        _retention(pr, ret_log_rate, ret_norm_g),
        _hgrn2(ph, lb, hgrn_norm_g),
        _mlstm(pm, ml_i_bias, ml_f_bias, ml_norm_g),
        _ssd(ps, ssd_conv_w, ssd_conv_b, ssd_a_log, ssd_dt_bias, ssd_d, ssd_norm_g),
    ], axis=-1).astype(x.dtype)
    y = jnp.einsum('bte,ed->btd', mixed, w_out)
    x = _layer_norm(DN_ALPHA * x + y, ln_g, ln_b)
    gate = jax.nn.sigmoid(jnp.einsum('btd,de->bte', x, w_ple_gate).astype(jnp.float32))
    return (x + gate * jnp.einsum('btp,pd->btd', p, w_ple_proj)).astype(x.dtype)


def _trunk(x, p, params):
    (w_in, w_out, ln_g, ln_b, ret_log_rate, ret_norm_g, hgrn_lb_logits, hgrn_norm_g,
     mlstm_i_bias, mlstm_f_bias, mlstm_norm_g, ssd_conv_w, ssd_conv_b, ssd_a_log,
     ssd_dt_bias, ssd_d, ssd_norm_g, w_ple_gate, w_ple_proj) = params
    lb_w = jax.nn.softmax(hgrn_lb_logits.astype(jnp.float32), axis=0)
    lb = jnp.cumsum(lb_w, axis=0) - lb_w[0:1]
    for l in range(DEPTH):
        x = _layer(x, p[l], w_in[l], w_out[l], ln_g[l], ln_b[l], ret_log_rate[l], ret_norm_g[l],
                   lb[l], hgrn_norm_g[l], mlstm_i_bias[l], mlstm_f_bias[l], mlstm_norm_g[l],
                   ssd_conv_w[l], ssd_conv_b[l], ssd_a_log[l], ssd_dt_bias[l], ssd_d[l],
                   ssd_norm_g[l], w_ple_gate[l], w_ple_proj[l])
    return x


def setup_inputs(seed: int = 0) -> dict:
    key = jax.random.key(seed)
    ks = jax.random.split(key, 24)
    f32 = jnp.float32

    def nrm(k, shape, scale):
        return scale * jax.random.normal(k, shape, f32)

    x_prompt = nrm(ks[0], (BATCH, SEQ, D_MODEL), 1.0)
    x_sample = nrm(ks[1], (DEC_BATCH, DEC_SEQ, D_MODEL), 1.0)
    p_prompt = nrm(ks[2], (DEPTH, BATCH, SEQ, D_PLE), 1.0)
    p_sample = nrm(ks[3], (DEPTH, DEC_BATCH, DEC_SEQ, D_PLE), 1.0)
    w_in = nrm(ks[4], (DEPTH, D_MODEL, D_PROJ), D_MODEL ** -0.5)
    w_out = nrm(ks[5], (DEPTH, D_INNER, D_MODEL), DN_BETA * D_INNER ** -0.5)
    ln_g = 1.0 + nrm(ks[6], (DEPTH, D_MODEL), 0.02)
    ln_b = nrm(ks[7], (DEPTH, D_MODEL), 0.02)
    ret_exp = jnp.linspace(5.0, 12.0, RET_HEADS, dtype=f32)
    ret_log_rate = jnp.log(-jnp.log1p(-jnp.exp2(-ret_exp))) + nrm(ks[8], (DEPTH, 2, RET_HEADS), 0.05)
    ret_norm_g = 1.0 + nrm(ks[9], (DEPTH, BRANCH), 0.02)
    hgrn_lb_logits = nrm(ks[10], (DEPTH, 2, BRANCH), 0.1)
    hgrn_norm_g = 1.0 + nrm(ks[11], (DEPTH, BRANCH), 0.02)
    mlstm_i_bias = nrm(ks[12], (DEPTH, 2, ML_HEADS), 0.1)
    mlstm_f_bias = jnp.linspace(3.0, 6.0, ML_HEADS, dtype=f32) + nrm(ks[13], (DEPTH, 2, ML_HEADS), 0.1)
    mlstm_norm_g = 1.0 + nrm(ks[14], (DEPTH, BRANCH), 0.02)
    ssd_conv_w = nrm(ks[15], (DEPTH, SSD_CONV, SSD_CONV_DIM), SSD_CONV ** -0.5)
    ssd_conv_b = nrm(ks[16], (DEPTH, SSD_CONV_DIM), 0.02)
    ssd_a_log = jnp.log(jax.random.uniform(ks[17], (DEPTH, 2, SSD_HEADS), f32, 1.0, 16.0))
    dt0 = jnp.exp(jax.random.uniform(ks[18], (DEPTH, 2, SSD_HEADS), f32, math.log(1e-3), math.log(1e-1)))
    ssd_dt_bias = dt0 + jnp.log(-jnp.expm1(-dt0))
    ssd_d = 1.0 + nrm(ks[19], (DEPTH, SSD_HEADS), 0.02)
    ssd_norm_g = 1.0 + nrm(ks[20], (DEPTH, BRANCH), 0.02)
    w_ple_gate = nrm(ks[21], (DEPTH, D_MODEL, D_MODEL), D_MODEL ** -0.5)
    w_ple_proj = nrm(ks[22], (DEPTH, D_PLE, D_MODEL), D_PLE ** -0.5)
    return {'x_prompt': x_prompt, 'x_sample': x_sample, 'p_prompt': p_prompt, 'p_sample': p_sample,
            'w_in': w_in, 'w_out': w_out, 'ln_g': ln_g, 'ln_b': ln_b,
            'ret_log_rate': ret_log_rate, 'ret_norm_g': ret_norm_g,
            'hgrn_lb_logits': hgrn_lb_logits, 'hgrn_norm_g': hgrn_norm_g,
            'mlstm_i_bias': mlstm_i_bias, 'mlstm_f_bias': mlstm_f_bias, 'mlstm_norm_g': mlstm_norm_g,
            'ssd_conv_w': ssd_conv_w, 'ssd_conv_b': ssd_conv_b, 'ssd_a_log': ssd_a_log,
            'ssd_dt_bias': ssd_dt_bias, 'ssd_d': ssd_d, 'ssd_norm_g': ssd_norm_g,
            'w_ple_gate': w_ple_gate, 'w_ple_proj': w_ple_proj}


def reference(x_prompt, x_sample, p_prompt, p_sample, w_in, w_out, ln_g, ln_b,
              ret_log_rate, ret_norm_g, hgrn_lb_logits, hgrn_norm_g,
              mlstm_i_bias, mlstm_f_bias, mlstm_norm_g,
              ssd_conv_w, ssd_conv_b, ssd_a_log, ssd_dt_bias, ssd_d, ssd_norm_g,
              w_ple_gate, w_ple_proj):
    params = (w_in, w_out, ln_g, ln_b, ret_log_rate, ret_norm_g, hgrn_lb_logits, hgrn_norm_g,
              mlstm_i_bias, mlstm_f_bias, mlstm_norm_g, ssd_conv_w, ssd_conv_b, ssd_a_log,
              ssd_dt_bias, ssd_d, ssd_norm_g, w_ple_gate, w_ple_proj)
    y_prompt = _trunk(x_prompt, p_prompt, params)
    y_sample = _trunk(x_sample, p_sample, params)
    return (y_prompt, y_sample)
```

```python
import functools
import math

import jax
import jax.numpy as jnp
from jax import lax
from jax.experimental import pallas as pl
from jax.experimental.pallas import tpu as pltpu

BF = jnp.bfloat16
F32 = jnp.float32

CHUNK = 128
HEAD = 128
BRANCH = 512
SSD_HEADDIM = 64
SSD_HEADS = 8
SSD_GROUPS = 2
SSD_CONV = 5
HALO = 8
ROPE_BASE = 10000.0
NORM_EPS = 1e-5
OUT_ROWS = 256
VMEM_LIMIT = 56 * 1024 * 1024


def _dot(a, b):
    return jnp.dot(a, b, preferred_element_type=F32)


def _dot_nt(a, b):
    return lax.dot_general(a, b, (((1,), (1,)), ((), ())), preferred_element_type=F32)


def _dot_tn(a, b):
    return lax.dot_general(a, b, (((0,), (0,)), ((), ())), preferred_element_type=F32)


def _split3(x):
    hi = x.astype(BF)
    r = x - hi.astype(F32)
    mid = r.astype(BF)
    lo = (r - mid.astype(F32)).astype(BF)
    return hi, mid, lo


def _sel_dot(sel, x):
    hi, mid, lo = _split3(x)
    return _dot(sel, hi) + _dot(sel, mid) + _dot(sel, lo)


def _dot_sel(x, sel):
    hi, mid, lo = _split3(x)
    return _dot(hi, sel) + _dot(mid, sel) + _dot(lo, sel)


def _sigmoid(x):
    return 1.0 / (1.0 + jnp.exp(-x))


def _silu(x):
    return x * _sigmoid(x)


def _softplus(x):
    return jnp.maximum(x, 0.0) + jnp.log1p(jnp.exp(-jnp.abs(x)))


def _log_sigmoid(x):
    return jnp.minimum(x, 0.0) - jnp.log1p(jnp.exp(-jnp.abs(x)))


def _iota2(shape, axis):
    return lax.broadcasted_iota(jnp.int32, shape, axis)


def _tri_masks(c):
    row = _iota2((c, c), 0)
    col = _iota2((c, c), 1)
    return col <= row, col >= row


def _group_norm(o, center):
    if center:
        o = o - jnp.mean(o, axis=-1, keepdims=True)
    return o * lax.rsqrt(jnp.mean(o * o, axis=-1, keepdims=True) + NORM_EPS)


def _chunk_index(phase, c, nc):
    return jnp.where(phase == 0, nc - 1 - c, c)


def _mixer_call(body, b, t, d, in_arrays, in_specs, scratch, name):
    nc = t // CHUNK
    return pl.pallas_call(
        body,
        grid=(b, 2, nc),
        in_specs=in_specs,
        out_specs=pl.BlockSpec((1, CHUNK, BRANCH),
                               lambda bi, ph, c: (bi, jnp.where(ph == 0, 0, c), 0)),
        out_shape=jax.ShapeDtypeStruct((b, t, BRANCH), BF),
        scratch_shapes=scratch,
        compiler_params=pltpu.CompilerParams(
            dimension_semantics=("arbitrary", "arbitrary", "arbitrary"),
            vmem_limit_bytes=VMEM_LIMIT),
        name=name,
    )(*in_arrays)


def _x_spec(nc, d):
    return pl.BlockSpec((1, CHUNK, d), lambda bi, ph, c: (bi, _chunk_index(ph, c, nc), 0))


def _full_spec(shape):
    zeros = (0,) * len(shape)
    return pl.BlockSpec(shape, lambda bi, ph, c: zeros)


def _rope(x, cos2, sin2):
    return x * cos2 + pltpu.roll(x, HEAD // 2, axis=1) * sin2


def _ret_body(lg_ref, x_ref, cos_ref, sin_ref, w_ref, ng_ref, o_ref,
              sf_ref, sb_ref, sbs_ref, m_ref, *, nc):
    c_sz = CHUNK
    nh = BRANCH // HEAD
    phase = pl.program_id(1)
    c = pl.program_id(2)
    ci = _chunk_index(phase, c, nc)
    scale = HEAD ** -0.5
    xb = x_ref[0].astype(BF)
    cos2 = cos_ref[...]
    sin2 = sin_ref[...]
    pos = _iota2((c_sz, HEAD), 0).astype(F32)
    span = jnp.full((1, HEAD), float(c_sz), F32)

    @pl.when(phase == 0)
    def _():
        @pl.when(c == 0)
        def _():
            sb_ref[...] = jnp.zeros_like(sb_ref)

        kv = _dot(xb, w_ref[:, BRANCH:3 * BRANCH])
        for h in range(nh):
            k = _rope(kv[:, h * HEAD:(h + 1) * HEAD], cos2, sin2)
            v = kv[:, BRANCH + h * HEAD:BRANCH + (h + 1) * HEAD]
            lg_b = lg_ref[1, h]
            sbs_ref[ci, h] = sb_ref[h].astype(BF)
            k_in = (k * (jnp.exp(lg_b * pos) * scale)).astype(BF)
            sb_ref[h] = jnp.exp(lg_b * span) * sb_ref[h] + _dot_tn(k_in, v.astype(BF))

    @pl.when(phase == 1)
    def _():
        @pl.when(c == 0)
        def _():
            sf_ref[...] = jnp.zeros_like(sf_ref)
            lower, upper = _tri_masks(c_sz)
            dist = (_iota2((c_sz, c_sz), 0) - _iota2((c_sz, c_sz), 1)).astype(F32)
            for h in range(nh):
                fwd = jnp.where(lower, jnp.exp(lg_ref[0, h] * dist), 0.0)
                bwd = jnp.where(upper, jnp.exp(-lg_ref[1, h] * dist), 0.0)
                m_ref[h] = (fwd + bwd) * scale

        proj = _dot(xb, w_ref[...])
        for h in range(nh):
            sl = slice(h * HEAD, (h + 1) * HEAD)
            q = _rope(proj[:, sl], cos2, sin2).astype(BF)
            k = _rope(proj[:, BRANCH + h * HEAD:BRANCH + (h + 1) * HEAD], cos2, sin2)
            v = proj[:, 2 * BRANCH + h * HEAD:2 * BRANCH + (h + 1) * HEAD].astype(BF)
            g = proj[:, 3 * BRANCH + h * HEAD:3 * BRANCH + (h + 1) * HEAD]
            lg_f = lg_ref[0, h]
            lg_b = lg_ref[1, h]
            p = (_dot_nt(q, k.astype(BF)) * m_ref[h]).astype(BF)
            o = _dot(p, v)
            states = jnp.concatenate([sf_ref[h].astype(BF), sbs_ref[ci, h]], axis=1)
            r = _dot(q, states)
            o = o + jnp.exp(lg_f * (pos + 1.0)) * r[:, :HEAD] + jnp.exp(lg_b * (c_sz - pos)) * r[:, HEAD:]
            k_in = (k * (jnp.exp(lg_f * (c_sz - 1.0 - pos)) * scale)).astype(BF)
            sf_ref[h] = jnp.exp(lg_f * span) * sf_ref[h] + _dot_tn(k_in, v)
            y = _group_norm(o, True) * ng_ref[:, sl] * _silu(g)
            o_ref[0, :, sl] = y.astype(o_ref.dtype)


def _retention(x, w, log_rate, norm_g, cos2, sin2):
    b, t, d = x.shape
    nc = t // CHUNK
    nh = BRANCH // HEAD
    lg = -jnp.exp(log_rate.astype(F32))
    in_specs = [
        pl.BlockSpec(memory_space=pltpu.SMEM),
        _x_spec(nc, d),
        pl.BlockSpec((CHUNK, HEAD), lambda bi, ph, c: (_chunk_index(ph, c, nc), 0)),
        pl.BlockSpec((CHUNK, HEAD), lambda bi, ph, c: (_chunk_index(ph, c, nc), 0)),
        _full_spec(w.shape),
        _full_spec((1, BRANCH)),
    ]
    scratch = [
        pltpu.VMEM((nh, HEAD, HEAD), F32),
        pltpu.VMEM((nh, HEAD, HEAD), F32),
        pltpu.VMEM((nc, nh, HEAD, HEAD), BF),
        pltpu.VMEM((nh, CHUNK, CHUNK), F32),
    ]
    return _mixer_call(functools.partial(_ret_body, nc=nc), b, t, d,
                       (lg, x, cos2, sin2, w, norm_g.reshape(1, BRANCH)), in_specs, scratch,
                       "retention")


HGRN_LEVELS = int(math.log2(CHUNK))


def _hgrn_tables():
    import numpy as np
    c = CHUNK
    t = np.arange(c)
    sel = np.zeros((2, HGRN_LEVELS + 1, c, c), np.float32)
    mask = np.zeros((2, HGRN_LEVELS + 1, c, c), np.float32)
    u = t[None, :]
    tt = t[:, None]
    for l in range(HGRN_LEVELS):
        s = 1 << l
        pos = t % (2 * s)
        bnd = (t - pos + s - 1)[:, None]
        is_hi = (pos >= s)[:, None]
        sel[0, l] = np.where(is_hi, (u > bnd) & (u <= tt), (u > tt) & (u <= bnd))
        sel[1, l] = np.where(is_hi, (u > bnd) & (u < tt), (u >= tt) & (u <= bnd))
        same = (t[:, None] // (2 * s)) == (t[None, :] // (2 * s))
        mask[0, l] = same & is_hi & ~is_hi.T
        mask[1, l] = same & ~is_hi & is_hi.T
    sel[0, HGRN_LEVELS] = u <= tt
    sel[1, HGRN_LEVELS] = u >= tt
    mask[0, HGRN_LEVELS] = np.eye(c)
    return sel, mask


def _hgrn_gates(z, lb):
    s = _sigmoid(z)
    log_f = jnp.log(lb + (1.0 - lb) * s)
    k = (1.0 - lb) * (1.0 - s)
    return k, log_f


def _hgrn_body(x_ref, w_ref, lb_ref, ng_ref, sel_ref, mask_ref, o_ref,
               sf_ref, sb_ref, sbs_ref, *, nc):
    c_sz = CHUNK
    nh = BRANCH // HEAD
    nl = HGRN_LEVELS
    phase = pl.program_id(1)
    c = pl.program_id(2)
    ci = _chunk_index(phase, c, nc)
    xb = x_ref[0].astype(BF)

    @pl.when(phase == 0)
    def _():
        @pl.when(c == 0)
        def _():
            sb_ref[...] = jnp.zeros_like(sb_ref)

        zv = _dot(xb, w_ref[:, 2 * BRANCH:4 * BRANCH])
        kb, lfb = _hgrn_gates(zv[:, :BRANCH], lb_ref[1:2, :])
        suf = _sel_dot(sel_ref[1, nl], lfb)
        total = suf[0:1, :]
        k_in = (kb * jnp.exp(total - suf)).astype(BF)
        dec = jnp.exp(total)
        v = zv[:, BRANCH:].astype(BF)
        for h in range(nh):
            sl = slice(h * HEAD, (h + 1) * HEAD)
            sbs_ref[ci, h] = sb_ref[h].astype(BF)
            sb_ref[h] = dec[:, sl] * sb_ref[h] + _dot_tn(v[:, sl], k_in[:, sl])

    @pl.when(phase == 1)
    def _():
        @pl.when(c == 0)
        def _():
            sf_ref[...] = jnp.zeros_like(sf_ref)

        proj = _dot(xb, w_ref[...])
        q = proj[:, :BRANCH]
        kf, lff = _hgrn_gates(proj[:, BRANCH:2 * BRANCH], lb_ref[0:1, :])
        kb, lfb = _hgrn_gates(proj[:, 2 * BRANCH:3 * BRANCH], lb_ref[1:2, :])
        v = proj[:, 3 * BRANCH:4 * BRANCH].astype(BF)
        g = proj[:, 4 * BRANCH:]

        scores = [None] * nh
        tok = _iota2((c_sz, BRANCH), 0)
        ksum = (kf + kb).astype(BF)
        qb = q.astype(BF)
        for h in range(nh):
            sl = slice(h * HEAD, (h + 1) * HEAD)
            scores[h] = _dot_nt(qb[:, sl], ksum[:, sl]) * mask_ref[0, nl]
        for dr, (kd, lfd) in enumerate(((kf, lff), (kb, lfb))):
            for l in range(nl):
                e = jnp.exp(_sel_dot(sel_ref[dr, l], lfd))
                is_hi = (tok & (2 * (1 << l) - 1)) >= (1 << l)
                is_query = is_hi if dr == 0 else jnp.logical_not(is_hi)
                u = (jnp.where(is_query, q, kd) * e).astype(BF)
                for h in range(nh):
                    sl = slice(h * HEAD, (h + 1) * HEAD)
                    scores[h] = scores[h] + _dot_nt(u[:, sl], u[:, sl]) * mask_ref[dr, l]

        pre = _sel_dot(sel_ref[0, nl], lff)
        suf = _sel_dot(sel_ref[1, nl], lfb)
        q_f = (q * jnp.exp(pre)).astype(BF)
        q_b = (q * jnp.exp(suf)).astype(BF)
        total = pre[c_sz - 1:c_sz, :]
        k_in = (kf * jnp.exp(total - pre)).astype(BF)
        dec = jnp.exp(total)
        for h in range(nh):
            sl = slice(h * HEAD, (h + 1) * HEAD)
            o = _dot(scores[h].astype(BF), v[:, sl])
            q_cat = jnp.concatenate([q_f[:, sl], q_b[:, sl]], axis=1)
            s_cat = jnp.concatenate([sf_ref[h].astype(BF), sbs_ref[ci, h]], axis=1)
            o = o + _dot_nt(q_cat, s_cat)
            sf_ref[h] = dec[:, sl] * sf_ref[h] + _dot_tn(v[:, sl], k_in[:, sl])
            y = _group_norm(o, False) * ng_ref[:, sl] * _silu(g[:, sl])
            o_ref[0, :, sl] = y.astype(o_ref.dtype)


def _hgrn2(x, w, lb, norm_g):
    b, t, d = x.shape
    nc = t // CHUNK
    nh = BRANCH // HEAD
    sel, mask = _hgrn_tables()
    sel = jnp.asarray(sel, BF)
    mask = jnp.asarray(mask, F32)
    in_specs = [
        _x_spec(nc, d),
        _full_spec(w.shape),
        _full_spec((2, BRANCH)),
        _full_spec((1, BRANCH)),
        _full_spec(sel.shape),
        _full_spec(mask.shape),
    ]
    scratch = [
        pltpu.VMEM((nh, HEAD, HEAD), F32),
        pltpu.VMEM((nh, HEAD, HEAD), F32),
        pltpu.VMEM((nc, nh, HEAD, HEAD), BF),
    ]
    return _mixer_call(functools.partial(_hgrn_body, nc=nc), b, t, d,
                       (x, w, lb, norm_g.reshape(1, BRANCH), sel, mask), in_specs, scratch,
                       "hgrn2")


ML_HEADS = 4
ML_AUG = 2 * HEAD


def _cummax_rows(a, reverse):
    n = a.shape[0]
    row = _iota2(a.shape, 0)
    k = 1
    while k < n:
        if reverse:
            shifted = pltpu.roll(a, n - k, axis=0)
            a = jnp.where(row < n - k, jnp.maximum(a, shifted), a)
        else:
            shifted = pltpu.roll(a, k, axis=0)
            a = jnp.where(row >= k, jnp.maximum(a, shifted), a)
        k *= 2
    return a


def _ml_body(x_ref, w_ref, wgc_ref, wgr_ref, bc_ref, br_ref, ng_ref, o_ref,
             cf_ref, cb_ref, cbs_ref, m_ref, mbs_ref, *, nc):
    c_sz = CHUNK
    nh = ML_HEADS
    phase = pl.program_id(1)
    c = pl.program_id(2)
    ci = _chunk_index(phase, c, nc)
    xb = x_ref[0].astype(BF)
    lane = _iota2((c_sz, HEAD), 1)
    is_fwd = lane < nh
    is_fwd_row = _iota2((1, HEAD), 1) < nh
    lower, upper = _tri_masks(c_sz)
    tri_lo = lower.astype(BF)
    tri_up = upper.astype(BF)

    gc = _dot(xb, wgc_ref[...]) + bc_ref[...]
    ipre = gc[:, :HEAD]
    logf = _log_sigmoid(gc[:, HEAD:])
    cum = jnp.where(is_fwd, _sel_dot(tri_lo, logf), _sel_dot(tri_up, logf))
    a = ipre - cum
    run_max = jnp.where(is_fwd, _cummax_rows(a, False), _cummax_rows(a, True))

    def edge(arr):
        return jnp.where(is_fwd_row, arr[c_sz - 1:c_sz, :], arr[0:1, :])

    def update_state(state_ref, lane0, k_all, v_aug, wk, sc):
        for h in range(nh):
            l = lane0 + h
            kw = (k_all[:, h * HEAD:(h + 1) * HEAD] * wk[:, l:l + 1]).astype(BF)
            state_ref[h] = sc[:, l:l + 1] * state_ref[h] + _dot_tn(kw, v_aug[h])

    ones_col = (lane == 0).astype(BF)

    @pl.when(phase == 0)
    def _():
        @pl.when(c == 0)
        def _():
            cb_ref[...] = jnp.zeros_like(cb_ref)
            m_ref[...] = jnp.zeros_like(m_ref)

        m_prev = m_ref[...]
        mbs_ref[ci] = m_prev
        for h in range(nh):
            cbs_ref[ci, h] = cb_ref[h].astype(BF)
        mu = jnp.maximum(m_prev, run_max)
        mu_e = edge(mu)
        kv = _dot(xb, w_ref[:, BRANCH:3 * BRANCH])
        v_aug = [jnp.concatenate([kv[:, BRANCH + h * HEAD:BRANCH + (h + 1) * HEAD].astype(BF), ones_col],
                                 axis=1) for h in range(nh)]
        update_state(cb_ref, nh, kv[:, :BRANCH], v_aug, jnp.exp(a - mu_e), jnp.exp(m_prev - mu_e))
        m_ref[...] = edge(cum) + mu_e

    @pl.when(phase == 1)
    def _():
        @pl.when(c == 0)
        def _():
            cf_ref[...] = jnp.zeros_like(cf_ref)
            m_ref[...] = jnp.zeros_like(m_ref)

        m_prev = jnp.where(is_fwd_row, m_ref[...], mbs_ref[ci])
        mu = jnp.maximum(m_prev, run_max)
        s_inter = jnp.exp(m_prev - mu)
        thr = jnp.exp(-(cum + mu))

        gr = _dot_nt(wgr_ref[...], xb) + br_ref[...]
        ipre_r = gr[:2 * nh]
        logf_r = _log_sigmoid(gr[2 * nh:])
        row8 = _iota2((2 * nh, c_sz), 0)
        cum_r = jnp.where(row8 < nh, _dot_sel(logf_r, tri_up), _dot_sel(logf_r, tri_lo))
        a_r = ipre_r - cum_r

        proj = _dot(xb, w_ref[...])
        v_aug = [jnp.concatenate([proj[:, 2 * BRANCH + h * HEAD:2 * BRANCH + (h + 1) * HEAD].astype(BF),
                                  ones_col], axis=1) for h in range(nh)]
        for h in range(nh):
            sl = slice(h * HEAD, (h + 1) * HEAD)
            q = proj[:, sl].astype(BF)
            k = proj[:, BRANCH + h * HEAD:BRANCH + (h + 1) * HEAD].astype(BF)
            qk = _dot_nt(q, k)
            states = jnp.concatenate([cf_ref[h].astype(BF), cbs_ref[ci, h]], axis=1)
            r = _dot(q, states)
            hsum = None
            for dr in range(2):
                l = dr * nh + h
                wgt = jnp.where(lower if dr == 0 else upper,
                                jnp.exp(a_r[l:l + 1, :] - mu[:, l:l + 1]), 0.0)
                tot = s_inter[:, l:l + 1] * r[:, dr * ML_AUG:(dr + 1) * ML_AUG] + _dot((qk * wgt).astype(BF), v_aug[h])
                den = jnp.maximum(jnp.abs(tot[:, HEAD:HEAD + 1]), thr[:, l:l + 1])
                out = tot[:, :HEAD] / den
                hsum = out if hsum is None else hsum + out
            o_gate = proj[:, 3 * BRANCH + h * HEAD:3 * BRANCH + (h + 1) * HEAD]
            g = proj[:, 4 * BRANCH + h * HEAD:4 * BRANCH + (h + 1) * HEAD]
            y = _group_norm(_sigmoid(o_gate) * hsum, True) * ng_ref[:, sl] * _silu(g)
            o_ref[0, :, sl] = y.astype(o_ref.dtype)

        mu_e = edge(mu)
        update_state(cf_ref, 0, proj[:, BRANCH:2 * BRANCH], v_aug, jnp.exp(a - mu_e), jnp.exp(m_prev - mu_e))
        m_ref[...] = edge(cum) + mu_e


def _mlstm(x, w, w_gates, i_bias, f_bias, norm_g):
    b, t, d = x.shape
    nc = t // CHUNK
    nh = ML_HEADS
    pad = HEAD - 2 * nh
    w_i = w_gates[:, :2 * nh]
    w_f = w_gates[:, 2 * nh:]
    wgc = jnp.concatenate([jnp.pad(w_i, ((0, 0), (0, pad))), jnp.pad(w_f, ((0, 0), (0, pad)))], axis=1)
    wgr = w_gates.T
    bias = jnp.concatenate([i_bias.reshape(-1), f_bias.reshape(-1)]).astype(F32)
    bc = jnp.concatenate([jnp.pad(bias[:2 * nh], (0, pad)), jnp.pad(bias[2 * nh:], (0, pad))]).reshape(1, 2 * HEAD)
    br = bias.reshape(4 * nh, 1)
    in_specs = [
        _x_spec(nc, d),
        _full_spec(w.shape),
        _full_spec(wgc.shape),
        _full_spec(wgr.shape),
        _full_spec(bc.shape),
        _full_spec(br.shape),
        _full_spec((1, BRANCH)),
    ]
    scratch = [
        pltpu.VMEM((nh, HEAD, ML_AUG), F32),
        pltpu.VMEM((nh, HEAD, ML_AUG), F32),
        pltpu.VMEM((nc, nh, HEAD, ML_AUG), BF),
        pltpu.VMEM((1, HEAD), F32),
        pltpu.VMEM((nc, 1, HEAD), F32),
    ]
    return _mixer_call(functools.partial(_ml_body, nc=nc), b, t, d,
                       (x, w, wgc.astype(BF), wgr.astype(BF), bc, br, norm_g.reshape(1, BRANCH)),
                       in_specs, scratch, "mlstm")


SSD_XBC = BRANCH + 2 * SSD_GROUPS * HEAD
SSD_GW = BRANCH // SSD_GROUPS


def _ssd_body(x_ref, xp_ref, xn_ref, wz_ref, wx_ref, wdc_ref, wdr_ref, dbc_ref, dbr_ref,
              ac_ref, ar_ref, cw_ref, cb_ref, dsk_ref, ng_ref, ex_ref, o_ref,
              xe_ref, sf_ref, sb_ref, sbs_ref, *, nc):
    c_sz = CHUNK
    nh = SSD_HEADS
    phase = pl.program_id(1)
    c = pl.program_id(2)
    ci = _chunk_index(phase, c, nc)
    xb = x_ref[0].astype(BF)
    lane = _iota2((c_sz, HEAD), 1)
    is_fwd = lane < nh
    lower, upper = _tri_masks(c_sz)
    tri_lo = lower.astype(BF)
    tri_up = upper.astype(BF)

    halo = jnp.concatenate([xp_ref[0, 0], xn_ref[0, 0]], axis=0).astype(BF)
    ph = _dot(halo, wx_ref[...])
    keep_prev = jnp.where(ci > 0, 1.0, 0.0)
    keep_next = jnp.where(ci < nc - 1, 1.0, 0.0)
    xe_ref[0:HALO, :] = ph[:HALO] * keep_prev
    xe_ref[HALO:HALO + c_sz, :] = _dot(xb, wx_ref[...])
    xe_ref[HALO + c_sz:, :] = ph[HALO:] * keep_next
    half = (SSD_CONV - 1) // 2
    conv = cb_ref[...]
    for k in range(SSD_CONV):
        conv = conv + cw_ref[k:k + 1, :] * xe_ref[pl.ds(HALO - half + k, c_sz), :]
    xbc = _silu(conv)
    xs = xbc[:, :BRANCH]
    bm = xbc[:, BRANCH:BRANCH + SSD_GROUPS * HEAD].astype(BF)
    cm = xbc[:, BRANCH + SSD_GROUPS * HEAD:].astype(BF)

    dt = _softplus(_dot(xb, wdc_ref[...]) + dbc_ref[...])
    ld = dt * ac_ref[...]
    cum = jnp.where(is_fwd, _sel_dot(tri_lo, ld), _sel_dot(tri_up, ld))
    cum_e = jnp.where(_iota2((1, HEAD), 1) < nh, cum[c_sz - 1:c_sz, :], cum[0:1, :])
    w_state = jnp.exp(cum_e - cum) * dt
    dec = jnp.exp(jnp.broadcast_to(cum_e, (8, HEAD)))

    def expand(a, dr):
        hi, mid, _ = _split3(a)
        e = ex_ref[:, dr * BRANCH:(dr + 1) * BRANCH]
        return _dot(hi, e) + _dot(mid, e)

    def update_state(state_ref, dr):
        xw = (xs * expand(w_state, dr)).astype(BF)
        d512 = expand(dec, dr)[0:1, :]
        for g in range(SSD_GROUPS):
            gs = slice(g * SSD_GW, (g + 1) * SSD_GW)
            state_ref[g] = d512[:, gs] * state_ref[g] + _dot_tn(bm[:, g * HEAD:(g + 1) * HEAD], xw[:, gs])

    @pl.when(phase == 0)
    def _():
        @pl.when(c == 0)
        def _():
            sb_ref[...] = jnp.zeros_like(sb_ref)

        for g in range(SSD_GROUPS):
            sbs_ref[ci, g] = sb_ref[g].astype(BF)
        update_state(sb_ref, 1)

    @pl.when(phase == 1)
    def _():
        @pl.when(c == 0)
        def _():
            sf_ref[...] = jnp.zeros_like(sf_ref)

        dt_r = _softplus(_dot_nt(wdr_ref[...], xb) + dbr_ref[...])
        ld_r = dt_r * ar_ref[...]
        row16 = _iota2((2 * nh, c_sz), 0)
        cum_r = jnp.where(row16 < nh, _dot_sel(ld_r, tri_up), _dot_sel(ld_r, tri_lo))
        z = _dot(xb, wz_ref[...])
        lane_half = _iota2((c_sz, HEAD), 1) < SSD_HEADDIM
        ecum_f = expand(jnp.exp(cum), 0)
        ecum_b = expand(jnp.exp(cum), 1)
        per = SSD_HEADS // SSD_GROUPS
        ys = []
        for g in range(SSD_GROUPS):
            cg = cm[:, g * HEAD:(g + 1) * HEAD]
            gmat = _dot_nt(cg, bm[:, g * HEAD:(g + 1) * HEAD])
            gs = slice(g * SSD_GW, (g + 1) * SSD_GW)
            inter = (ecum_f[:, gs] * _dot(cg, sf_ref[g].astype(BF)) +
                     ecum_b[:, gs] * _dot(cg, sbs_ref[ci, g]))
            pieces = []
            for pair in range(per // 2):
                acc = None
                for side in range(2):
                    h = g * per + 2 * pair + side
                    lf, lbk = h, nh + h
                    m = (jnp.where(lower, jnp.exp(cum[:, lf:lf + 1] - cum_r[lf:lf + 1, :]) * dt_r[lf:lf + 1, :], 0.0) +
                         jnp.where(upper, jnp.exp(cum[:, lbk:lbk + 1] - cum_r[lbk:lbk + 1, :]) * dt_r[lbk:lbk + 1, :], 0.0))
                    xpair = xs[:, (h // 2) * HEAD:(h // 2 + 1) * HEAD]
                    xh = jnp.where(lane_half if side == 0 else jnp.logical_not(lane_half), xpair, 0.0).astype(BF)
                    part = _dot((gmat * m).astype(BF), xh)
                    acc = part if acc is None else acc + part
                pieces.append(acc)
            ys.append(jnp.concatenate(pieces, axis=1) + inter)
        y = jnp.concatenate(ys, axis=1) + dsk_ref[...] * xs
        y = y * _silu(z)
        for g in range(SSD_GROUPS):
            gs = slice(g * SSD_GW, (g + 1) * SSD_GW)
            o_ref[0, :, gs] = (_group_norm(y[:, gs], False) * ng_ref[:, gs]).astype(o_ref.dtype)
        update_state(sf_ref, 0)


def _ssd(x, w_z, w_xbc, w_dt, conv_w, conv_b, a_log, dt_bias, d_skip, norm_g):
    b, t, d = x.shape
    nc = t // CHUNK
    nh = SSD_HEADS
    pad = HEAD - 2 * nh
    wdc = jnp.pad(w_dt, ((0, 0), (0, pad)))
    wdr = w_dt.T
    db = dt_bias.reshape(-1).astype(F32)
    a = (-jnp.exp(a_log.astype(F32))).reshape(-1)
    dbc = jnp.pad(db, (0, pad)).reshape(1, HEAD)
    ac = jnp.pad(a, (0, pad)).reshape(1, HEAD)
    dbr = db.reshape(2 * nh, 1)
    ar = a.reshape(2 * nh, 1)
    cw = jnp.pad(conv_w.astype(F32), ((0, 8 - SSD_CONV), (0, 0)))
    dsk = jnp.repeat(d_skip.astype(F32), SSD_HEADDIM).reshape(1, BRANCH)
    import numpy as np
    ex = np.zeros((HEAD, 2 * BRANCH), np.float32)
    for l in range(2 * nh):
        ex[l, l * SSD_HEADDIM:(l + 1) * SSD_HEADDIM] = 1.0
    ex = jnp.asarray(ex, BF)
    x8 = x.reshape(b, t // HALO, HALO, d)
    per = CHUNK // HALO
    last = t // HALO - 1
    in_specs = [
        _x_spec(nc, d),
        pl.BlockSpec((1, 1, HALO, d),
                     lambda bi, ph, c: (bi, jnp.maximum(_chunk_index(ph, c, nc) * per - 1, 0), 0, 0)),
        pl.BlockSpec((1, 1, HALO, d),
                     lambda bi, ph, c: (bi, jnp.minimum((_chunk_index(ph, c, nc) + 1) * per, last), 0, 0)),
        _full_spec(w_z.shape),
        _full_spec(w_xbc.shape),
        _full_spec(wdc.shape),
        _full_spec(wdr.shape),
        _full_spec(dbc.shape),
        _full_spec(dbr.shape),
        _full_spec(ac.shape),
        _full_spec(ar.shape),
        _full_spec(cw.shape),
        _full_spec((1, SSD_XBC)),
        _full_spec((1, BRANCH)),
        _full_spec((1, BRANCH)),
        _full_spec(ex.shape),
    ]
    scratch = [
        pltpu.VMEM((CHUNK + 2 * HALO, SSD_XBC), F32),
        pltpu.VMEM((SSD_GROUPS, HEAD, SSD_GW), F32),
        pltpu.VMEM((SSD_GROUPS, HEAD, SSD_GW), F32),
        pltpu.VMEM((nc, SSD_GROUPS, HEAD, SSD_GW), BF),
    ]
    return _mixer_call(functools.partial(_ssd_body, nc=nc), b, t, d,
                       (x, x8, x8, w_z, w_xbc, wdc.astype(BF), wdr.astype(BF), dbc, dbr, ac, ar, cw,
                        conv_b.astype(F32).reshape(1, SSD_XBC), dsk, norm_g.reshape(1, BRANCH), ex),
                       in_specs, scratch, "ssd")


def _out_body(x_ref, p_ref, m0_ref, m1_ref, m2_ref, m3_ref, wo_ref, lg_ref, lb_ref, wg_ref, wp_ref,
              o_ref, *, alpha):
    y = None
    for i, m_ref in enumerate((m0_ref, m1_ref, m2_ref, m3_ref)):
        part = _dot(m_ref[0], wo_ref[i * BRANCH:(i + 1) * BRANCH, :])
        y = part if y is None else y + part
    r = alpha * x_ref[0] + y
    mu = jnp.mean(r, axis=-1, keepdims=True)
    rc = r - mu
    var = jnp.mean(rc * rc, axis=-1, keepdims=True)
    xn = rc * lax.rsqrt(var + NORM_EPS) * lg_ref[...] + lb_ref[...]
    gate = _sigmoid(_dot(xn.astype(BF), wg_ref[...]))
    o_ref[0] = xn + gate * _dot(p_ref[0].astype(BF), wp_ref[...])


def _out_layer(x, p, mixed, w_out, ln_g, ln_b, w_gate, w_proj, alpha):
    b, t, d = x.shape
    dp = p.shape[-1]
    rows = OUT_ROWS
    row_spec = lambda width: pl.BlockSpec((1, rows, width), lambda bi, r: (bi, r, 0))
    full = lambda shape: pl.BlockSpec(shape, lambda bi, r: (0,) * len(shape))
    return pl.pallas_call(
        functools.partial(_out_body, alpha=alpha),
        grid=(b, t // rows),
        in_specs=[row_spec(d), row_spec(dp)] + [row_spec(BRANCH)] * 4 +
                 [full(w_out.shape), full((1, d)), full((1, d)), full(w_gate.shape), full(w_proj.shape)],
        out_specs=row_spec(d),
        out_shape=jax.ShapeDtypeStruct((b, t, d), F32),
        compiler_params=pltpu.CompilerParams(
            dimension_semantics=("arbitrary", "arbitrary"), vmem_limit_bytes=VMEM_LIMIT),
        name="out_proj",
    )(x, p, *mixed, w_out, ln_g.reshape(1, d), ln_b.reshape(1, d), w_gate, w_proj)


def _rope_tables(t):
    inv = ROPE_BASE ** (-jnp.arange(0, HEAD, 2, dtype=F32) / HEAD)
    ang = jnp.arange(t, dtype=F32)[:, None] * inv[None, :]
    cos = jnp.cos(ang)
    sin = jnp.sin(ang)
    return jnp.concatenate([cos, cos], axis=1), jnp.concatenate([-sin, sin], axis=1)


def _layer_weights(l, w_in, w_out, hgrn_lb_logits, w_ple_gate, w_ple_proj):
    ret_w = 4 * BRANCH
    hgrn_w = 5 * BRANCH
    ml_w = 5 * BRANCH + 4 * ML_HEADS
    wl = w_in[l]
    o1 = ret_w
    o2 = o1 + hgrn_w
    o3 = o2 + ml_w
    w_ret = wl[:, :o1].astype(BF)
    w_hgrn = wl[:, o1:o2].astype(BF)
    w_ml = wl[:, o2:o2 + 5 * BRANCH]
    k_scale = jnp.concatenate([jnp.ones((BRANCH,), F32), jnp.full((BRANCH,), HEAD ** -0.5, F32),
                               jnp.ones((3 * BRANCH,), F32)])
    w_ml = (w_ml * k_scale).astype(BF)
    w_ml_gates = wl[:, o2 + 5 * BRANCH:o3]
    w_z = wl[:, o3:o3 + BRANCH].astype(BF)
    w_xbc = wl[:, o3 + BRANCH:o3 + BRANCH + SSD_XBC].astype(BF)
    w_dt = wl[:, o3 + BRANCH + SSD_XBC:]
    lb_w = jax.nn.softmax(hgrn_lb_logits.astype(F32), axis=0)
    lb = (jnp.cumsum(lb_w, axis=0) - lb_w[0:1])[l]
    return dict(w_ret=w_ret, w_hgrn=w_hgrn, w_ml=w_ml, w_ml_gates=w_ml_gates, w_z=w_z, w_xbc=w_xbc,
                w_dt=w_dt, lb=lb, w_out=w_out[l].astype(BF), w_gate=w_ple_gate[l].astype(BF),
                w_proj=w_ple_proj[l].astype(BF))


def kernel(x_prompt, x_sample, p_prompt, p_sample, w_in, w_out, ln_g, ln_b, ret_log_rate, ret_norm_g,
           hgrn_lb_logits, hgrn_norm_g, mlstm_i_bias, mlstm_f_bias, mlstm_norm_g, ssd_conv_w, ssd_conv_b,
           ssd_a_log, ssd_dt_bias, ssd_d, ssd_norm_g, w_ple_gate, w_ple_proj):
    depth = w_in.shape[0]
    alpha = (2 * depth) ** 0.25
    weights = [_layer_weights(l, w_in, w_out, hgrn_lb_logits, w_ple_gate, w_ple_proj) for l in range(depth)]
    cos2, sin2 = _rope_tables(x_prompt.shape[1])

    def trunk(x, p):
        for l in range(depth):
            w = weights[l]
            mixed = (
                _retention(x, w["w_ret"], ret_log_rate[l], ret_norm_g[l], cos2, sin2),
                _hgrn2(x, w["w_hgrn"], w["lb"], hgrn_norm_g[l]),
                _mlstm(x, w["w_ml"], w["w_ml_gates"], mlstm_i_bias[l], mlstm_f_bias[l], mlstm_norm_g[l]),
                _ssd(x, w["w_z"], w["w_xbc"], w["w_dt"], ssd_conv_w[l], ssd_conv_b[l], ssd_a_log[l],
                     ssd_dt_bias[l], ssd_d[l], ssd_norm_g[l]),
            )
            x = _out_layer(x, p[l], mixed, w["w_out"], ln_g[l], ln_b[l], w["w_gate"], w["w_proj"], alpha)
        return x

    return trunk(x_prompt, p_prompt), trunk(x_sample, p_sample)
```

```python
import functools
import math

import jax
import jax.numpy as jnp
from jax import lax
from jax.experimental import pallas as pl
from jax.experimental.pallas import tpu as pltpu

BF = jnp.bfloat16
F32 = jnp.float32

CHUNK = 128
HEAD = 128
BRANCH = 512
SSD_HEADDIM = 64
SSD_HEADS = 8
SSD_GROUPS = 2
SSD_CONV = 5
HALO = 8
ROPE_BASE = 10000.0
NORM_EPS = 1e-5
OUT_ROWS = 256
VMEM_LIMIT = 56 * 1024 * 1024


def _dot(a, b):
    return jnp.dot(a, b, preferred_element_type=F32)


def _dot_nt(a, b):
    return lax.dot_general(a, b, (((1,), (1,)), ((), ())), preferred_element_type=F32)


def _dot_tn(a, b):
    return lax.dot_general(a, b, (((0,), (0,)), ((), ())), preferred_element_type=F32)


def _split3(x):
    hi = x.astype(BF)
    r = x - hi.astype(F32)
    mid = r.astype(BF)
    lo = (r - mid.astype(F32)).astype(BF)
    return hi, mid, lo


def _sel_dot(sel, x):
    hi, mid, lo = _split3(x)
    return _dot(sel, hi) + _dot(sel, mid) + _dot(sel, lo)


def _dot_sel(x, sel):
    hi, mid, lo = _split3(x)
    return _dot(hi, sel) + _dot(mid, sel) + _dot(lo, sel)


def _sigmoid(x):
    return 1.0 / (1.0 + jnp.exp(-x))


def _silu(x):
    return x * _sigmoid(x)


def _softplus(x):
    return jnp.maximum(x, 0.0) + jnp.log1p(jnp.exp(-jnp.abs(x)))


def _log_sigmoid(x):
    return jnp.minimum(x, 0.0) - jnp.log1p(jnp.exp(-jnp.abs(x)))


def _iota2(shape, axis):
    return lax.broadcasted_iota(jnp.int32, shape, axis)


def _tri_masks(c):
    row = _iota2((c, c), 0)
    col = _iota2((c, c), 1)
    return col <= row, col >= row


def _group_norm(o, center):
    if center:
        o = o - jnp.mean(o, axis=-1, keepdims=True)
    return o * lax.rsqrt(jnp.mean(o * o, axis=-1, keepdims=True) + NORM_EPS)


def _chunk_index(phase, c, nc):
    return jnp.where(phase == 0, nc - 1 - c, c)


def _mixer_call(body, b, t, d, in_arrays, in_specs, scratch, name):
    nc = t // CHUNK
    return pl.pallas_call(
        body,
        grid=(b, 2, nc),
        in_specs=in_specs,
        out_specs=pl.BlockSpec((1, CHUNK, BRANCH),
                               lambda bi, ph, c: (bi, jnp.where(ph == 0, 0, c), 0)),
        out_shape=jax.ShapeDtypeStruct((b, t, BRANCH), BF),
        scratch_shapes=scratch,
        compiler_params=pltpu.CompilerParams(
            dimension_semantics=("arbitrary", "arbitrary", "arbitrary"),
            vmem_limit_bytes=VMEM_LIMIT),
        name=name,
    )(*in_arrays)


def _x_spec(nc, d):
    return pl.BlockSpec((1, CHUNK, d), lambda bi, ph, c: (bi, _chunk_index(ph, c, nc), 0))


def _full_spec(shape):
    zeros = (0,) * len(shape)
    return pl.BlockSpec(shape, lambda bi, ph, c: zeros)


def _rope(x, cos2, sin2):
    return x * cos2 + pltpu.roll(x, HEAD // 2, axis=1) * sin2


def _ret_body(lg_ref, x_ref, cos_ref, sin_ref, w_ref, ng_ref, o_ref,
              sf_ref, sb_ref, sbs_ref, m_ref, *, nc):
    c_sz = CHUNK
    nh = BRANCH // HEAD
    phase = pl.program_id(1)
    c = pl.program_id(2)
    ci = _chunk_index(phase, c, nc)
    scale = HEAD ** -0.5
    xb = x_ref[0].astype(BF)
    cos2 = cos_ref[...]
    sin2 = sin_ref[...]
    pos = _iota2((c_sz, HEAD), 0).astype(F32)
    span = jnp.full((1, HEAD), float(c_sz), F32)

    @pl.when(phase == 0)
    def _():
        @pl.when(c == 0)
        def _():
            sb_ref[...] = jnp.zeros_like(sb_ref)

        kv = _dot(xb, w_ref[:, BRANCH:3 * BRANCH])
        for h in range(nh):
            k = _rope(kv[:, h * HEAD:(h + 1) * HEAD], cos2, sin2)
            v = kv[:, BRANCH + h * HEAD:BRANCH + (h + 1) * HEAD]
            lg_b = lg_ref[1, h]
            sbs_ref[ci, h] = sb_ref[h].astype(BF)
            k_in = (k * (jnp.exp(lg_b * pos) * scale)).astype(BF)
            sb_ref[h] = jnp.exp(lg_b * span) * sb_ref[h] + _dot_tn(k_in, v.astype(BF))

    @pl.when(phase == 1)
    def _():
        @pl.when(c == 0)
        def _():
            sf_ref[...] = jnp.zeros_like(sf_ref)
            lower, upper = _tri_masks(c_sz)
            dist = (_iota2((c_sz, c_sz), 0) - _iota2((c_sz, c_sz), 1)).astype(F32)
            for h in range(nh):
                fwd = jnp.where(lower, jnp.exp(lg_ref[0, h] * dist), 0.0)
                bwd = jnp.where(upper, jnp.exp(-lg_ref[1, h] * dist), 0.0)
                m_ref[h] = (fwd + bwd) * scale

        proj = _dot(xb, w_ref[...])
        for h in range(nh):
            sl = slice(h * HEAD, (h + 1) * HEAD)
            q = _rope(proj[:, sl], cos2, sin2).astype(BF)
            k = _rope(proj[:, BRANCH + h * HEAD:BRANCH + (h + 1) * HEAD], cos2, sin2)
            v = proj[:, 2 * BRANCH + h * HEAD:2 * BRANCH + (h + 1) * HEAD].astype(BF)
            g = proj[:, 3 * BRANCH + h * HEAD:3 * BRANCH + (h + 1) * HEAD]
            lg_f = lg_ref[0, h]
            lg_b = lg_ref[1, h]
            p = (_dot_nt(q, k.astype(BF)) * m_ref[h]).astype(BF)
            o = _dot(p, v)
            states = jnp.concatenate([sf_ref[h].astype(BF), sbs_ref[ci, h]], axis=1)
            r = _dot(q, states)
            o = o + jnp.exp(lg_f * (pos + 1.0)) * r[:, :HEAD] + jnp.exp(lg_b * (c_sz - pos)) * r[:, HEAD:]
            k_in = (k * (jnp.exp(lg_f * (c_sz - 1.0 - pos)) * scale)).astype(BF)
            sf_ref[h] = jnp.exp(lg_f * span) * sf_ref[h] + _dot_tn(k_in, v)
            y = _group_norm(o, True) * ng_ref[:, sl] * _silu(g)
            o_ref[0, :, sl] = y.astype(o_ref.dtype)


def _retention(x, w, log_rate, norm_g, cos2, sin2):
    b, t, d = x.shape
    nc = t // CHUNK
    nh = BRANCH // HEAD
    lg = -jnp.exp(log_rate.astype(F32))
    in_specs = [
        pl.BlockSpec(memory_space=pltpu.SMEM),
        _x_spec(nc, d),
        pl.BlockSpec((CHUNK, HEAD), lambda bi, ph, c: (_chunk_index(ph, c, nc), 0)),
        pl.BlockSpec((CHUNK, HEAD), lambda bi, ph, c: (_chunk_index(ph, c, nc), 0)),
        _full_spec(w.shape),
        _full_spec((1, BRANCH)),
    ]
    scratch = [
        pltpu.VMEM((nh, HEAD, HEAD), F32),
        pltpu.VMEM((nh, HEAD, HEAD), F32),
        pltpu.VMEM((nc, nh, HEAD, HEAD), BF),
        pltpu.VMEM((nh, CHUNK, CHUNK), F32),
    ]
    return _mixer_call(functools.partial(_ret_body, nc=nc), b, t, d,
                       (lg, x, cos2, sin2, w, norm_g.reshape(1, BRANCH)), in_specs, scratch,
                       "retention")


HGRN_LEVELS = int(math.log2(CHUNK))


def _hgrn_masks():
    import numpy as np
    c = CHUNK
    t = np.arange(c)
    mask = np.zeros((2, HGRN_LEVELS + 1, c, c), np.float32)
    for l in range(HGRN_LEVELS):
        s = 1 << l
        is_hi = ((t % (2 * s)) >= s)[:, None]
        same = (t[:, None] // (2 * s)) == (t[None, :] // (2 * s))
        mask[0, l] = same & is_hi & ~is_hi.T
        mask[1, l] = same & ~is_hi & is_hi.T
    mask[0, HGRN_LEVELS] = np.eye(c)
    return mask


def _split_exponent(cum, log_f, is_hi, l, reverse):
    c, w = cum.shape
    s = 1 << l
    if s == 1:
        return jnp.where(jnp.logical_not(is_hi) if reverse else is_hi, log_f, 0.0)
    edge = s if reverse else s - 1
    if 2 * s >= 8:
        blocks = cum.reshape(c // (2 * s), 2 * s, w)
        at_split = jnp.broadcast_to(blocks[:, edge:edge + 1, :], blocks.shape).reshape(c, w)
    else:
        tiles = cum.reshape(c // 8, 8, w)
        sub = _iota2(tiles.shape, 1)
        at_split = None
        for blk in range(8 // (2 * s)):
            r = blk * 2 * s + edge
            row = jnp.broadcast_to(tiles[:, r:r + 1, :], tiles.shape)
            at_split = row if at_split is None else jnp.where(sub >= blk * 2 * s, row, at_split)
        at_split = at_split.reshape(c, w)
    d = cum - at_split
    return jnp.where(is_hi, -d, d) if reverse else jnp.where(is_hi, d, -d)


def _hgrn_gates(z, lb):
    s = _sigmoid(z)
    log_f = jnp.log(lb + (1.0 - lb) * s)
    k = (1.0 - lb) * (1.0 - s)
    return k, log_f


def _hgrn_body(x_ref, w_ref, lb_ref, ng_ref, mask_ref, o_ref,
               sf_ref, sb_ref, sbs_ref, *, nc):
    c_sz = CHUNK
    nh = BRANCH // HEAD
    nl = HGRN_LEVELS
    phase = pl.program_id(1)
    c = pl.program_id(2)
    ci = _chunk_index(phase, c, nc)
    xb = x_ref[0].astype(BF)

    @pl.when(phase == 0)
    def _():
        @pl.when(c == 0)
        def _():
            sb_ref[...] = jnp.zeros_like(sb_ref)

        zv = _dot(xb, w_ref[:, 2 * BRANCH:4 * BRANCH])
        kb, lfb = _hgrn_gates(zv[:, :BRANCH], lb_ref[1:2, :])
        suf = _sel_dot(_tri_masks(c_sz)[1].astype(BF), lfb)
        total = suf[0:1, :]
        k_in = (kb * jnp.exp(total - suf)).astype(BF)
        dec = jnp.exp(total)
        v = zv[:, BRANCH:].astype(BF)
        for h in range(nh):
            sl = slice(h * HEAD, (h + 1) * HEAD)
            sbs_ref[ci, h] = sb_ref[h].astype(BF)
            sb_ref[h] = dec[:, sl] * sb_ref[h] + _dot_tn(v[:, sl], k_in[:, sl])

    @pl.when(phase == 1)
    def _():
        @pl.when(c == 0)
        def _():
            sf_ref[...] = jnp.zeros_like(sf_ref)

        proj = _dot(xb, w_ref[...])
        q = proj[:, :BRANCH]
        kf, lff = _hgrn_gates(proj[:, BRANCH:2 * BRANCH], lb_ref[0:1, :])
        kb, lfb = _hgrn_gates(proj[:, 2 * BRANCH:3 * BRANCH], lb_ref[1:2, :])
        v = proj[:, 3 * BRANCH:4 * BRANCH].astype(BF)
        g = proj[:, 4 * BRANCH:]

        scores = [None] * nh
        tok = _iota2((c_sz, BRANCH), 0)
        ksum = (kf + kb).astype(BF)
        qb = q.astype(BF)
        for h in range(nh):
            sl = slice(h * HEAD, (h + 1) * HEAD)
            scores[h] = _dot_nt(qb[:, sl], ksum[:, sl]) * mask_ref[0, nl]
        lower, upper = _tri_masks(c_sz)
        pre = _sel_dot(lower.astype(BF), lff)
        suf = _sel_dot(upper.astype(BF), lfb)
        for dr, (kd, lfd, cum) in enumerate(((kf, lff, pre), (kb, lfb, suf))):
            for l in range(nl):
                is_hi = (tok & (2 * (1 << l) - 1)) >= (1 << l)
                e = jnp.exp(_split_exponent(cum, lfd, is_hi, l, dr == 1))
                is_query = is_hi if dr == 0 else jnp.logical_not(is_hi)
                u = (jnp.where(is_query, q, kd) * e).astype(BF)
                for h in range(nh):
                    sl = slice(h * HEAD, (h + 1) * HEAD)
                    scores[h] = scores[h] + _dot_nt(u[:, sl], u[:, sl]) * mask_ref[dr, l]

        q_f = (q * jnp.exp(pre)).astype(BF)
        q_b = (q * jnp.exp(suf)).astype(BF)
        total = pre[c_sz - 1:c_sz, :]
        k_in = (kf * jnp.exp(total - pre)).astype(BF)
        dec = jnp.exp(total)
        for h in range(nh):
            sl = slice(h * HEAD, (h + 1) * HEAD)
            o = _dot(scores[h].astype(BF), v[:, sl])
            q_cat = jnp.concatenate([q_f[:, sl], q_b[:, sl]], axis=1)
            s_cat = jnp.concatenate([sf_ref[h].astype(BF), sbs_ref[ci, h]], axis=1)
            o = o + _dot_nt(q_cat, s_cat)
            sf_ref[h] = dec[:, sl] * sf_ref[h] + _dot_tn(v[:, sl], k_in[:, sl])
            y = _group_norm(o, False) * ng_ref[:, sl] * _silu(g[:, sl])
            o_ref[0, :, sl] = y.astype(o_ref.dtype)


def _hgrn2(x, w, lb, norm_g):
    b, t, d = x.shape
    nc = t // CHUNK
    nh = BRANCH // HEAD
    mask = jnp.asarray(_hgrn_masks(), F32)
    in_specs = [
        _x_spec(nc, d),
        _full_spec(w.shape),
        _full_spec((2, BRANCH)),
        _full_spec((1, BRANCH)),
        _full_spec(mask.shape),
    ]
    scratch = [
        pltpu.VMEM((nh, HEAD, HEAD), F32),
        pltpu.VMEM((nh, HEAD, HEAD), F32),
        pltpu.VMEM((nc, nh, HEAD, HEAD), BF),
    ]
    return _mixer_call(functools.partial(_hgrn_body, nc=nc), b, t, d,
                       (x, w, lb, norm_g.reshape(1, BRANCH), mask), in_specs, scratch,
                       "hgrn2")


ML_HEADS = 4
ML_AUG = 2 * HEAD


def _cummax_rows(a, reverse):
    n = a.shape[0]
    row = _iota2(a.shape, 0)
    k = 1
    while k < n:
        if reverse:
            shifted = pltpu.roll(a, n - k, axis=0)
            a = jnp.where(row < n - k, jnp.maximum(a, shifted), a)
        else:
            shifted = pltpu.roll(a, k, axis=0)
            a = jnp.where(row >= k, jnp.maximum(a, shifted), a)
        k *= 2
    return a


def _ml_body(x_ref, w_ref, wgc_ref, wgr_ref, bc_ref, br_ref, ng_ref, o_ref,
             cf_ref, cb_ref, cbs_ref, m_ref, mbs_ref, *, nc):
    c_sz = CHUNK
    nh = ML_HEADS
    phase = pl.program_id(1)
    c = pl.program_id(2)
    ci = _chunk_index(phase, c, nc)
    xb = x_ref[0].astype(BF)
    lane = _iota2((c_sz, HEAD), 1)
    is_fwd = lane < nh
    is_fwd_row = _iota2((1, HEAD), 1) < nh
    lower, upper = _tri_masks(c_sz)
    tri_lo = lower.astype(BF)
    tri_up = upper.astype(BF)

    gc = _dot(xb, wgc_ref[...]) + bc_ref[...]
    ipre = gc[:, :HEAD]
    logf = _log_sigmoid(gc[:, HEAD:])
    cum = jnp.where(is_fwd, _sel_dot(tri_lo, logf), _sel_dot(tri_up, logf))
    a = ipre - cum
    run_max = jnp.where(is_fwd, _cummax_rows(a, False), _cummax_rows(a, True))

    def edge(arr):
        return jnp.where(is_fwd_row, arr[c_sz - 1:c_sz, :], arr[0:1, :])

    def update_state(state_ref, lane0, k_all, v_aug, wk, sc):
        for h in range(nh):
            l = lane0 + h
            kw = (k_all[:, h * HEAD:(h + 1) * HEAD] * wk[:, l:l + 1]).astype(BF)
            state_ref[h] = sc[:, l:l + 1] * state_ref[h] + _dot_tn(kw, v_aug[h])

    ones_col = (lane == 0).astype(BF)

    @pl.when(phase == 0)
    def _():
        @pl.when(c == 0)
        def _():
            cb_ref[...] = jnp.zeros_like(cb_ref)
            m_ref[...] = jnp.zeros_like(m_ref)

        m_prev = m_ref[...]
        mbs_ref[ci] = m_prev
        for h in range(nh):
            cbs_ref[ci, h] = cb_ref[h].astype(BF)
        mu = jnp.maximum(m_prev, run_max)
        mu_e = edge(mu)
        kv = _dot(xb, w_ref[:, BRANCH:3 * BRANCH])
        v_aug = [jnp.concatenate([kv[:, BRANCH + h * HEAD:BRANCH + (h + 1) * HEAD].astype(BF), ones_col],
                                 axis=1) for h in range(nh)]
        update_state(cb_ref, nh, kv[:, :BRANCH], v_aug, jnp.exp(a - mu_e), jnp.exp(m_prev - mu_e))
        m_ref[...] = edge(cum) + mu_e

    @pl.when(phase == 1)
    def _():
        @pl.when(c == 0)
        def _():
            cf_ref[...] = jnp.zeros_like(cf_ref)
            m_ref[...] = jnp.zeros_like(m_ref)

        m_prev = jnp.where(is_fwd_row, m_ref[...], mbs_ref[ci])
        mu = jnp.maximum(m_prev, run_max)
        s_inter = jnp.exp(m_prev - mu)
        thr = jnp.exp(-(cum + mu))

        gr = _dot_nt(wgr_ref[...], xb) + br_ref[...]
        ipre_r = gr[:2 * nh]
        logf_r = _log_sigmoid(gr[2 * nh:])
        row8 = _iota2((2 * nh, c_sz), 0)
        cum_r = jnp.where(row8 < nh, _dot_sel(logf_r, tri_up), _dot_sel(logf_r, tri_lo))
        a_r = ipre_r - cum_r

        proj = _dot(xb, w_ref[...])
        v_aug = [jnp.concatenate([proj[:, 2 * BRANCH + h * HEAD:2 * BRANCH + (h + 1) * HEAD].astype(BF),
                                  ones_col], axis=1) for h in range(nh)]
        for h in range(nh):
            sl = slice(h * HEAD, (h + 1) * HEAD)
            q = proj[:, sl].astype(BF)
            k = proj[:, BRANCH + h * HEAD:BRANCH + (h + 1) * HEAD].astype(BF)
            qk = _dot_nt(q, k)
            states = jnp.concatenate([cf_ref[h].astype(BF), cbs_ref[ci, h]], axis=1)
            r = _dot(q, states)
            hsum = None
            for dr in range(2):
                l = dr * nh + h
                wgt = jnp.where(lower if dr == 0 else upper,
                                jnp.exp(a_r[l:l + 1, :] - mu[:, l:l + 1]), 0.0)
                tot = s_inter[:, l:l + 1] * r[:, dr * ML_AUG:(dr + 1) * ML_AUG] + _dot((qk * wgt).astype(BF), v_aug[h])
                den = jnp.maximum(jnp.abs(tot[:, HEAD:HEAD + 1]), thr[:, l:l + 1])
                out = tot[:, :HEAD] / den
                hsum = out if hsum is None else hsum + out
            o_gate = proj[:, 3 * BRANCH + h * HEAD:3 * BRANCH + (h + 1) * HEAD]
            g = proj[:, 4 * BRANCH + h * HEAD:4 * BRANCH + (h + 1) * HEAD]
            y = _group_norm(_sigmoid(o_gate) * hsum, True) * ng_ref[:, sl] * _silu(g)
            o_ref[0, :, sl] = y.astype(o_ref.dtype)

        mu_e = edge(mu)
        update_state(cf_ref, 0, proj[:, BRANCH:2 * BRANCH], v_aug, jnp.exp(a - mu_e), jnp.exp(m_prev - mu_e))
        m_ref[...] = edge(cum) + mu_e


def _mlstm(x, w, w_gates, i_bias, f_bias, norm_g):
    b, t, d = x.shape
    nc = t // CHUNK
    nh = ML_HEADS
    pad = HEAD - 2 * nh
    w_i = w_gates[:, :2 * nh]
    w_f = w_gates[:, 2 * nh:]
    wgc = jnp.concatenate([jnp.pad(w_i, ((0, 0), (0, pad))), jnp.pad(w_f, ((0, 0), (0, pad)))], axis=1)
    wgr = w_gates.T
    bias = jnp.concatenate([i_bias.reshape(-1), f_bias.reshape(-1)]).astype(F32)
    bc = jnp.concatenate([jnp.pad(bias[:2 * nh], (0, pad)), jnp.pad(bias[2 * nh:], (0, pad))]).reshape(1, 2 * HEAD)
    br = bias.reshape(4 * nh, 1)
    in_specs = [
        _x_spec(nc, d),
        _full_spec(w.shape),
        _full_spec(wgc.shape),
        _full_spec(wgr.shape),
        _full_spec(bc.shape),
        _full_spec(br.shape),
        _full_spec((1, BRANCH)),
    ]
    scratch = [
        pltpu.VMEM((nh, HEAD, ML_AUG), F32),
        pltpu.VMEM((nh, HEAD, ML_AUG), F32),
        pltpu.VMEM((nc, nh, HEAD, ML_AUG), BF),
        pltpu.VMEM((1, HEAD), F32),
        pltpu.VMEM((nc, 1, HEAD), F32),
    ]
    return _mixer_call(functools.partial(_ml_body, nc=nc), b, t, d,
                       (x, w, wgc.astype(BF), wgr.astype(BF), bc, br, norm_g.reshape(1, BRANCH)),
                       in_specs, scratch, "mlstm")


SSD_XBC = BRANCH + 2 * SSD_GROUPS * HEAD
SSD_GW = BRANCH // SSD_GROUPS


def _ssd_body(x_ref, xp_ref, xn_ref, wz_ref, wx_ref, wdc_ref, wdr_ref, dbc_ref, dbr_ref,
              ac_ref, ar_ref, cw_ref, cb_ref, dsk_ref, ng_ref, ex_ref, o_ref,
              xe_ref, sf_ref, sb_ref, sbs_ref, *, nc):
    c_sz = CHUNK
    nh = SSD_HEADS
    phase = pl.program_id(1)
    c = pl.program_id(2)
    ci = _chunk_index(phase, c, nc)
    xb = x_ref[0].astype(BF)
    lane = _iota2((c_sz, HEAD), 1)
    is_fwd = lane < nh
    lower, upper = _tri_masks(c_sz)
    tri_lo = lower.astype(BF)
    tri_up = upper.astype(BF)

    halo = jnp.concatenate([xp_ref[0, 0], xn_ref[0, 0]], axis=0).astype(BF)
    ph = _dot(halo, wx_ref[...])
    keep_prev = jnp.where(ci > 0, 1.0, 0.0)
    keep_next = jnp.where(ci < nc - 1, 1.0, 0.0)
    xe_ref[0:HALO, :] = ph[:HALO] * keep_prev
    xe_ref[HALO:HALO + c_sz, :] = _dot(xb, wx_ref[...])
    xe_ref[HALO + c_sz:, :] = ph[HALO:] * keep_next
    half = (SSD_CONV - 1) // 2
    conv = cb_ref[...]
    for k in range(SSD_CONV):
        conv = conv + cw_ref[k:k + 1, :] * xe_ref[pl.ds(HALO - half + k, c_sz), :]
    xbc = _silu(conv)
    xs = xbc[:, :BRANCH]
    bm = xbc[:, BRANCH:BRANCH + SSD_GROUPS * HEAD].astype(BF)
    cm = xbc[:, BRANCH + SSD_GROUPS * HEAD:].astype(BF)

    dt = _softplus(_dot(xb, wdc_ref[...]) + dbc_ref[...])
    ld = dt * ac_ref[...]
    cum = jnp.where(is_fwd, _sel_dot(tri_lo, ld), _sel_dot(tri_up, ld))
    cum_e = jnp.where(_iota2((1, HEAD), 1) < nh, cum[c_sz - 1:c_sz, :], cum[0:1, :])
    w_state = jnp.exp(cum_e - cum) * dt
    dec = jnp.exp(jnp.broadcast_to(cum_e, (8, HEAD)))

    def expand(a, dr):
        hi, mid, _ = _split3(a)
        e = ex_ref[:, dr * BRANCH:(dr + 1) * BRANCH]
        return _dot(hi, e) + _dot(mid, e)

    def update_state(state_ref, dr):
        xw = (xs * expand(w_state, dr)).astype(BF)
        d512 = expand(dec, dr)[0:1, :]
        for g in range(SSD_GROUPS):
            gs = slice(g * SSD_GW, (g + 1) * SSD_GW)
            state_ref[g] = d512[:, gs] * state_ref[g] + _dot_tn(bm[:, g * HEAD:(g + 1) * HEAD], xw[:, gs])

    @pl.when(phase == 0)
    def _():
        @pl.when(c == 0)
        def _():
            sb_ref[...] = jnp.zeros_like(sb_ref)

        for g in range(SSD_GROUPS):
            sbs_ref[ci, g] = sb_ref[g].astype(BF)
        update_state(sb_ref, 1)

    @pl.when(phase == 1)
    def _():
        @pl.when(c == 0)
        def _():
            sf_ref[...] = jnp.zeros_like(sf_ref)

        dt_r = _softplus(_dot_nt(wdr_ref[...], xb) + dbr_ref[...])
        ld_r = dt_r * ar_ref[...]
        row16 = _iota2((2 * nh, c_sz), 0)
        cum_r = jnp.where(row16 < nh, _dot_sel(ld_r, tri_up), _dot_sel(ld_r, tri_lo))
        z = _dot(xb, wz_ref[...])
        lane_half = _iota2((c_sz, HEAD), 1) < SSD_HEADDIM
        ecum_f = expand(jnp.exp(cum), 0)
        ecum_b = expand(jnp.exp(cum), 1)
        per = SSD_HEADS // SSD_GROUPS
        ys = []
        for g in range(SSD_GROUPS):
            cg = cm[:, g * HEAD:(g + 1) * HEAD]
            gmat = _dot_nt(cg, bm[:, g * HEAD:(g + 1) * HEAD])
            gs = slice(g * SSD_GW, (g + 1) * SSD_GW)
            inter = (ecum_f[:, gs] * _dot(cg, sf_ref[g].astype(BF)) +
                     ecum_b[:, gs] * _dot(cg, sbs_ref[ci, g]))
            pieces = []
            for pair in range(per // 2):
                acc = None
                for side in range(2):
                    h = g * per + 2 * pair + side
                    lf, lbk = h, nh + h
                    m = (jnp.where(lower, jnp.exp(cum[:, lf:lf + 1] - cum_r[lf:lf + 1, :]) * dt_r[lf:lf + 1, :], 0.0) +
                         jnp.where(upper, jnp.exp(cum[:, lbk:lbk + 1] - cum_r[lbk:lbk + 1, :]) * dt_r[lbk:lbk + 1, :], 0.0))
                    xpair = xs[:, (h // 2) * HEAD:(h // 2 + 1) * HEAD]
                    xh = jnp.where(lane_half if side == 0 else jnp.logical_not(lane_half), xpair, 0.0).astype(BF)
                    part = _dot((gmat * m).astype(BF), xh)
                    acc = part if acc is None else acc + part
                pieces.append(acc)
            ys.append(jnp.concatenate(pieces, axis=1) + inter)
        y = jnp.concatenate(ys, axis=1) + dsk_ref[...] * xs
        y = y * _silu(z)
        for g in range(SSD_GROUPS):
            gs = slice(g * SSD_GW, (g + 1) * SSD_GW)
            o_ref[0, :, gs] = (_group_norm(y[:, gs], False) * ng_ref[:, gs]).astype(o_ref.dtype)
        update_state(sf_ref, 0)


def _ssd(x, w_z, w_xbc, w_dt, conv_w, conv_b, a_log, dt_bias, d_skip, norm_g):
    b, t, d = x.shape
    nc = t // CHUNK
    nh = SSD_HEADS
    pad = HEAD - 2 * nh
    wdc = jnp.pad(w_dt, ((0, 0), (0, pad)))
    wdr = w_dt.T
    db = dt_bias.reshape(-1).astype(F32)
    a = (-jnp.exp(a_log.astype(F32))).reshape(-1)
    dbc = jnp.pad(db, (0, pad)).reshape(1, HEAD)
    ac = jnp.pad(a, (0, pad)).reshape(1, HEAD)
    dbr = db.reshape(2 * nh, 1)
    ar = a.reshape(2 * nh, 1)
    cw = jnp.pad(conv_w.astype(F32), ((0, 8 - SSD_CONV), (0, 0)))
    dsk = jnp.repeat(d_skip.astype(F32), SSD_HEADDIM).reshape(1, BRANCH)
    import numpy as np
    ex = np.zeros((HEAD, 2 * BRANCH), np.float32)
    for l in range(2 * nh):
        ex[l, l * SSD_HEADDIM:(l + 1) * SSD_HEADDIM] = 1.0
    ex = jnp.asarray(ex, BF)
    x8 = x.reshape(b, t // HALO, HALO, d)
    per = CHUNK // HALO
    last = t // HALO - 1
    in_specs = [
        _x_spec(nc, d),
        pl.BlockSpec((1, 1, HALO, d),
                     lambda bi, ph, c: (bi, jnp.maximum(_chunk_index(ph, c, nc) * per - 1, 0), 0, 0)),
        pl.BlockSpec((1, 1, HALO, d),
                     lambda bi, ph, c: (bi, jnp.minimum((_chunk_index(ph, c, nc) + 1) * per, last), 0, 0)),
        _full_spec(w_z.shape),
        _full_spec(w_xbc.shape),
        _full_spec(wdc.shape),
        _full_spec(wdr.shape),
        _full_spec(dbc.shape),
        _full_spec(dbr.shape),
        _full_spec(ac.shape),
        _full_spec(ar.shape),
        _full_spec(cw.shape),
        _full_spec((1, SSD_XBC)),
        _full_spec((1, BRANCH)),
        _full_spec((1, BRANCH)),
        _full_spec(ex.shape),
    ]
    scratch = [
        pltpu.VMEM((CHUNK + 2 * HALO, SSD_XBC), F32),
        pltpu.VMEM((SSD_GROUPS, HEAD, SSD_GW), F32),
        pltpu.VMEM((SSD_GROUPS, HEAD, SSD_GW), F32),
        pltpu.VMEM((nc, SSD_GROUPS, HEAD, SSD_GW), BF),
    ]
    return _mixer_call(functools.partial(_ssd_body, nc=nc), b, t, d,
                       (x, x8, x8, w_z, w_xbc, wdc.astype(BF), wdr.astype(BF), dbc, dbr, ac, ar, cw,
                        conv_b.astype(F32).reshape(1, SSD_XBC), dsk, norm_g.reshape(1, BRANCH), ex),
                       in_specs, scratch, "ssd")


def _out_body(x_ref, p_ref, m0_ref, m1_ref, m2_ref, m3_ref, wo_ref, lg_ref, lb_ref, wg_ref, wp_ref,
              o_ref, *, alpha):
    y = None
    for i, m_ref in enumerate((m0_ref, m1_ref, m2_ref, m3_ref)):
        part = _dot(m_ref[0], wo_ref[i * BRANCH:(i + 1) * BRANCH, :])
        y = part if y is None else y + part
    r = alpha * x_ref[0] + y
    mu = jnp.mean(r, axis=-1, keepdims=True)
    rc = r - mu
    var = jnp.mean(rc * rc, axis=-1, keepdims=True)
    xn = rc * lax.rsqrt(var + NORM_EPS) * lg_ref[...] + lb_ref[...]
    gate = _sigmoid(_dot(xn.astype(BF), wg_ref[...]))
    o_ref[0] = xn + gate * _dot(p_ref[0].astype(BF), wp_ref[...])


def _out_layer(x, p, mixed, w_out, ln_g, ln_b, w_gate, w_proj, alpha):
    b, t, d = x.shape
    dp = p.shape[-1]
    rows = OUT_ROWS
    row_spec = lambda width: pl.BlockSpec((1, rows, width), lambda bi, r: (bi, r, 0))
    full = lambda shape: pl.BlockSpec(shape, lambda bi, r: (0,) * len(shape))
    return pl.pallas_call(
        functools.partial(_out_body, alpha=alpha),
        grid=(b, t // rows),
        in_specs=[row_spec(d), row_spec(dp)] + [row_spec(BRANCH)] * 4 +
                 [full(w_out.shape), full((1, d)), full((1, d)), full(w_gate.shape), full(w_proj.shape)],
        out_specs=row_spec(d),
        out_shape=jax.ShapeDtypeStruct((b, t, d), F32),
        compiler_params=pltpu.CompilerParams(
            dimension_semantics=("arbitrary", "arbitrary"), vmem_limit_bytes=VMEM_LIMIT),
        name="out_proj",
    )(x, p, *mixed, w_out, ln_g.reshape(1, d), ln_b.reshape(1, d), w_gate, w_proj)


def _rope_tables(t):
    inv = ROPE_BASE ** (-jnp.arange(0, HEAD, 2, dtype=F32) / HEAD)
    ang = jnp.arange(t, dtype=F32)[:, None] * inv[None, :]
    cos = jnp.cos(ang)
    sin = jnp.sin(ang)
    return jnp.concatenate([cos, cos], axis=1), jnp.concatenate([-sin, sin], axis=1)


def _layer_weights(l, w_in, w_out, hgrn_lb_logits, w_ple_gate, w_ple_proj):
    ret_w = 4 * BRANCH
    hgrn_w = 5 * BRANCH
    ml_w = 5 * BRANCH + 4 * ML_HEADS
    wl = w_in[l]
    o1 = ret_w
    o2 = o1 + hgrn_w
    o3 = o2 + ml_w
    w_ret = wl[:, :o1].astype(BF)
    w_hgrn = wl[:, o1:o2].astype(BF)
    w_ml = wl[:, o2:o2 + 5 * BRANCH]
    k_scale = jnp.concatenate([jnp.ones((BRANCH,), F32), jnp.full((BRANCH,), HEAD ** -0.5, F32),
                               jnp.ones((3 * BRANCH,), F32)])
    w_ml = (w_ml * k_scale).astype(BF)
    w_ml_gates = wl[:, o2 + 5 * BRANCH:o3]
    w_z = wl[:, o3:o3 + BRANCH].astype(BF)
    w_xbc = wl[:, o3 + BRANCH:o3 + BRANCH + SSD_XBC].astype(BF)
    w_dt = wl[:, o3 + BRANCH + SSD_XBC:]
    lb_w = jax.nn.softmax(hgrn_lb_logits.astype(F32), axis=0)
    lb = (jnp.cumsum(lb_w, axis=0) - lb_w[0:1])[l]
    return dict(w_ret=w_ret, w_hgrn=w_hgrn, w_ml=w_ml, w_ml_gates=w_ml_gates, w_z=w_z, w_xbc=w_xbc,
                w_dt=w_dt, lb=lb, w_out=w_out[l].astype(BF), w_gate=w_ple_gate[l].astype(BF),
                w_proj=w_ple_proj[l].astype(BF))


def kernel(x_prompt, x_sample, p_prompt, p_sample, w_in, w_out, ln_g, ln_b, ret_log_rate, ret_norm_g,
           hgrn_lb_logits, hgrn_norm_g, mlstm_i_bias, mlstm_f_bias, mlstm_norm_g, ssd_conv_w, ssd_conv_b,
           ssd_a_log, ssd_dt_bias, ssd_d, ssd_norm_g, w_ple_gate, w_ple_proj):
    depth = w_in.shape[0]
    alpha = (2 * depth) ** 0.25
    weights = [_layer_weights(l, w_in, w_out, hgrn_lb_logits, w_ple_gate, w_ple_proj) for l in range(depth)]
    cos2, sin2 = _rope_tables(x_prompt.shape[1])

    def trunk(x, p):
        for l in range(depth):
            w = weights[l]
            mixed = (
                _retention(x, w["w_ret"], ret_log_rate[l], ret_norm_g[l], cos2, sin2),
                _hgrn2(x, w["w_hgrn"], w["lb"], hgrn_norm_g[l]),
                _mlstm(x, w["w_ml"], w["w_ml_gates"], mlstm_i_bias[l], mlstm_f_bias[l], mlstm_norm_g[l]),
                _ssd(x, w["w_z"], w["w_xbc"], w["w_dt"], ssd_conv_w[l], ssd_conv_b[l], ssd_a_log[l],
                     ssd_dt_bias[l], ssd_d[l], ssd_norm_g[l]),
            )
            x = _out_layer(x, p[l], mixed, w["w_out"], ln_g[l], ln_b[l], w["w_gate"], w["w_proj"], alpha)
        return x

    return trunk(x_prompt, p_prompt), trunk(x_sample, p_sample)
```

```python
import functools
import math

import numpy as np
import jax
import jax.numpy as jnp
from jax import lax
from jax.experimental import pallas as pl
from jax.experimental.pallas import tpu as pltpu

BF = jnp.bfloat16
F32 = jnp.float32

CHUNK = 128
TBLK = 512
HEAD = 128
BRANCH = 512
SSD_HEADDIM = 64
SSD_HEADS = 8
SSD_GROUPS = 2
SSD_CONV = 5
HALO = 16
ROPE_BASE = 10000.0
NORM_EPS = 1e-5
OUT_ROWS = 512
VMEM_LIMIT = 56 * 1024 * 1024


def _dot(a, b):
    return jnp.dot(a, b, preferred_element_type=F32)


def _dot_nt(a, b):
    return lax.dot_general(a, b, (((1,), (1,)), ((), ())), preferred_element_type=F32)


def _dot_tn(a, b):
    return lax.dot_general(a, b, (((0,), (0,)), ((), ())), preferred_element_type=F32)


def _split3(x):
    hi = x.astype(BF)
    r = x - hi.astype(F32)
    mid = r.astype(BF)
    lo = (r - mid.astype(F32)).astype(BF)
    return hi, mid, lo


def _sel_dot(sel, x):
    hi, mid, lo = _split3(x)
    return _dot(sel, hi) + _dot(sel, mid) + _dot(sel, lo)


def _dot_sel(x, sel):
    hi, mid, lo = _split3(x)
    return _dot(hi, sel) + _dot(mid, sel) + _dot(lo, sel)


def _sigmoid(x):
    return 1.0 / (1.0 + jnp.exp(-x))


def _silu(x):
    return x * _sigmoid(x)


def _softplus(x):
    return jnp.maximum(x, 0.0) + jnp.log1p(jnp.exp(-jnp.abs(x)))


def _log_sigmoid(x):
    return jnp.minimum(x, 0.0) - jnp.log1p(jnp.exp(-jnp.abs(x)))


def _iota2(shape, axis):
    return lax.broadcasted_iota(jnp.int32, shape, axis)


def _tri_masks(c):
    row = _iota2((c, c), 0)
    col = _iota2((c, c), 1)
    return col <= row, col >= row


def _prefix_suffix_rows(x):
    pre = _sel_dot(_tri_masks(x.shape[0])[0].astype(BF), x)
    return pre, pre[x.shape[0] - 1:, :] - pre + x


def _prefix_suffix_lanes(x):
    n = x.shape[1]
    pre = _dot_sel(x, _tri_masks(n)[1].astype(BF))
    return pre, pre[:, n - 1:] - pre + x


def _group_norm(o, center):
    if center:
        o = o - jnp.mean(o, axis=-1, keepdims=True)
    return o * lax.rsqrt(jnp.mean(o * o, axis=-1, keepdims=True) + NORM_EPS)


def _blk_index(phase, c, nblk):
    return jnp.where(phase == 0, nblk - 1 - c, c)


def _sweep(nblk, bwd_chunk, fwd_chunk, bwd_init, fwd_init):
    phase = pl.program_id(1)
    c = pl.program_id(2)
    per = TBLK // CHUNK

    @pl.when(phase == 0)
    def _():
        pl.when(c == 0)(bwd_init)
        blk = nblk - 1 - c

        def body(j, carry):
            jj = per - 1 - j
            bwd_chunk(pl.multiple_of(jj * CHUNK, CHUNK), blk * per + jj)
            return carry

        lax.fori_loop(0, per, body, 0)

    @pl.when(phase == 1)
    def _():
        pl.when(c == 0)(fwd_init)

        def body(j, carry):
            fwd_chunk(pl.multiple_of(j * CHUNK, CHUNK), c * per + j)
            return carry

        lax.fori_loop(0, per, body, 0)


def _mixer_call(body, b, t, in_arrays, in_specs, scratch, name):
    nblk = t // TBLK
    return pl.pallas_call(
        functools.partial(body, nblk=nblk),
        grid=(b, 2, nblk),
        in_specs=in_specs,
        out_specs=pl.BlockSpec((1, TBLK, BRANCH),
                               lambda bi, ph, c: (bi, jnp.where(ph == 0, 0, c), 0)),
        out_shape=jax.ShapeDtypeStruct((b, t, BRANCH), BF),
        scratch_shapes=scratch,
        compiler_params=pltpu.CompilerParams(
            dimension_semantics=("arbitrary", "arbitrary", "arbitrary"),
            vmem_limit_bytes=VMEM_LIMIT),
        name=name,
    )(*in_arrays)


def _x_spec(nblk, d):
    return pl.BlockSpec((1, TBLK, d), lambda bi, ph, c: (bi, _blk_index(ph, c, nblk), 0))


def _full_spec(shape):
    zeros = (0,) * len(shape)
    return pl.BlockSpec(shape, lambda bi, ph, c: zeros)


def _rows(ci):
    return pl.ds(pl.multiple_of(ci * CHUNK, CHUNK), CHUNK)


def _rope(x, cos2, sin2):
    return x * cos2 + pltpu.roll(x, HEAD // 2, axis=1) * sin2


def _ret_body(lg_ref, x_ref, cos_ref, sin_ref, wkv_ref, wqg_ref, ng_ref, o_ref,
              sf_ref, sb_ref, sbs_ref, m_ref, kv_ref, *, nblk):
    c_sz = CHUNK
    nh = BRANCH // HEAD
    scale = HEAD ** -0.5

    def bwd_init():
        sb_ref[...] = jnp.zeros_like(sb_ref)

    def fwd_init():
        sf_ref[...] = jnp.zeros_like(sf_ref)
        lower, upper = _tri_masks(c_sz)
        dist = (_iota2((c_sz, c_sz), 0) - _iota2((c_sz, c_sz), 1)).astype(F32)
        for h in range(nh):
            fwd = jnp.where(lower, jnp.exp(lg_ref[0, h] * dist), 0.0)
            bwd = jnp.where(upper, jnp.exp(-lg_ref[1, h] * dist), 0.0)
            m_ref[h] = (fwd + bwd) * scale

    def bwd_chunk(r0, ci):
        rows = pl.ds(r0, c_sz)
        pos = _iota2((c_sz, HEAD), 0).astype(F32)
        span = jnp.full((1, HEAD), float(c_sz), F32)
        cos2 = cos_ref[rows, :]
        sin2 = sin_ref[rows, :]
        kv = _dot(x_ref[0, rows, :], wkv_ref[...])
        for h in range(nh):
            k = _rope(kv[:, h * HEAD:(h + 1) * HEAD], cos2, sin2)
            v = kv[:, BRANCH + h * HEAD:BRANCH + (h + 1) * HEAD].astype(BF)
            kv_ref[_rows(ci), h * HEAD:(h + 1) * HEAD] = k.astype(BF)
            kv_ref[_rows(ci), BRANCH + h * HEAD:BRANCH + (h + 1) * HEAD] = v
            lg_b = lg_ref[1, h]
            sbs_ref[ci, h] = sb_ref[h].astype(BF)
            k_in = (k * (jnp.exp(lg_b * pos) * scale)).astype(BF)
            sb_ref[h] = jnp.exp(lg_b * span) * sb_ref[h] + _dot_tn(k_in, v)

    def fwd_chunk(r0, ci):
        rows = pl.ds(r0, c_sz)
        pos = _iota2((c_sz, HEAD), 0).astype(F32)
        span = jnp.full((1, HEAD), float(c_sz), F32)
        cos2 = cos_ref[rows, :]
        sin2 = sin_ref[rows, :]
        qg = _dot(x_ref[0, rows, :], wqg_ref[...])
        for h in range(nh):
            sl = slice(h * HEAD, (h + 1) * HEAD)
            q = _rope(qg[:, sl], cos2, sin2).astype(BF)
            g = qg[:, BRANCH + h * HEAD:BRANCH + (h + 1) * HEAD]
            k = kv_ref[_rows(ci), sl]
            v = kv_ref[_rows(ci), BRANCH + h * HEAD:BRANCH + (h + 1) * HEAD]
            lg_f = lg_ref[0, h]
            lg_b = lg_ref[1, h]
            p = (_dot_nt(q, k) * m_ref[h]).astype(BF)
            o = _dot(p, v)
            states = jnp.concatenate([sf_ref[h].astype(BF), sbs_ref[ci, h]], axis=1)
            r = _dot(q, states)
            o = o + jnp.exp(lg_f * (pos + 1.0)) * r[:, :HEAD] + jnp.exp(lg_b * (c_sz - pos)) * r[:, HEAD:]
            k_in = (k.astype(F32) * (jnp.exp(lg_f * (c_sz - 1.0 - pos)) * scale)).astype(BF)
            sf_ref[h] = jnp.exp(lg_f * span) * sf_ref[h] + _dot_tn(k_in, v)
            y = _group_norm(o, True) * ng_ref[:, sl] * _silu(g)
            o_ref[0, rows, sl] = y.astype(o_ref.dtype)

    _sweep(nblk, bwd_chunk, fwd_chunk, bwd_init, fwd_init)


def _retention(x, w_kv, w_qg, log_rate, norm_g, cos2, sin2):
    b, t, d = x.shape
    nblk = t // TBLK
    nc = t // CHUNK
    nh = BRANCH // HEAD
    lg = -jnp.exp(log_rate.astype(F32))
    in_specs = [
        pl.BlockSpec(memory_space=pltpu.SMEM),
        _x_spec(nblk, d),
        pl.BlockSpec((TBLK, HEAD), lambda bi, ph, c: (_blk_index(ph, c, nblk), 0)),
        pl.BlockSpec((TBLK, HEAD), lambda bi, ph, c: (_blk_index(ph, c, nblk), 0)),
        _full_spec(w_kv.shape),
        _full_spec(w_qg.shape),
        _full_spec((1, BRANCH)),
    ]
    scratch = [
        pltpu.VMEM((nh, HEAD, HEAD), F32),
        pltpu.VMEM((nh, HEAD, HEAD), F32),
        pltpu.VMEM((nc, nh, HEAD, HEAD), BF),
        pltpu.VMEM((nh, CHUNK, CHUNK), F32),
        pltpu.VMEM((t, 2 * BRANCH), BF),
    ]
    return _mixer_call(_ret_body, b, t, (lg, x, cos2, sin2, w_kv, w_qg, norm_g.reshape(1, BRANCH)),
                       in_specs, scratch, "retention")


HGRN_LEVELS = int(math.log2(CHUNK))


def _hgrn_masks():
    c = CHUNK
    t = np.arange(c)
    mask = np.zeros((2, HGRN_LEVELS + 1, c, c), np.float32)
    for l in range(HGRN_LEVELS):
        s = 1 << l
        is_hi = ((t % (2 * s)) >= s)[:, None]
        same = (t[:, None] // (2 * s)) == (t[None, :] // (2 * s))
        mask[0, l] = same & is_hi & ~is_hi.T
        mask[1, l] = same & ~is_hi & is_hi.T
    mask[0, HGRN_LEVELS] = np.eye(c)
    return mask


def _split_exponent(cum, log_f, is_hi, l, reverse):
    c, w = cum.shape
    s = 1 << l
    if s == 1:
        return jnp.where(jnp.logical_not(is_hi) if reverse else is_hi, log_f, 0.0)
    edge = s if reverse else s - 1
    if 2 * s >= 8:
        blocks = cum.reshape(c // (2 * s), 2 * s, w)
        at_split = jnp.broadcast_to(blocks[:, edge:edge + 1, :], blocks.shape).reshape(c, w)
    else:
        tiles = cum.reshape(c // 8, 8, w)
        sub = _iota2(tiles.shape, 1)
        at_split = None
        for blk in range(8 // (2 * s)):
            r = blk * 2 * s + edge
            row = jnp.broadcast_to(tiles[:, r:r + 1, :], tiles.shape)
            at_split = row if at_split is None else jnp.where(sub >= blk * 2 * s, row, at_split)
        at_split = at_split.reshape(c, w)
    d = cum - at_split
    return jnp.where(is_hi, -d, d) if reverse else jnp.where(is_hi, d, -d)


def _hgrn_gates(z, lb):
    s = _sigmoid(z)
    log_f = jnp.log(lb + (1.0 - lb) * s)
    k = (1.0 - lb) * (1.0 - s)
    return k, log_f


def _hgrn_body(x_ref, wzv_ref, wq_ref, lb_ref, ng_ref, mask_ref, o_ref,
               sf_ref, sb_ref, sbs_ref, v_ref, *, nblk):
    c_sz = CHUNK
    nh = BRANCH // HEAD
    nl = HGRN_LEVELS

    def bwd_init():
        sb_ref[...] = jnp.zeros_like(sb_ref)

    def fwd_init():
        sf_ref[...] = jnp.zeros_like(sf_ref)

    def bwd_chunk(r0, ci):
        zv = _dot(x_ref[0, pl.ds(r0, c_sz), :], wzv_ref[...])
        kb, lfb = _hgrn_gates(zv[:, :BRANCH], lb_ref[1:2, :])
        _, suf = _prefix_suffix_rows(lfb)
        total = suf[0:1, :]
        k_in = (kb * jnp.exp(total - suf)).astype(BF)
        dec = jnp.exp(total)
        v = zv[:, BRANCH:].astype(BF)
        v_ref[_rows(ci), :] = v
        for h in range(nh):
            sl = slice(h * HEAD, (h + 1) * HEAD)
            sbs_ref[ci, h] = sb_ref[h].astype(BF)
            sb_ref[h] = dec[:, sl] * sb_ref[h] + _dot_tn(v[:, sl], k_in[:, sl])

    def fwd_chunk(r0, ci):
        rows = pl.ds(r0, c_sz)
        proj = _dot(x_ref[0, rows, :], wq_ref[...])
        q = proj[:, :BRANCH]
        kf, lff = _hgrn_gates(proj[:, BRANCH:2 * BRANCH], lb_ref[0:1, :])
        kb, lfb = _hgrn_gates(proj[:, 2 * BRANCH:3 * BRANCH], lb_ref[1:2, :])
        g = proj[:, 3 * BRANCH:]
        v = v_ref[_rows(ci), :]

        scores = [None] * nh
        tok = _iota2((c_sz, BRANCH), 0)
        ksum = (kf + kb).astype(BF)
        qb = q.astype(BF)
        for h in range(nh):
            sl = slice(h * HEAD, (h + 1) * HEAD)
            scores[h] = _dot_nt(qb[:, sl], ksum[:, sl]) * mask_ref[0, nl]
        pre, _ = _prefix_suffix_rows(lff)
        _, suf = _prefix_suffix_rows(lfb)
        for dr, (kd, lfd, cum) in enumerate(((kf, lff, pre), (kb, lfb, suf))):
            for l in range(nl):
                is_hi = (tok & (2 * (1 << l) - 1)) >= (1 << l)
                e = jnp.exp(_split_exponent(cum, lfd, is_hi, l, dr == 1))
                is_query = is_hi if dr == 0 else jnp.logical_not(is_hi)
                u = (jnp.where(is_query, q, kd) * e).astype(BF)
                for h in range(nh):
                    sl = slice(h * HEAD, (h + 1) * HEAD)
                    scores[h] = scores[h] + _dot_nt(u[:, sl], u[:, sl]) * mask_ref[dr, l]

        q_f = (q * jnp.exp(pre)).astype(BF)
        q_b = (q * jnp.exp(suf)).astype(BF)
        total = pre[c_sz - 1:c_sz, :]
        k_in = (kf * jnp.exp(total - pre)).astype(BF)
        dec = jnp.exp(total)
        for h in range(nh):
            sl = slice(h * HEAD, (h + 1) * HEAD)
            o = _dot(scores[h].astype(BF), v[:, sl])
            q_cat = jnp.concatenate([q_f[:, sl], q_b[:, sl]], axis=1)
            s_cat = jnp.concatenate([sf_ref[h].astype(BF), sbs_ref[ci, h]], axis=1)
            o = o + _dot_nt(q_cat, s_cat)
            sf_ref[h] = dec[:, sl] * sf_ref[h] + _dot_tn(v[:, sl], k_in[:, sl])
            y = _group_norm(o, False) * ng_ref[:, sl] * _silu(g[:, sl])
            o_ref[0, rows, sl] = y.astype(o_ref.dtype)

    _sweep(nblk, bwd_chunk, fwd_chunk, bwd_init, fwd_init)


def _hgrn2(x, w_zv, w_q, lb, norm_g):
    b, t, d = x.shape
    nblk = t // TBLK
    nc = t // CHUNK
    nh = BRANCH // HEAD
    mask = jnp.asarray(_hgrn_masks(), F32)
    in_specs = [
        _x_spec(nblk, d),
        _full_spec(w_zv.shape),
        _full_spec(w_q.shape),
        _full_spec((2, BRANCH)),
        _full_spec((1, BRANCH)),
        _full_spec(mask.shape),
    ]
    scratch = [
        pltpu.VMEM((nh, HEAD, HEAD), F32),
        pltpu.VMEM((nh, HEAD, HEAD), F32),
        pltpu.VMEM((nc, nh, HEAD, HEAD), BF),
        pltpu.VMEM((t, BRANCH), BF),
    ]
    return _mixer_call(_hgrn_body, b, t, (x, w_zv, w_q, lb, norm_g.reshape(1, BRANCH), mask),
                       in_specs, scratch, "hgrn2")


ML_HEADS = 4
ML_AUG = 2 * HEAD


def _cummax_rows(a, reverse):
    n = a.shape[0]
    row = _iota2(a.shape, 0)
    k = 1
    while k < n:
        if reverse:
            shifted = pltpu.roll(a, n - k, axis=0)
            a = jnp.where(row < n - k, jnp.maximum(a, shifted), a)
        else:
            shifted = pltpu.roll(a, k, axis=0)
            a = jnp.where(row >= k, jnp.maximum(a, shifted), a)
        k *= 2
    return a


def _ml_body(x_ref, wkv_ref, wq_ref, wgc_ref, wgr_ref, bc_ref, br_ref, ng_ref, o_ref,
             cf_ref, cb_ref, cbs_ref, m_ref, mbs_ref, kv_ref, *, nblk):
    c_sz = CHUNK
    nh = ML_HEADS

    def gates(xb):
        is_fwd = _iota2((c_sz, HEAD), 1) < nh
        gc = _dot(xb, wgc_ref[...]) + bc_ref[...]
        ipre = gc[:, :HEAD]
        pre, suf = _prefix_suffix_rows(_log_sigmoid(gc[:, HEAD:]))
        cum = jnp.where(is_fwd, pre, suf)
        a = ipre - cum
        run_max = jnp.where(is_fwd, _cummax_rows(a, False), _cummax_rows(a, True))
        return a, cum, run_max

    def edge(arr):
        return jnp.where(_iota2((1, HEAD), 1) < nh, arr[c_sz - 1:c_sz, :], arr[0:1, :])

    def ones_col():
        return (_iota2((c_sz, HEAD), 1) == 0).astype(BF)

    def update_state(state_ref, lane0, k_all, v_aug, wk, sc):
        for h in range(nh):
            l = lane0 + h
            kw = (k_all[:, h * HEAD:(h + 1) * HEAD] * wk[:, l:l + 1]).astype(BF)
            state_ref[h] = sc[:, l:l + 1] * state_ref[h] + _dot_tn(kw, v_aug[h])

    def bwd_init():
        cb_ref[...] = jnp.zeros_like(cb_ref)
        m_ref[...] = jnp.zeros_like(m_ref)

    def fwd_init():
        cf_ref[...] = jnp.zeros_like(cf_ref)
        m_ref[...] = jnp.zeros_like(m_ref)

    def bwd_chunk(r0, ci):
        xb = x_ref[0, pl.ds(r0, c_sz), :]
        a, cum, run_max = gates(xb)
        m_prev = m_ref[...]
        mbs_ref[ci] = m_prev
        for h in range(nh):
            cbs_ref[ci, h] = cb_ref[h].astype(BF)
        mu_e = edge(jnp.maximum(m_prev, run_max))
        kv = _dot(xb, wkv_ref[...])
        kv_ref[_rows(ci), :] = kv.astype(BF)
        v_aug = [jnp.concatenate([kv[:, BRANCH + h * HEAD:BRANCH + (h + 1) * HEAD].astype(BF), ones_col()],
                                 axis=1) for h in range(nh)]
        update_state(cb_ref, nh, kv[:, :BRANCH], v_aug, jnp.exp(a - mu_e), jnp.exp(m_prev - mu_e))
        m_ref[...] = edge(cum) + mu_e

    def fwd_chunk(r0, ci):
        rows = pl.ds(r0, c_sz)
        xb = x_ref[0, rows, :]
        a, cum, run_max = gates(xb)
        lower, upper = _tri_masks(c_sz)
        m_prev = jnp.where(_iota2((1, HEAD), 1) < nh, m_ref[...], mbs_ref[ci])
        mu = jnp.maximum(m_prev, run_max)
        s_inter = jnp.exp(m_prev - mu)
        thr = jnp.exp(-(cum + mu))

        gr = _dot_nt(wgr_ref[...], xb) + br_ref[...]
        ipre_r = gr[:2 * nh]
        pre_r, suf_r = _prefix_suffix_lanes(_log_sigmoid(gr[2 * nh:]))
        a_r = ipre_r - jnp.where(_iota2((2 * nh, c_sz), 0) < nh, pre_r, suf_r)

        proj = _dot(xb, wq_ref[...])
        kvc = kv_ref[_rows(ci), :]
        v_aug = [jnp.concatenate([kvc[:, BRANCH + h * HEAD:BRANCH + (h + 1) * HEAD], ones_col()], axis=1)
                 for h in range(nh)]
        for h in range(nh):
            sl = slice(h * HEAD, (h + 1) * HEAD)
            q = proj[:, sl].astype(BF)
            qk = _dot_nt(q, kvc[:, sl])
            states = jnp.concatenate([cf_ref[h].astype(BF), cbs_ref[ci, h]], axis=1)
            r = _dot(q, states)
            hsum = None
            for dr in range(2):
                l = dr * nh + h
                wgt = jnp.where(lower if dr == 0 else upper,
                                jnp.exp(a_r[l:l + 1, :] - mu[:, l:l + 1]), 0.0)
                tot = s_inter[:, l:l + 1] * r[:, dr * ML_AUG:(dr + 1) * ML_AUG] + _dot((qk * wgt).astype(BF), v_aug[h])
                den = jnp.maximum(jnp.abs(tot[:, HEAD:HEAD + 1]), thr[:, l:l + 1])
                out = tot[:, :HEAD] / den
                hsum = out if hsum is None else hsum + out
            o_gate = proj[:, BRANCH + h * HEAD:BRANCH + (h + 1) * HEAD]
            g = proj[:, 2 * BRANCH + h * HEAD:2 * BRANCH + (h + 1) * HEAD]
            y = _group_norm(_sigmoid(o_gate) * hsum, True) * ng_ref[:, sl] * _silu(g)
            o_ref[0, rows, sl] = y.astype(o_ref.dtype)

        mu_e = edge(mu)
        update_state(cf_ref, 0, kvc[:, :BRANCH].astype(F32), v_aug, jnp.exp(a - mu_e), jnp.exp(m_prev - mu_e))
        m_ref[...] = edge(cum) + mu_e

    _sweep(nblk, bwd_chunk, fwd_chunk, bwd_init, fwd_init)


def _mlstm(x, w_kv, w_q, w_gates, i_bias, f_bias, norm_g):
    b, t, d = x.shape
    nblk = t // TBLK
    nc = t // CHUNK
    nh = ML_HEADS
    pad = HEAD - 2 * nh
    w_i = w_gates[:, :2 * nh]
    w_f = w_gates[:, 2 * nh:]
    wgc = jnp.concatenate([jnp.pad(w_i, ((0, 0), (0, pad))), jnp.pad(w_f, ((0, 0), (0, pad)))], axis=1)
    wgr = w_gates.T
    bias = jnp.concatenate([i_bias.reshape(-1), f_bias.reshape(-1)]).astype(F32)
    bc = jnp.concatenate([jnp.pad(bias[:2 * nh], (0, pad)), jnp.pad(bias[2 * nh:], (0, pad))]).reshape(1, 2 * HEAD)
    br = bias.reshape(4 * nh, 1)
    in_specs = [
        _x_spec(nblk, d),
        _full_spec(w_kv.shape),
        _full_spec(w_q.shape),
        _full_spec(wgc.shape),
        _full_spec(wgr.shape),
        _full_spec(bc.shape),
        _full_spec(br.shape),
        _full_spec((1, BRANCH)),
    ]
    scratch = [
        pltpu.VMEM((nh, HEAD, ML_AUG), F32),
        pltpu.VMEM((nh, HEAD, ML_AUG), F32),
        pltpu.VMEM((nc, nh, HEAD, ML_AUG), BF),
        pltpu.VMEM((1, HEAD), F32),
        pltpu.VMEM((nc, 1, HEAD), F32),
        pltpu.VMEM((t, 2 * BRANCH), BF),
    ]
    return _mixer_call(_ml_body, b, t,
                       (x, w_kv, w_q, wgc.astype(BF), wgr.astype(BF), bc, br, norm_g.reshape(1, BRANCH)),
                       in_specs, scratch, "mlstm")


SSD_XBC = BRANCH + 2 * SSD_GROUPS * HEAD
SSD_GW = BRANCH // SSD_GROUPS


def _ssd_body(x_ref, xp_ref, xn_ref, wz_ref, wx_ref, wdc_ref, wdr_ref, dbc_ref, dbr_ref,
              ac_ref, ar_ref, cw_ref, cb_ref, dsk_ref, ng_ref, ex_ref, o_ref,
              xe_ref, sf_ref, sb_ref, sbs_ref, act_ref, *, nblk):
    c_sz = CHUNK
    nh = SSD_HEADS
    per = TBLK // CHUNK

    def dt_columns(xb):
        is_fwd = _iota2((c_sz, HEAD), 1) < nh
        dt = _softplus(_dot(xb, wdc_ref[...]) + dbc_ref[...])
        pre, suf = _prefix_suffix_rows(dt * ac_ref[...])
        return dt, jnp.where(is_fwd, pre, suf)

    def expand(a, dr):
        hi, mid, _ = _split3(a)
        e = ex_ref[:, dr * BRANCH:(dr + 1) * BRANCH]
        return _dot(hi, e) + _dot(mid, e)

    def update_state(state_ref, dr, xs, bm, dt, cum):
        cum_e = jnp.where(_iota2((1, HEAD), 1) < nh, cum[c_sz - 1:c_sz, :], cum[0:1, :])
        w_state = jnp.exp(cum_e - cum) * dt
        dec = jnp.exp(jnp.broadcast_to(cum_e, (8, HEAD)))
        xw = (xs * expand(w_state, dr)).astype(BF)
        d512 = expand(dec, dr)[0:1, :]
        for g in range(SSD_GROUPS):
            gs = slice(g * SSD_GW, (g + 1) * SSD_GW)
            state_ref[g] = d512[:, gs] * state_ref[g] + _dot_tn(bm[:, g * HEAD:(g + 1) * HEAD], xw[:, gs])

    def bwd_init():
        sb_ref[...] = jnp.zeros_like(sb_ref)

    def fwd_init():
        sf_ref[...] = jnp.zeros_like(sf_ref)

    def project_block():
        blk = nblk - 1 - pl.program_id(2)

        def body(j, carry):
            r0 = pl.multiple_of(j * c_sz, c_sz)
            xe_ref[pl.ds(HALO + r0, c_sz), :] = _dot(x_ref[0, pl.ds(r0, c_sz), :], wx_ref[...])
            return carry

        lax.fori_loop(0, per, body, 0)
        halo = jnp.concatenate([xp_ref[0, 0], xn_ref[0, 0]], axis=0)
        ph = _dot(halo, wx_ref[...])
        xe_ref[0:HALO, :] = ph[:HALO] * jnp.where(blk > 0, 1.0, 0.0)
        xe_ref[HALO + TBLK:, :] = ph[HALO:] * jnp.where(blk < nblk - 1, 1.0, 0.0)

    def bwd_chunk(r0, ci):
        half = (SSD_CONV - 1) // 2
        win = xe_ref[pl.ds(r0, c_sz + 2 * HALO), :]
        conv = cb_ref[...]
        for k in range(SSD_CONV):
            conv = conv + cw_ref[k:k + 1, :] * win[HALO - half + k:HALO - half + k + c_sz, :]
        act = _silu(conv)
        act_ref[_rows(ci), :] = act.astype(BF)
        dt, cum = dt_columns(x_ref[0, pl.ds(r0, c_sz), :])
        for g in range(SSD_GROUPS):
            sbs_ref[ci, g] = sb_ref[g].astype(BF)
        update_state(sb_ref, 1, act[:, :BRANCH], act[:, BRANCH:BRANCH + SSD_GROUPS * HEAD].astype(BF), dt, cum)

    def fwd_chunk(r0, ci):
        rows = pl.ds(r0, c_sz)
        xb = x_ref[0, rows, :]
        lower, upper = _tri_masks(c_sz)
        act = act_ref[_rows(ci), :]
        xs = act[:, :BRANCH].astype(F32)
        bm = act[:, BRANCH:BRANCH + SSD_GROUPS * HEAD]
        cm = act[:, BRANCH + SSD_GROUPS * HEAD:]
        dt, cum = dt_columns(xb)
        dt_r = _softplus(_dot_nt(wdr_ref[...], xb) + dbr_ref[...])
        pre_r, suf_r = _prefix_suffix_lanes(dt_r * ar_ref[...])
        cum_r = jnp.where(_iota2((2 * nh, c_sz), 0) < nh, pre_r, suf_r)
        z = _dot(xb, wz_ref[...])
        lane_half = _iota2((c_sz, HEAD), 1) < SSD_HEADDIM
        ecum = jnp.exp(cum)
        ecum_f = expand(ecum, 0)
        ecum_b = expand(ecum, 1)
        hpg = SSD_HEADS // SSD_GROUPS
        ys = []
        for g in range(SSD_GROUPS):
            cg = cm[:, g * HEAD:(g + 1) * HEAD]
            gmat = _dot_nt(cg, bm[:, g * HEAD:(g + 1) * HEAD])
            gs = slice(g * SSD_GW, (g + 1) * SSD_GW)
            inter = (ecum_f[:, gs] * _dot(cg, sf_ref[g].astype(BF)) +
                     ecum_b[:, gs] * _dot(cg, sbs_ref[ci, g]))
            pieces = []
            for pair in range(hpg // 2):
                acc = None
                for side in range(2):
                    h = g * hpg + 2 * pair + side
                    lf, lbk = h, nh + h
                    m = (jnp.where(lower, jnp.exp(cum[:, lf:lf + 1] - cum_r[lf:lf + 1, :]) * dt_r[lf:lf + 1, :], 0.0) +
                         jnp.where(upper, jnp.exp(cum[:, lbk:lbk + 1] - cum_r[lbk:lbk + 1, :]) * dt_r[lbk:lbk + 1, :], 0.0))
                    xpair = xs[:, (h // 2) * HEAD:(h // 2 + 1) * HEAD]
                    xh = jnp.where(lane_half if side == 0 else jnp.logical_not(lane_half), xpair, 0.0).astype(BF)
                    part = _dot((gmat * m).astype(BF), xh)
                    acc = part if acc is None else acc + part
                pieces.append(acc)
            ys.append(jnp.concatenate(pieces, axis=1) + inter)
        y = jnp.concatenate(ys, axis=1) + dsk_ref[...] * xs
        y = y * _silu(z)
        for g in range(SSD_GROUPS):
            gs = slice(g * SSD_GW, (g + 1) * SSD_GW)
            o_ref[0, rows, gs] = (_group_norm(y[:, gs], False) * ng_ref[:, gs]).astype(o_ref.dtype)
        update_state(sf_ref, 0, xs, bm, dt, cum)

    pl.when(pl.program_id(1) == 0)(project_block)
    _sweep(nblk, bwd_chunk, fwd_chunk, bwd_init, fwd_init)


def _ssd(x, w_z, w_xbc, w_dt, conv_w, conv_b, a_log, dt_bias, d_skip, norm_g):
    b, t, d = x.shape
    nblk = t // TBLK
    nc = t // CHUNK
    nh = SSD_HEADS
    pad = HEAD - 2 * nh
    wdc = jnp.pad(w_dt, ((0, 0), (0, pad)))
    wdr = w_dt.T
    db = dt_bias.reshape(-1).astype(F32)
    a = (-jnp.exp(a_log.astype(F32))).reshape(-1)
    dbc = jnp.pad(db, (0, pad)).reshape(1, HEAD)
    ac = jnp.pad(a, (0, pad)).reshape(1, HEAD)
    dbr = db.reshape(2 * nh, 1)
    ar = a.reshape(2 * nh, 1)
    cw = jnp.pad(conv_w.astype(F32), ((0, 8 - SSD_CONV), (0, 0)))
    dsk = jnp.repeat(d_skip.astype(F32), SSD_HEADDIM).reshape(1, BRANCH)
    ex = np.zeros((HEAD, 2 * BRANCH), np.float32)
    for l in range(2 * nh):
        ex[l, l * SSD_HEADDIM:(l + 1) * SSD_HEADDIM] = 1.0
    ex = jnp.asarray(ex, BF)
    xh = x.reshape(b, t // HALO, HALO, d)
    per = TBLK // HALO
    last = t // HALO - 1
    in_specs = [
        _x_spec(nblk, d),
        pl.BlockSpec((1, 1, HALO, d),
                     lambda bi, ph, c: (bi, jnp.maximum(_blk_index(ph, c, nblk) * per - 1, 0), 0, 0)),
        pl.BlockSpec((1, 1, HALO, d),
                     lambda bi, ph, c: (bi, jnp.minimum((_blk_index(ph, c, nblk) + 1) * per, last), 0, 0)),
        _full_spec(w_z.shape),
        _full_spec(w_xbc.shape),
        _full_spec(wdc.shape),
        _full_spec(wdr.shape),
        _full_spec(dbc.shape),
        _full_spec(dbr.shape),
        _full_spec(ac.shape),
        _full_spec(ar.shape),
        _full_spec(cw.shape),
        _full_spec((1, SSD_XBC)),
        _full_spec((1, BRANCH)),
        _full_spec((1, BRANCH)),
        _full_spec(ex.shape),
    ]
    scratch = [
        pltpu.VMEM((TBLK + 2 * HALO, SSD_XBC), F32),
        pltpu.VMEM((SSD_GROUPS, HEAD, SSD_GW), F32),
        pltpu.VMEM((SSD_GROUPS, HEAD, SSD_GW), F32),
        pltpu.VMEM((nc, SSD_GROUPS, HEAD, SSD_GW), BF),
        pltpu.VMEM((t, SSD_XBC), BF),
    ]
    return _mixer_call(_ssd_body, b, t,
                       (x, xh, xh, w_z, w_xbc, wdc.astype(BF), wdr.astype(BF), dbc, dbr, ac, ar, cw,
                        conv_b.astype(F32).reshape(1, SSD_XBC), dsk, norm_g.reshape(1, BRANCH), ex),
                       in_specs, scratch, "ssd")


def _out_body(x_ref, p_ref, m0_ref, m1_ref, m2_ref, m3_ref, wo_ref, lg_ref, lb_ref, wg_ref, wp_ref,
              o_ref, ob_ref, *, alpha):
    y = None
    for i, m_ref in enumerate((m0_ref, m1_ref, m2_ref, m3_ref)):
        part = _dot(m_ref[0], wo_ref[i * BRANCH:(i + 1) * BRANCH, :])
        y = part if y is None else y + part
    r = alpha * x_ref[0] + y
    mu = jnp.mean(r, axis=-1, keepdims=True)
    rc = r - mu
    var = jnp.mean(rc * rc, axis=-1, keepdims=True)
    xn = rc * lax.rsqrt(var + NORM_EPS) * lg_ref[...] + lb_ref[...]
    gate = _sigmoid(_dot(xn.astype(BF), wg_ref[...]))
    out = xn + gate * _dot(p_ref[0].astype(BF), wp_ref[...])
    o_ref[0] = out
    ob_ref[0] = out.astype(BF)


def _out_layer(x, p, mixed, w_out, ln_g, ln_b, w_gate, w_proj, alpha):
    b, t, d = x.shape
    dp = p.shape[-1]
    rows = OUT_ROWS
    row_spec = lambda width: pl.BlockSpec((1, rows, width), lambda bi, r: (bi, r, 0))
    full = lambda shape: pl.BlockSpec(shape, lambda bi, r: (0,) * len(shape))
    return pl.pallas_call(
        functools.partial(_out_body, alpha=alpha),
        grid=(b, t // rows),
        in_specs=[row_spec(d), row_spec(dp)] + [row_spec(BRANCH)] * 4 +
                 [full(w_out.shape), full((1, d)), full((1, d)), full(w_gate.shape), full(w_proj.shape)],
        out_specs=(row_spec(d), row_spec(d)),
        out_shape=(jax.ShapeDtypeStruct((b, t, d), F32), jax.ShapeDtypeStruct((b, t, d), BF)),
        compiler_params=pltpu.CompilerParams(
            dimension_semantics=("arbitrary", "arbitrary"), vmem_limit_bytes=VMEM_LIMIT),
        name="out_proj",
    )(x, p, *mixed, w_out, ln_g.reshape(1, d), ln_b.reshape(1, d), w_gate, w_proj)


def _rope_tables(t):
    inv = ROPE_BASE ** (-jnp.arange(0, HEAD, 2, dtype=F32) / HEAD)
    ang = jnp.arange(t, dtype=F32)[:, None] * inv[None, :]
    cos = jnp.cos(ang)
    sin = jnp.sin(ang)
    return jnp.concatenate([cos, cos], axis=1), jnp.concatenate([-sin, sin], axis=1)


def _layer_weights(l, w_in, w_out, hgrn_lb_logits, w_ple_gate, w_ple_proj):
    wl = w_in[l]
    nb = BRANCH
    o1 = 4 * nb
    o2 = o1 + 5 * nb
    o3 = o2 + 5 * nb + 4 * ML_HEADS

    def cols(*ranges):
        return jnp.concatenate([wl[:, a:b] for a, b in ranges], axis=1).astype(BF)

    ret_kv = cols((nb, 3 * nb))
    ret_qg = cols((0, nb), (3 * nb, 4 * nb))
    hg = o1
    hgrn_zv = cols((hg + 2 * nb, hg + 4 * nb))
    hgrn_q = cols((hg, hg + 3 * nb), (hg + 4 * nb, hg + 5 * nb))
    ml = o2
    ml_kv = jnp.concatenate([wl[:, ml + nb:ml + 2 * nb] * (HEAD ** -0.5), wl[:, ml + 2 * nb:ml + 3 * nb]],
                            axis=1).astype(BF)
    ml_q = cols((ml, ml + nb), (ml + 3 * nb, ml + 5 * nb))
    ml_gates = wl[:, ml + 5 * nb:o3]
    w_z = cols((o3, o3 + nb))
    w_xbc = cols((o3 + nb, o3 + nb + SSD_XBC))
    w_dt = wl[:, o3 + nb + SSD_XBC:]
    lb_w = jax.nn.softmax(hgrn_lb_logits.astype(F32), axis=0)
    lb = (jnp.cumsum(lb_w, axis=0) - lb_w[0:1])[l]
    return dict(ret_kv=ret_kv, ret_qg=ret_qg, hgrn_zv=hgrn_zv, hgrn_q=hgrn_q, ml_kv=ml_kv, ml_q=ml_q,
                ml_gates=ml_gates, w_z=w_z, w_xbc=w_xbc, w_dt=w_dt, lb=lb, w_out=w_out[l].astype(BF),
                w_gate=w_ple_gate[l].astype(BF), w_proj=w_ple_proj[l].astype(BF))


def kernel(x_prompt, x_sample, p_prompt, p_sample, w_in, w_out, ln_g, ln_b, ret_log_rate, ret_norm_g,
           hgrn_lb_logits, hgrn_norm_g, mlstm_i_bias, mlstm_f_bias, mlstm_norm_g, ssd_conv_w, ssd_conv_b,
           ssd_a_log, ssd_dt_bias, ssd_d, ssd_norm_g, w_ple_gate, w_ple_proj):
    depth = w_in.shape[0]
    alpha = (2 * depth) ** 0.25
    weights = [_layer_weights(l, w_in, w_out, hgrn_lb_logits, w_ple_gate, w_ple_proj) for l in range(depth)]

    def trunk(x, p):
        cos2, sin2 = _rope_tables(x.shape[1])
        xb = x.astype(BF)
        for l in range(depth):
            w = weights[l]
            mixed = (
                _retention(xb, w["ret_kv"], w["ret_qg"], ret_log_rate[l], ret_norm_g[l], cos2, sin2),
                _hgrn2(xb, w["hgrn_zv"], w["hgrn_q"], w["lb"], hgrn_norm_g[l]),
                _mlstm(xb, w["ml_kv"], w["ml_q"], w["ml_gates"], mlstm_i_bias[l], mlstm_f_bias[l],
                       mlstm_norm_g[l]),
                _ssd(xb, w["w_z"], w["w_xbc"], w["w_dt"], ssd_conv_w[l], ssd_conv_b[l], ssd_a_log[l],
                     ssd_dt_bias[l], ssd_d[l], ssd_norm_g[l]),
            )
            x, xb = _out_layer(x, p[l], mixed, w["w_out"], ln_g[l], ln_b[l], w["w_gate"], w["w_proj"], alpha)
        return x

    return trunk(x_prompt, p_prompt), trunk(x_sample, p_sample)
```

```python
import functools
import math

import numpy as np
import jax
import jax.numpy as jnp
from jax import lax
from jax.experimental import pallas as pl
from jax.experimental.pallas import tpu as pltpu

BF = jnp.bfloat16
F32 = jnp.float32

CHUNK = 128
TBLK = 512
HEAD = 128
BRANCH = 512
SSD_HEADDIM = 64
SSD_HEADS = 8
SSD_GROUPS = 2
SSD_CONV = 5
HALO = 16
ROPE_BASE = 10000.0
NORM_EPS = 1e-5
OUT_ROWS = 512
VMEM_LIMIT = 56 * 1024 * 1024


def _dot(a, b):
    return jnp.dot(a, b, preferred_element_type=F32)


def _dot_nt(a, b):
    return lax.dot_general(a, b, (((1,), (1,)), ((), ())), preferred_element_type=F32)


def _dot_tn(a, b):
    return lax.dot_general(a, b, (((0,), (0,)), ((), ())), preferred_element_type=F32)


def _split3(x):
    hi = x.astype(BF)
    r = x - hi.astype(F32)
    mid = r.astype(BF)
    lo = (r - mid.astype(F32)).astype(BF)
    return hi, mid, lo


def _sel_dot(sel, x):
    hi, mid, lo = _split3(x)
    return _dot(sel, hi) + _dot(sel, mid) + _dot(sel, lo)


def _dot_sel(x, sel):
    hi, mid, lo = _split3(x)
    return _dot(hi, sel) + _dot(mid, sel) + _dot(lo, sel)


def _sigmoid(x):
    return 1.0 / (1.0 + jnp.exp(-x))


def _silu(x):
    return x * _sigmoid(x)


def _softplus(x):
    return jnp.maximum(x, 0.0) + jnp.log1p(jnp.exp(-jnp.abs(x)))


def _log_sigmoid(x):
    return jnp.minimum(x, 0.0) - jnp.log1p(jnp.exp(-jnp.abs(x)))


def _iota2(shape, axis):
    return lax.broadcasted_iota(jnp.int32, shape, axis)


def _tri_masks(c):
    row = _iota2((c, c), 0)
    col = _iota2((c, c), 1)
    return col <= row, col >= row


def _prefix_suffix_rows(x):
    pre = _sel_dot(_tri_masks(x.shape[0])[0].astype(BF), x)
    return pre, pre[x.shape[0] - 1:, :] - pre + x


def _prefix_suffix_lanes(x):
    n = x.shape[1]
    pre = _dot_sel(x, _tri_masks(n)[1].astype(BF))
    return pre, pre[:, n - 1:] - pre + x


def _group_norm(o, center):
    if center:
        o = o - jnp.mean(o, axis=-1, keepdims=True)
    return o * lax.rsqrt(jnp.mean(o * o, axis=-1, keepdims=True) + NORM_EPS)


def _blk_index(phase, c, nblk):
    return jnp.where(phase == 0, nblk - 1 - c, c)


def _sweep(nblk, bwd_proj, bwd_chunk, fwd_proj, fwd_chunk, bwd_init, fwd_init, lookahead):
    phase = pl.program_id(1)
    c = pl.program_id(2)
    per = TBLK // CHUNK

    def walk(order, first_chunk, proj, chunk):
        if not lookahead:
            for j in order:
                chunk(j * CHUNK, first_chunk + j, proj(j * CHUNK))
            return
        ahead = proj(order[0] * CHUNK)
        for n, j in enumerate(order):
            cur = ahead
            if n + 1 < per:
                ahead = proj(order[n + 1] * CHUNK)
            chunk(j * CHUNK, first_chunk + j, cur)

    @pl.when(phase == 0)
    def _():
        pl.when(c == 0)(bwd_init)
        walk(list(reversed(range(per))), (nblk - 1 - c) * per, bwd_proj, bwd_chunk)

    @pl.when(phase == 1)
    def _():
        pl.when(c == 0)(fwd_init)
        walk(list(range(per)), c * per, fwd_proj, fwd_chunk)


def _mixer_call(body, b, t, in_arrays, in_specs, scratch, name):
    nblk = t // TBLK
    return pl.pallas_call(
        functools.partial(body, nblk=nblk),
        grid=(b, 2, nblk),
        in_specs=in_specs,
        out_specs=pl.BlockSpec((1, TBLK, BRANCH),
                               lambda bi, ph, c: (bi, jnp.where(ph == 0, 0, c), 0)),
        out_shape=jax.ShapeDtypeStruct((b, t, BRANCH), BF),
        scratch_shapes=scratch,
        compiler_params=pltpu.CompilerParams(
            dimension_semantics=("arbitrary", "arbitrary", "arbitrary"),
            vmem_limit_bytes=VMEM_LIMIT),
        name=name,
    )(*in_arrays)


def _x_spec(nblk, d):
    return pl.BlockSpec((1, TBLK, d), lambda bi, ph, c: (bi, _blk_index(ph, c, nblk), 0))


def _full_spec(shape):
    zeros = (0,) * len(shape)
    return pl.BlockSpec(shape, lambda bi, ph, c: zeros)


def _rows(ci):
    return pl.ds(pl.multiple_of(ci * CHUNK, CHUNK), CHUNK)


def _rope(x, cos2, sin2):
    return x * cos2 + pltpu.roll(x, HEAD // 2, axis=1) * sin2


def _ret_body(lg_ref, x_ref, cos_ref, sin_ref, wkv_ref, wqg_ref, ng_ref, o_ref,
              sf_ref, sb_ref, sbs_ref, m_ref, kv_ref, *, nblk):
    c_sz = CHUNK
    nh = BRANCH // HEAD
    scale = HEAD ** -0.5

    def bwd_init():
        sb_ref[...] = jnp.zeros_like(sb_ref)

    def fwd_init():
        sf_ref[...] = jnp.zeros_like(sf_ref)
        lower, upper = _tri_masks(c_sz)
        dist = (_iota2((c_sz, c_sz), 0) - _iota2((c_sz, c_sz), 1)).astype(F32)
        for h in range(nh):
            fwd = jnp.where(lower, jnp.exp(lg_ref[0, h] * dist), 0.0)
            bwd = jnp.where(upper, jnp.exp(-lg_ref[1, h] * dist), 0.0)
            m_ref[h] = (fwd + bwd) * scale

    def bwd_proj(r0):
        return _dot(x_ref[0, pl.ds(r0, c_sz), :], wkv_ref[...])

    def fwd_proj(r0):
        return _dot(x_ref[0, pl.ds(r0, c_sz), :], wqg_ref[...])

    def bwd_chunk(r0, ci, kv):
        rows = pl.ds(r0, c_sz)
        pos = _iota2((c_sz, HEAD), 0).astype(F32)
        span = jnp.full((1, HEAD), float(c_sz), F32)
        cos2 = cos_ref[rows, :]
        sin2 = sin_ref[rows, :]
        for h in range(nh):
            k = _rope(kv[:, h * HEAD:(h + 1) * HEAD], cos2, sin2)
            v = kv[:, BRANCH + h * HEAD:BRANCH + (h + 1) * HEAD].astype(BF)
            kv_ref[_rows(ci), h * HEAD:(h + 1) * HEAD] = k.astype(BF)
            kv_ref[_rows(ci), BRANCH + h * HEAD:BRANCH + (h + 1) * HEAD] = v
            lg_b = lg_ref[1, h]
            sbs_ref[ci, h] = sb_ref[h].astype(BF)
            k_in = (k * (jnp.exp(lg_b * pos) * scale)).astype(BF)
            sb_ref[h] = jnp.exp(lg_b * span) * sb_ref[h] + _dot_tn(k_in, v)

    def fwd_chunk(r0, ci, qg):
        rows = pl.ds(r0, c_sz)
        pos = _iota2((c_sz, HEAD), 0).astype(F32)
        span = jnp.full((1, HEAD), float(c_sz), F32)
        cos2 = cos_ref[rows, :]
        sin2 = sin_ref[rows, :]
        for h in range(nh):
            sl = slice(h * HEAD, (h + 1) * HEAD)
            q = _rope(qg[:, sl], cos2, sin2).astype(BF)
            g = qg[:, BRANCH + h * HEAD:BRANCH + (h + 1) * HEAD]
            k = kv_ref[_rows(ci), sl]
            v = kv_ref[_rows(ci), BRANCH + h * HEAD:BRANCH + (h + 1) * HEAD]
            lg_f = lg_ref[0, h]
            lg_b = lg_ref[1, h]
            p = (_dot_nt(q, k) * m_ref[h]).astype(BF)
            o = _dot(p, v)
            states = jnp.concatenate([sf_ref[h].astype(BF), sbs_ref[ci, h]], axis=1)
            r = _dot(q, states)
            o = o + jnp.exp(lg_f * (pos + 1.0)) * r[:, :HEAD] + jnp.exp(lg_b * (c_sz - pos)) * r[:, HEAD:]
            k_in = (k.astype(F32) * (jnp.exp(lg_f * (c_sz - 1.0 - pos)) * scale)).astype(BF)
            sf_ref[h] = jnp.exp(lg_f * span) * sf_ref[h] + _dot_tn(k_in, v)
            y = _group_norm(o, True) * ng_ref[:, sl] * _silu(g)
            o_ref[0, rows, sl] = y.astype(o_ref.dtype)

    _sweep(nblk, bwd_proj, bwd_chunk, fwd_proj, fwd_chunk, bwd_init, fwd_init, lookahead=True)


def _retention(x, w_kv, w_qg, log_rate, norm_g, cos2, sin2):
    b, t, d = x.shape
    nblk = t // TBLK
    nc = t // CHUNK
    nh = BRANCH // HEAD
    lg = -jnp.exp(log_rate.astype(F32))
    in_specs = [
        pl.BlockSpec(memory_space=pltpu.SMEM),
        _x_spec(nblk, d),
        pl.BlockSpec((TBLK, HEAD), lambda bi, ph, c: (_blk_index(ph, c, nblk), 0)),
        pl.BlockSpec((TBLK, HEAD), lambda bi, ph, c: (_blk_index(ph, c, nblk), 0)),
        _full_spec(w_kv.shape),
        _full_spec(w_qg.shape),
        _full_spec((1, BRANCH)),
    ]
    scratch = [
        pltpu.VMEM((nh, HEAD, HEAD), F32),
        pltpu.VMEM((nh, HEAD, HEAD), F32),
        pltpu.VMEM((nc, nh, HEAD, HEAD), BF),
        pltpu.VMEM((nh, CHUNK, CHUNK), F32),
        pltpu.VMEM((t, 2 * BRANCH), BF),
    ]
    return _mixer_call(_ret_body, b, t, (lg, x, cos2, sin2, w_kv, w_qg, norm_g.reshape(1, BRANCH)),
                       in_specs, scratch, "retention")


HGRN_LEVELS = int(math.log2(CHUNK))


def _hgrn_masks():
    c = CHUNK
    t = np.arange(c)
    mask = np.zeros((2, HGRN_LEVELS + 1, c, c), np.float32)
    sign = np.zeros((2, HGRN_LEVELS, c), np.float32)
    for l in range(HGRN_LEVELS):
        s = 1 << l
        hi = (t % (2 * s)) >= s
        is_hi = hi[:, None]
        same = (t[:, None] // (2 * s)) == (t[None, :] // (2 * s))
        mask[0, l] = same & is_hi & ~is_hi.T
        mask[1, l] = same & ~is_hi & is_hi.T
        lower = 0.0 if s == 1 else -1.0
        sign[0, l] = np.where(hi, 1.0, lower)
        sign[1, l] = np.where(hi, lower, 1.0)
    mask[0, HGRN_LEVELS] = np.eye(c)
    sign = np.broadcast_to((sign * math.log2(math.e))[..., None], sign.shape + (HEAD,))
    return mask, np.ascontiguousarray(sign)


def _split_delta(cum, log_f, l, reverse):
    c, w = cum.shape
    s = 1 << l
    if s == 1:
        return log_f
    edge = s if reverse else s - 1
    if 2 * s >= 8:
        blocks = cum.reshape(c // (2 * s), 2 * s, w)
        at_split = jnp.broadcast_to(blocks[:, edge:edge + 1, :], blocks.shape).reshape(c, w)
    else:
        tiles = cum.reshape(c // 8, 8, w)
        sub = _iota2(tiles.shape, 1)
        at_split = None
        for blk in range(8 // (2 * s)):
            r = blk * 2 * s + edge
            row = jnp.broadcast_to(tiles[:, r:r + 1, :], tiles.shape)
            at_split = row if at_split is None else jnp.where(sub >= blk * 2 * s, row, at_split)
        at_split = at_split.reshape(c, w)
    return cum - at_split


def _hgrn_gates(z, lb):
    s = _sigmoid(z)
    log_f = jnp.log(lb + (1.0 - lb) * s)
    k = (1.0 - lb) * (1.0 - s)
    return k, log_f


def _hgrn_body(x_ref, wzv_ref, wq_ref, lb_ref, ng_ref, mask_ref, sign_ref, o_ref,
               sf_ref, sb_ref, sbs_ref, v_ref, *, nblk):
    c_sz = CHUNK
    nh = BRANCH // HEAD
    nl = HGRN_LEVELS

    def bwd_init():
        sb_ref[...] = jnp.zeros_like(sb_ref)

    def fwd_init():
        sf_ref[...] = jnp.zeros_like(sf_ref)

    def x_chunk(r0):
        return x_ref[0, pl.ds(r0, c_sz), :]

    def bwd_chunk(r0, ci, xb):
        zv = _dot(xb, wzv_ref[...])
        kb, lfb = _hgrn_gates(zv[:, :BRANCH], lb_ref[1:2, :])
        _, suf = _prefix_suffix_rows(lfb)
        total = suf[0:1, :]
        k_in = (kb * jnp.exp(total - suf)).astype(BF)
        dec = jnp.exp(total)
        v = zv[:, BRANCH:].astype(BF)
        v_ref[_rows(ci), :] = v
        for h in range(nh):
            sl = slice(h * HEAD, (h + 1) * HEAD)
            sbs_ref[ci, h] = sb_ref[h].astype(BF)
            sb_ref[h] = dec[:, sl] * sb_ref[h] + _dot_tn(v[:, sl], k_in[:, sl])

    def fwd_chunk(r0, ci, xb):
        rows = pl.ds(r0, c_sz)
        proj = _dot(xb, wq_ref[...])
        q = proj[:, :BRANCH]
        kf, lff = _hgrn_gates(proj[:, BRANCH:2 * BRANCH], lb_ref[0:1, :])
        kb, lfb = _hgrn_gates(proj[:, 2 * BRANCH:3 * BRANCH], lb_ref[1:2, :])
        g = proj[:, 3 * BRANCH:]
        v = v_ref[_rows(ci), :]

        scores = [None] * nh
        ksum = (kf + kb).astype(BF)
        qb = q.astype(BF)
        for h in range(nh):
            sl = slice(h * HEAD, (h + 1) * HEAD)
            scores[h] = _dot_nt(qb[:, sl], ksum[:, sl]) * mask_ref[0, nl]
        pre, _ = _prefix_suffix_rows(lff)
        _, suf = _prefix_suffix_rows(lfb)
        for dr, (kd, lfd, cum) in enumerate(((kf, lff, pre), (kb, lfb, suf))):
            for l in range(nl):
                sgn = jnp.concatenate([sign_ref[dr, l]] * nh, axis=1)
                e = jnp.exp2(_split_delta(cum, lfd, l, dr == 1) * sgn)
                u = (jnp.where(sgn > 0.0, q, kd) * e).astype(BF)
                for h in range(nh):
                    sl = slice(h * HEAD, (h + 1) * HEAD)
                    scores[h] = scores[h] + _dot_nt(u[:, sl], u[:, sl]) * mask_ref[dr, l]

        q_f = (q * jnp.exp(pre)).astype(BF)
        q_b = (q * jnp.exp(suf)).astype(BF)
        total = pre[c_sz - 1:c_sz, :]
        k_in = (kf * jnp.exp(total - pre)).astype(BF)
        dec = jnp.exp(total)
        for h in range(nh):
            sl = slice(h * HEAD, (h + 1) * HEAD)
            o = _dot(scores[h].astype(BF), v[:, sl])
            q_cat = jnp.concatenate([q_f[:, sl], q_b[:, sl]], axis=1)
            s_cat = jnp.concatenate([sf_ref[h].astype(BF), sbs_ref[ci, h]], axis=1)
            o = o + _dot_nt(q_cat, s_cat)
            sf_ref[h] = dec[:, sl] * sf_ref[h] + _dot_tn(v[:, sl], k_in[:, sl])
            y = _group_norm(o, False) * ng_ref[:, sl] * _silu(g[:, sl])
            o_ref[0, rows, sl] = y.astype(o_ref.dtype)

    _sweep(nblk, x_chunk, bwd_chunk, x_chunk, fwd_chunk, bwd_init, fwd_init, lookahead=False)


def _hgrn2(x, w_zv, w_q, lb, norm_g):
    b, t, d = x.shape
    nblk = t // TBLK
    nc = t // CHUNK
    nh = BRANCH // HEAD
    mask, sign = _hgrn_masks()
    mask = jnp.asarray(mask, F32)
    sign = jnp.asarray(sign, F32)
    in_specs = [
        _x_spec(nblk, d),
        _full_spec(w_zv.shape),
        _full_spec(w_q.shape),
        _full_spec((2, BRANCH)),
        _full_spec((1, BRANCH)),
        _full_spec(mask.shape),
        _full_spec(sign.shape),
    ]
    scratch = [
        pltpu.VMEM((nh, HEAD, HEAD), F32),
        pltpu.VMEM((nh, HEAD, HEAD), F32),
        pltpu.VMEM((nc, nh, HEAD, HEAD), BF),
        pltpu.VMEM((t, BRANCH), BF),
    ]
    return _mixer_call(_hgrn_body, b, t, (x, w_zv, w_q, lb, norm_g.reshape(1, BRANCH), mask, sign),
                       in_specs, scratch, "hgrn2")


ML_HEADS = 4
ML_AUG = 2 * HEAD


def _cummax_rows(a, reverse):
    n = a.shape[0]
    row = _iota2(a.shape, 0)
    k = 1
    while k < n:
        if reverse:
            shifted = pltpu.roll(a, n - k, axis=0)
            a = jnp.where(row < n - k, jnp.maximum(a, shifted), a)
        else:
            shifted = pltpu.roll(a, k, axis=0)
            a = jnp.where(row >= k, jnp.maximum(a, shifted), a)
        k *= 2
    return a


def _ml_body(x_ref, wkv_ref, wq_ref, wgc_ref, wgr_ref, bc_ref, br_ref, ng_ref, o_ref,
             cf_ref, cb_ref, cbs_ref, m_ref, mbs_ref, kv_ref, *, nblk):
    c_sz = CHUNK
    nh = ML_HEADS

    def gates(xb):
        is_fwd = _iota2((c_sz, HEAD), 1) < nh
        gc = _dot(xb, wgc_ref[...]) + bc_ref[...]
        ipre = gc[:, :HEAD]
        pre, suf = _prefix_suffix_rows(_log_sigmoid(gc[:, HEAD:]))
        cum = jnp.where(is_fwd, pre, suf)
        a = ipre - cum
        run_max = jnp.where(is_fwd, _cummax_rows(a, False), _cummax_rows(a, True))
        return a, cum, run_max

    def edge(arr):
        return jnp.where(_iota2((1, HEAD), 1) < nh, arr[c_sz - 1:c_sz, :], arr[0:1, :])

    def ones_col():
        return (_iota2((c_sz, HEAD), 1) == 0).astype(BF)

    def update_state(state_ref, lane0, k_all, v_aug, wk, sc):
        for h in range(nh):
            l = lane0 + h
            kw = (k_all[:, h * HEAD:(h + 1) * HEAD] * wk[:, l:l + 1]).astype(BF)
            state_ref[h] = sc[:, l:l + 1] * state_ref[h] + _dot_tn(kw, v_aug[h])

    def bwd_init():
        cb_ref[...] = jnp.zeros_like(cb_ref)
        m_ref[...] = jnp.zeros_like(m_ref)

    def fwd_init():
        cf_ref[...] = jnp.zeros_like(cf_ref)
        m_ref[...] = jnp.zeros_like(m_ref)

    def x_chunk(r0):
        return x_ref[0, pl.ds(r0, c_sz), :]

    def bwd_chunk(r0, ci, xb):
        a, cum, run_max = gates(xb)
        m_prev = m_ref[...]
        mbs_ref[ci] = m_prev
        for h in range(nh):
            cbs_ref[ci, h] = cb_ref[h].astype(BF)
        mu_e = edge(jnp.maximum(m_prev, run_max))
        kv = _dot(xb, wkv_ref[...])
        kv_ref[_rows(ci), :] = kv.astype(BF)
        v_aug = [jnp.concatenate([kv[:, BRANCH + h * HEAD:BRANCH + (h + 1) * HEAD].astype(BF), ones_col()],
                                 axis=1) for h in range(nh)]
        update_state(cb_ref, nh, kv[:, :BRANCH], v_aug, jnp.exp(a - mu_e), jnp.exp(m_prev - mu_e))
        m_ref[...] = edge(cum) + mu_e

    def fwd_chunk(r0, ci, xb):
        rows = pl.ds(r0, c_sz)
        a, cum, run_max = gates(xb)
        lower, upper = _tri_masks(c_sz)
        m_prev = jnp.where(_iota2((1, HEAD), 1) < nh, m_ref[...], mbs_ref[ci])
        mu = jnp.maximum(m_prev, run_max)
        s_inter = jnp.exp(m_prev - mu)
        thr = jnp.exp(-(cum + mu))

        gr = _dot_nt(wgr_ref[...], xb) + br_ref[...]
        ipre_r = gr[:2 * nh]
        pre_r, suf_r = _prefix_suffix_lanes(_log_sigmoid(gr[2 * nh:]))
        a_r = ipre_r - jnp.where(_iota2((2 * nh, c_sz), 0) < nh, pre_r, suf_r)

        proj = _dot(xb, wq_ref[...])
        kvc = kv_ref[_rows(ci), :]
        v_aug = [jnp.concatenate([kvc[:, BRANCH + h * HEAD:BRANCH + (h + 1) * HEAD], ones_col()], axis=1)
                 for h in range(nh)]
        for h in range(nh):
            sl = slice(h * HEAD, (h + 1) * HEAD)
            q = proj[:, sl].astype(BF)
            qk = _dot_nt(q, kvc[:, sl])
            states = jnp.concatenate([cf_ref[h].astype(BF), cbs_ref[ci, h]], axis=1)
            r = _dot(q, states)
            hsum = None
            for dr in range(2):
                l = dr * nh + h
                wgt = jnp.where(lower if dr == 0 else upper,
                                jnp.exp(a_r[l:l + 1, :] - mu[:, l:l + 1]), 0.0)
                tot = s_inter[:, l:l + 1] * r[:, dr * ML_AUG:(dr + 1) * ML_AUG] + _dot((qk * wgt).astype(BF), v_aug[h])
                den = jnp.maximum(jnp.abs(tot[:, HEAD:HEAD + 1]), thr[:, l:l + 1])
                out = tot[:, :HEAD] / den
                hsum = out if hsum is None else hsum + out
            o_gate = proj[:, BRANCH + h * HEAD:BRANCH + (h + 1) * HEAD]
            g = proj[:, 2 * BRANCH + h * HEAD:2 * BRANCH + (h + 1) * HEAD]
            y = _group_norm(_sigmoid(o_gate) * hsum, True) * ng_ref[:, sl] * _silu(g)
            o_ref[0, rows, sl] = y.astype(o_ref.dtype)

        mu_e = edge(mu)
        update_state(cf_ref, 0, kvc[:, :BRANCH].astype(F32), v_aug, jnp.exp(a - mu_e), jnp.exp(m_prev - mu_e))
        m_ref[...] = edge(cum) + mu_e

    _sweep(nblk, x_chunk, bwd_chunk, x_chunk, fwd_chunk, bwd_init, fwd_init, lookahead=False)


def _mlstm(x, w_kv, w_q, w_gates, i_bias, f_bias, norm_g):
    b, t, d = x.shape
    nblk = t // TBLK
    nc = t // CHUNK
    nh = ML_HEADS
    pad = HEAD - 2 * nh
    w_i = w_gates[:, :2 * nh]
    w_f = w_gates[:, 2 * nh:]
    wgc = jnp.concatenate([jnp.pad(w_i, ((0, 0), (0, pad))), jnp.pad(w_f, ((0, 0), (0, pad)))], axis=1)
    wgr = w_gates.T
    bias = jnp.concatenate([i_bias.reshape(-1), f_bias.reshape(-1)]).astype(F32)
    bc = jnp.concatenate([jnp.pad(bias[:2 * nh], (0, pad)), jnp.pad(bias[2 * nh:], (0, pad))]).reshape(1, 2 * HEAD)
    br = bias.reshape(4 * nh, 1)
    in_specs = [
        _x_spec(nblk, d),
        _full_spec(w_kv.shape),
        _full_spec(w_q.shape),
        _full_spec(wgc.shape),
        _full_spec(wgr.shape),
        _full_spec(bc.shape),
        _full_spec(br.shape),
        _full_spec((1, BRANCH)),
    ]
    scratch = [
        pltpu.VMEM((nh, HEAD, ML_AUG), F32),
        pltpu.VMEM((nh, HEAD, ML_AUG), F32),
        pltpu.VMEM((nc, nh, HEAD, ML_AUG), BF),
        pltpu.VMEM((1, HEAD), F32),
        pltpu.VMEM((nc, 1, HEAD), F32),
        pltpu.VMEM((t, 2 * BRANCH), BF),
    ]
    return _mixer_call(_ml_body, b, t,
                       (x, w_kv, w_q, wgc.astype(BF), wgr.astype(BF), bc, br, norm_g.reshape(1, BRANCH)),
                       in_specs, scratch, "mlstm")


SSD_XBC = BRANCH + 2 * SSD_GROUPS * HEAD
SSD_GW = BRANCH // SSD_GROUPS


def _ssd_body(x_ref, xp_ref, xn_ref, wz_ref, wx_ref, wdc_ref, wdr_ref, dbc_ref, dbr_ref,
              ac_ref, ar_ref, cw_ref, cb_ref, dsk_ref, ng_ref, ex_ref, o_ref,
              xe_ref, sf_ref, sb_ref, sbs_ref, act_ref, *, nblk):
    c_sz = CHUNK
    nh = SSD_HEADS
    per = TBLK // CHUNK

    def dt_columns(xb):
        is_fwd = _iota2((c_sz, HEAD), 1) < nh
        dt = _softplus(_dot(xb, wdc_ref[...]) + dbc_ref[...])
        pre, suf = _prefix_suffix_rows(dt * ac_ref[...])
        return dt, jnp.where(is_fwd, pre, suf)

    def expand(a, dr):
        hi, mid, _ = _split3(a)
        e = ex_ref[:, dr * BRANCH:(dr + 1) * BRANCH]
        return _dot(hi, e) + _dot(mid, e)

    def update_state(state_ref, dr, xs, bm, dt, cum):
        cum_e = jnp.where(_iota2((1, HEAD), 1) < nh, cum[c_sz - 1:c_sz, :], cum[0:1, :])
        w_state = jnp.exp(cum_e - cum) * dt
        dec = jnp.exp(jnp.broadcast_to(cum_e, (8, HEAD)))
        xw = (xs * expand(w_state, dr)).astype(BF)
        d512 = expand(dec, dr)[0:1, :]
        for g in range(SSD_GROUPS):
            gs = slice(g * SSD_GW, (g + 1) * SSD_GW)
            state_ref[g] = d512[:, gs] * state_ref[g] + _dot_tn(bm[:, g * HEAD:(g + 1) * HEAD], xw[:, gs])

    def bwd_init():
        sb_ref[...] = jnp.zeros_like(sb_ref)

    def fwd_init():
        sf_ref[...] = jnp.zeros_like(sf_ref)

    def project_block():
        blk = nblk - 1 - pl.program_id(2)

        def body(j, carry):
            r0 = pl.multiple_of(j * c_sz, c_sz)
            xe_ref[pl.ds(HALO + r0, c_sz), :] = _dot(x_ref[0, pl.ds(r0, c_sz), :], wx_ref[...])
            return carry

        lax.fori_loop(0, per, body, 0)
        halo = jnp.concatenate([xp_ref[0, 0], xn_ref[0, 0]], axis=0)
        ph = _dot(halo, wx_ref[...])
        xe_ref[0:HALO, :] = ph[:HALO] * jnp.where(blk > 0, 1.0, 0.0)
        xe_ref[HALO + TBLK:, :] = ph[HALO:] * jnp.where(blk < nblk - 1, 1.0, 0.0)

    def x_chunk(r0):
        return x_ref[0, pl.ds(r0, c_sz), :]

    def bwd_chunk(r0, ci, xb):
        half = (SSD_CONV - 1) // 2
        win = xe_ref[pl.ds(r0, c_sz + 2 * HALO), :]
        conv = cb_ref[...]
        for k in range(SSD_CONV):
            conv = conv + cw_ref[k:k + 1, :] * win[HALO - half + k:HALO - half + k + c_sz, :]
        act = _silu(conv)
        act_ref[_rows(ci), :] = act.astype(BF)
        dt, cum = dt_columns(xb)
        for g in range(SSD_GROUPS):
            sbs_ref[ci, g] = sb_ref[g].astype(BF)
        update_state(sb_ref, 1, act[:, :BRANCH], act[:, BRANCH:BRANCH + SSD_GROUPS * HEAD].astype(BF), dt, cum)

    def fwd_chunk(r0, ci, xb):
        rows = pl.ds(r0, c_sz)
        lower, upper = _tri_masks(c_sz)
        act = act_ref[_rows(ci), :]
        xs = act[:, :BRANCH].astype(F32)
        bm = act[:, BRANCH:BRANCH + SSD_GROUPS * HEAD]
        cm = act[:, BRANCH + SSD_GROUPS * HEAD:]
        dt, cum = dt_columns(xb)
        dt_r = _softplus(_dot_nt(wdr_ref[...], xb) + dbr_ref[...])
        pre_r, suf_r = _prefix_suffix_lanes(dt_r * ar_ref[...])
        cum_r = jnp.where(_iota2((2 * nh, c_sz), 0) < nh, pre_r, suf_r)
        z = _dot(xb, wz_ref[...])
        lane_half = _iota2((c_sz, HEAD), 1) < SSD_HEADDIM
        ecum = jnp.exp(cum)
        ecum_f = expand(ecum, 0)
        ecum_b = expand(ecum, 1)
        hpg = SSD_HEADS // SSD_GROUPS
        ys = []
        for g in range(SSD_GROUPS):
            cg = cm[:, g * HEAD:(g + 1) * HEAD]
            gmat = _dot_nt(cg, bm[:, g * HEAD:(g + 1) * HEAD])
            gs = slice(g * SSD_GW, (g + 1) * SSD_GW)
            inter = (ecum_f[:, gs] * _dot(cg, sf_ref[g].astype(BF)) +
                     ecum_b[:, gs] * _dot(cg, sbs_ref[ci, g]))
            pieces = []
            for pair in range(hpg // 2):
                acc = None
                for side in range(2):
                    h = g * hpg + 2 * pair + side
                    lf, lbk = h, nh + h
                    m = (jnp.where(lower, jnp.exp(cum[:, lf:lf + 1] - cum_r[lf:lf + 1, :]) * dt_r[lf:lf + 1, :], 0.0) +
                         jnp.where(upper, jnp.exp(cum[:, lbk:lbk + 1] - cum_r[lbk:lbk + 1, :]) * dt_r[lbk:lbk + 1, :], 0.0))
                    xpair = xs[:, (h // 2) * HEAD:(h // 2 + 1) * HEAD]
                    xh = jnp.where(lane_half if side == 0 else jnp.logical_not(lane_half), xpair, 0.0).astype(BF)
                    part = _dot((gmat * m).astype(BF), xh)
                    acc = part if acc is None else acc + part
                pieces.append(acc)
            ys.append(jnp.concatenate(pieces, axis=1) + inter)
        y = jnp.concatenate(ys, axis=1) + dsk_ref[...] * xs
        y = y * _silu(z)
        for g in range(SSD_GROUPS):
            gs = slice(g * SSD_GW, (g + 1) * SSD_GW)
            o_ref[0, rows, gs] = (_group_norm(y[:, gs], False) * ng_ref[:, gs]).astype(o_ref.dtype)
        update_state(sf_ref, 0, xs, bm, dt, cum)

    pl.when(pl.program_id(1) == 0)(project_block)
    _sweep(nblk, x_chunk, bwd_chunk, x_chunk, fwd_chunk, bwd_init, fwd_init, lookahead=False)


def _ssd(x, w_z, w_xbc, w_dt, conv_w, conv_b, a_log, dt_bias, d_skip, norm_g):
    b, t, d = x.shape
    nblk = t // TBLK
    nc = t // CHUNK
    nh = SSD_HEADS
    pad = HEAD - 2 * nh
    wdc = jnp.pad(w_dt, ((0, 0), (0, pad)))
    wdr = w_dt.T
    db = dt_bias.reshape(-1).astype(F32)
    a = (-jnp.exp(a_log.astype(F32))).reshape(-1)
    dbc = jnp.pad(db, (0, pad)).reshape(1, HEAD)
    ac = jnp.pad(a, (0, pad)).reshape(1, HEAD)
    dbr = db.reshape(2 * nh, 1)
    ar = a.reshape(2 * nh, 1)
    cw = jnp.pad(conv_w.astype(F32), ((0, 8 - SSD_CONV), (0, 0)))
    dsk = jnp.repeat(d_skip.astype(F32), SSD_HEADDIM).reshape(1, BRANCH)
    ex = np.zeros((HEAD, 2 * BRANCH), np.float32)
    for l in range(2 * nh):
        ex[l, l * SSD_HEADDIM:(l + 1) * SSD_HEADDIM] = 1.0
    ex = jnp.asarray(ex, BF)
    xh = x.reshape(b, t // HALO, HALO, d)
    per = TBLK // HALO
    last = t // HALO - 1
    in_specs = [
        _x_spec(nblk, d),
        pl.BlockSpec((1, 1, HALO, d),
                     lambda bi, ph, c: (bi, jnp.maximum(_blk_index(ph, c, nblk) * per - 1, 0), 0, 0)),
        pl.BlockSpec((1, 1, HALO, d),
                     lambda bi, ph, c: (bi, jnp.minimum((_blk_index(ph, c, nblk) + 1) * per, last), 0, 0)),
        _full_spec(w_z.shape),
        _full_spec(w_xbc.shape),
        _full_spec(wdc.shape),
        _full_spec(wdr.shape),
        _full_spec(dbc.shape),
        _full_spec(dbr.shape),
        _full_spec(ac.shape),
        _full_spec(ar.shape),
        _full_spec(cw.shape),
        _full_spec((1, SSD_XBC)),
        _full_spec((1, BRANCH)),
        _full_spec((1, BRANCH)),
        _full_spec(ex.shape),
    ]
    scratch = [
        pltpu.VMEM((TBLK + 2 * HALO, SSD_XBC), F32),
        pltpu.VMEM((SSD_GROUPS, HEAD, SSD_GW), F32),
        pltpu.VMEM((SSD_GROUPS, HEAD, SSD_GW), F32),
        pltpu.VMEM((nc, SSD_GROUPS, HEAD, SSD_GW), BF),
        pltpu.VMEM((t, SSD_XBC), BF),
    ]
    return _mixer_call(_ssd_body, b, t,
                       (x, xh, xh, w_z, w_xbc, wdc.astype(BF), wdr.astype(BF), dbc, dbr, ac, ar, cw,
                        conv_b.astype(F32).reshape(1, SSD_XBC), dsk, norm_g.reshape(1, BRANCH), ex),
                       in_specs, scratch, "ssd")


def _out_body(x_ref, p_ref, m0_ref, m1_ref, m2_ref, m3_ref, wo_ref, lg_ref, lb_ref, wg_ref, wp_ref,
              o_ref, ob_ref, *, alpha):
    y = None
    for i, m_ref in enumerate((m0_ref, m1_ref, m2_ref, m3_ref)):
        part = _dot(m_ref[0], wo_ref[i * BRANCH:(i + 1) * BRANCH, :])
        y = part if y is None else y + part
    r = alpha * x_ref[0] + y
    mu = jnp.mean(r, axis=-1, keepdims=True)
    rc = r - mu
    var = jnp.mean(rc * rc, axis=-1, keepdims=True)
    xn = rc * lax.rsqrt(var + NORM_EPS) * lg_ref[...] + lb_ref[...]
    gate = _sigmoid(_dot(xn.astype(BF), wg_ref[...]))
    out = xn + gate * _dot(p_ref[0].astype(BF), wp_ref[...])
    o_ref[0] = out
    ob_ref[0] = out.astype(BF)


def _out_layer(x, p, mixed, w_out, ln_g, ln_b, w_gate, w_proj, alpha):
    b, t, d = x.shape
    dp = p.shape[-1]
    rows = OUT_ROWS
    row_spec = lambda width: pl.BlockSpec((1, rows, width), lambda bi, r: (bi, r, 0))
    full = lambda shape: pl.BlockSpec(shape, lambda bi, r: (0,) * len(shape))
    return pl.pallas_call(
        functools.partial(_out_body, alpha=alpha),
        grid=(b, t // rows),
        in_specs=[row_spec(d), row_spec(dp)] + [row_spec(BRANCH)] * 4 +
                 [full(w_out.shape), full((1, d)), full((1, d)), full(w_gate.shape), full(w_proj.shape)],
        out_specs=(row_spec(d), row_spec(d)),
        out_shape=(jax.ShapeDtypeStruct((b, t, d), F32), jax.ShapeDtypeStruct((b, t, d), BF)),
        compiler_params=pltpu.CompilerParams(
            dimension_semantics=("arbitrary", "arbitrary"), vmem_limit_bytes=VMEM_LIMIT),
        name="out_proj",
    )(x, p, *mixed, w_out, ln_g.reshape(1, d), ln_b.reshape(1, d), w_gate, w_proj)


def _rope_tables(t):
    inv = ROPE_BASE ** (-jnp.arange(0, HEAD, 2, dtype=F32) / HEAD)
    ang = jnp.arange(t, dtype=F32)[:, None] * inv[None, :]
    cos = jnp.cos(ang)
    sin = jnp.sin(ang)
    return jnp.concatenate([cos, cos], axis=1), jnp.concatenate([-sin, sin], axis=1)


def _layer_weights(l, w_in, w_out, hgrn_lb_logits, w_ple_gate, w_ple_proj):
    wl = w_in[l]
    nb = BRANCH
    o1 = 4 * nb
    o2 = o1 + 5 * nb
    o3 = o2 + 5 * nb + 4 * ML_HEADS

    def cols(*ranges):
        return jnp.concatenate([wl[:, a:b] for a, b in ranges], axis=1).astype(BF)

    ret_kv = cols((nb, 3 * nb))
    ret_qg = cols((0, nb), (3 * nb, 4 * nb))
    hg = o1
    hgrn_zv = cols((hg + 2 * nb, hg + 4 * nb))
    hgrn_q = cols((hg, hg + 3 * nb), (hg + 4 * nb, hg + 5 * nb))
    ml = o2
    ml_kv = jnp.concatenate([wl[:, ml + nb:ml + 2 * nb] * (HEAD ** -0.5), wl[:, ml + 2 * nb:ml + 3 * nb]],
                            axis=1).astype(BF)
    ml_q = cols((ml, ml + nb), (ml + 3 * nb, ml + 5 * nb))
    ml_gates = wl[:, ml + 5 * nb:o3]
    w_z = cols((o3, o3 + nb))
    w_xbc = cols((o3 + nb, o3 + nb + SSD_XBC))
    w_dt = wl[:, o3 + nb + SSD_XBC:]
    lb_w = jax.nn.softmax(hgrn_lb_logits.astype(F32), axis=0)
    lb = (jnp.cumsum(lb_w, axis=0) - lb_w[0:1])[l]
    return dict(ret_kv=ret_kv, ret_qg=ret_qg, hgrn_zv=hgrn_zv, hgrn_q=hgrn_q, ml_kv=ml_kv, ml_q=ml_q,
                ml_gates=ml_gates, w_z=w_z, w_xbc=w_xbc, w_dt=w_dt, lb=lb, w_out=w_out[l].astype(BF),
                w_gate=w_ple_gate[l].astype(BF), w_proj=w_ple_proj[l].astype(BF))


def kernel(x_prompt, x_sample, p_prompt, p_sample, w_in, w_out, ln_g, ln_b, ret_log_rate, ret_norm_g,
           hgrn_lb_logits, hgrn_norm_g, mlstm_i_bias, mlstm_f_bias, mlstm_norm_g, ssd_conv_w, ssd_conv_b,
           ssd_a_log, ssd_dt_bias, ssd_d, ssd_norm_g, w_ple_gate, w_ple_proj):
    depth = w_in.shape[0]
    alpha = (2 * depth) ** 0.25
    weights = [_layer_weights(l, w_in, w_out, hgrn_lb_logits, w_ple_gate, w_ple_proj) for l in range(depth)]

    def trunk(x, p):
        cos2, sin2 = _rope_tables(x.shape[1])
        xb = x.astype(BF)
        for l in range(depth):
            w = weights[l]
            mixed = (
                _retention(xb, w["ret_kv"], w["ret_qg"], ret_log_rate[l], ret_norm_g[l], cos2, sin2),
                _hgrn2(xb, w["hgrn_zv"], w["hgrn_q"], w["lb"], hgrn_norm_g[l]),
                _mlstm(xb, w["ml_kv"], w["ml_q"], w["ml_gates"], mlstm_i_bias[l], mlstm_f_bias[l],
                       mlstm_norm_g[l]),
                _ssd(xb, w["w_z"], w["w_xbc"], w["w_dt"], ssd_conv_w[l], ssd_conv_b[l], ssd_a_log[l],
                     ssd_dt_bias[l], ssd_d[l], ssd_norm_g[l]),
            )
            x, xb = _out_layer(x, p[l], mixed, w["w_out"], ln_g[l], ln_b[l], w["w_gate"], w["w_proj"], alpha)
        return x

    return trunk(x_prompt, p_prompt), trunk(x_sample, p_sample)
```

```python
import functools
import math

import numpy as np
import jax
import jax.numpy as jnp
from jax import lax
from jax.experimental import pallas as pl
from jax.experimental.pallas import tpu as pltpu

BF = jnp.bfloat16
F32 = jnp.float32

CHUNK = 128
TBLK = 1024
SUB = 512
HEAD = 128
BRANCH = 512
SSD_HEADDIM = 64
SSD_HEADS = 8
SSD_GROUPS = 2
SSD_CONV = 5
HALO = 16
ROPE_BASE = 10000.0
NORM_EPS = 1e-5
OUT_ROWS = 512
VMEM_LIMIT = 56 * 1024 * 1024


def _dot(a, b):
    return jnp.dot(a, b, preferred_element_type=F32)


def _dot_nt(a, b):
    return lax.dot_general(a, b, (((1,), (1,)), ((), ())), preferred_element_type=F32)


def _dot_tn(a, b):
    return lax.dot_general(a, b, (((0,), (0,)), ((), ())), preferred_element_type=F32)


def _split3(x):
    hi = x.astype(BF)
    r = x - hi.astype(F32)
    mid = r.astype(BF)
    lo = (r - mid.astype(F32)).astype(BF)
    return hi, mid, lo


def _sel_dot(sel, x):
    hi, mid, lo = _split3(x)
    return _dot(sel, hi) + _dot(sel, mid) + _dot(sel, lo)


def _dot_sel(x, sel):
    hi, mid, lo = _split3(x)
    return _dot(hi, sel) + _dot(mid, sel) + _dot(lo, sel)


def _sigmoid(x):
    return 1.0 / (1.0 + jnp.exp(-x))


def _silu(x):
    return x * _sigmoid(x)


def _softplus(x):
    return jnp.maximum(x, 0.0) + jnp.log1p(jnp.exp(-jnp.abs(x)))


def _log_sigmoid(x):
    return jnp.minimum(x, 0.0) - jnp.log1p(jnp.exp(-jnp.abs(x)))


def _iota2(shape, axis):
    return lax.broadcasted_iota(jnp.int32, shape, axis)


def _tri_masks(c):
    row = _iota2((c, c), 0)
    col = _iota2((c, c), 1)
    return col <= row, col >= row


def _prefix_suffix_rows(x):
    pre = _sel_dot(_tri_masks(x.shape[0])[0].astype(BF), x)
    return pre, pre[x.shape[0] - 1:, :] - pre + x


def _prefix_suffix_lanes(x):
    n = x.shape[1]
    pre = _dot_sel(x, _tri_masks(n)[1].astype(BF))
    return pre, pre[:, n - 1:] - pre + x


def _group_norm(o, center):
    if center:
        o = o - jnp.mean(o, axis=-1, keepdims=True)
    return o * lax.rsqrt(jnp.mean(o * o, axis=-1, keepdims=True) + NORM_EPS)


def _blk_index(phase, c, nblk):
    return jnp.where(phase == 0, nblk - 1 - c, c)


def _sweep(nblk, bwd_proj, bwd_chunk, fwd_proj, fwd_chunk, bwd_init, fwd_init, lookahead):
    phase = pl.program_id(1)
    c = pl.program_id(2)
    per = SUB // CHUNK
    nsub = TBLK // SUB

    def walk(order, base, first_chunk, proj, chunk):
        if not lookahead:
            for j in order:
                chunk(base + j * CHUNK, first_chunk + j, proj(base + j * CHUNK))
            return
        ahead = proj(base + order[0] * CHUNK)
        for n, j in enumerate(order):
            cur = ahead
            if n + 1 < per:
                ahead = proj(base + order[n + 1] * CHUNK)
            chunk(base + j * CHUNK, first_chunk + j, cur)

    @pl.when(phase == 0)
    def _():
        pl.when(c == 0)(bwd_init)
        blk = nblk - 1 - c

        def body(i, carry):
            sub = nsub - 1 - i
            walk(list(reversed(range(per))), pl.multiple_of(sub * SUB, SUB), (blk * nsub + sub) * per,
                 bwd_proj, bwd_chunk)
            return carry

        lax.fori_loop(0, nsub, body, 0)

    @pl.when(phase == 1)
    def _():
        pl.when(c == 0)(fwd_init)

        def body(sub, carry):
            walk(list(range(per)), pl.multiple_of(sub * SUB, SUB), (c * nsub + sub) * per, fwd_proj, fwd_chunk)
            return carry

        lax.fori_loop(0, nsub, body, 0)


def _mixer_call(body, b, t, in_arrays, in_specs, scratch, name):
    nblk = t // TBLK
    return pl.pallas_call(
        functools.partial(body, nblk=nblk),
        grid=(b, 2, nblk),
        in_specs=in_specs,
        out_specs=pl.BlockSpec((1, TBLK, BRANCH),
                               lambda bi, ph, c: (bi, jnp.where(ph == 0, 0, c), 0)),
        out_shape=jax.ShapeDtypeStruct((b, t, BRANCH), BF),
        scratch_shapes=scratch,
        compiler_params=pltpu.CompilerParams(
            dimension_semantics=("arbitrary", "arbitrary", "arbitrary"),
            vmem_limit_bytes=VMEM_LIMIT),
        name=name,
    )(*in_arrays)


def _x_spec(nblk, d):
    return pl.BlockSpec((1, TBLK, d), lambda bi, ph, c: (bi, _blk_index(ph, c, nblk), 0))


def _full_spec(shape):
    zeros = (0,) * len(shape)
    return pl.BlockSpec(shape, lambda bi, ph, c: zeros)


def _rows(ci):
    return pl.ds(pl.multiple_of(ci * CHUNK, CHUNK), CHUNK)


def _rope(x, cos2, sin2):
    return x * cos2 + pltpu.roll(x, HEAD // 2, axis=1) * sin2


def _ret_body(lg_ref, x_ref, cos_ref, sin_ref, wkv_ref, wqg_ref, ng_ref, o_ref,
              sf_ref, sb_ref, sbs_ref, m_ref, kv_ref, *, nblk):
    c_sz = CHUNK
    nh = BRANCH // HEAD
    scale = HEAD ** -0.5

    def bwd_init():
        sb_ref[...] = jnp.zeros_like(sb_ref)

    def fwd_init():
        sf_ref[...] = jnp.zeros_like(sf_ref)
        lower, upper = _tri_masks(c_sz)
        dist = (_iota2((c_sz, c_sz), 0) - _iota2((c_sz, c_sz), 1)).astype(F32)
        for h in range(nh):
            fwd = jnp.where(lower, jnp.exp(lg_ref[0, h] * dist), 0.0)
            bwd = jnp.where(upper, jnp.exp(-lg_ref[1, h] * dist), 0.0)
            m_ref[h] = (fwd + bwd) * scale

    def bwd_proj(r0):
        return _dot(x_ref[0, pl.ds(r0, c_sz), :], wkv_ref[...])

    def fwd_proj(r0):
        return _dot(x_ref[0, pl.ds(r0, c_sz), :], wqg_ref[...])

    def bwd_chunk(r0, ci, kv):
        rows = pl.ds(r0, c_sz)
        pos = _iota2((c_sz, HEAD), 0).astype(F32)
        span = jnp.full((1, HEAD), float(c_sz), F32)
        cos2 = cos_ref[rows, :]
        sin2 = sin_ref[rows, :]
        for h in range(nh):
            k = _rope(kv[:, h * HEAD:(h + 1) * HEAD], cos2, sin2)
            v = kv[:, BRANCH + h * HEAD:BRANCH + (h + 1) * HEAD].astype(BF)
            kv_ref[_rows(ci), h * HEAD:(h + 1) * HEAD] = k.astype(BF)
            kv_ref[_rows(ci), BRANCH + h * HEAD:BRANCH + (h + 1) * HEAD] = v
            lg_b = lg_ref[1, h]
            sbs_ref[ci, h] = sb_ref[h].astype(BF)
            k_in = (k * (jnp.exp(lg_b * pos) * scale)).astype(BF)
            sb_ref[h] = jnp.exp(lg_b * span) * sb_ref[h] + _dot_tn(k_in, v)

    def fwd_chunk(r0, ci, qg):
        rows = pl.ds(r0, c_sz)
        pos = _iota2((c_sz, HEAD), 0).astype(F32)
        span = jnp.full((1, HEAD), float(c_sz), F32)
        cos2 = cos_ref[rows, :]
        sin2 = sin_ref[rows, :]
        for h in range(nh):
            sl = slice(h * HEAD, (h + 1) * HEAD)
            q = _rope(qg[:, sl], cos2, sin2).astype(BF)
            g = qg[:, BRANCH + h * HEAD:BRANCH + (h + 1) * HEAD]
            k = kv_ref[_rows(ci), sl]
            v = kv_ref[_rows(ci), BRANCH + h * HEAD:BRANCH + (h + 1) * HEAD]
            lg_f = lg_ref[0, h]
            lg_b = lg_ref[1, h]
            p = (_dot_nt(q, k) * m_ref[h]).astype(BF)
            o = _dot(p, v)
            states = jnp.concatenate([sf_ref[h].astype(BF), sbs_ref[ci, h]], axis=1)
            r = _dot(q, states)
            o = o + jnp.exp(lg_f * (pos + 1.0)) * r[:, :HEAD] + jnp.exp(lg_b * (c_sz - pos)) * r[:, HEAD:]
            k_in = (k.astype(F32) * (jnp.exp(lg_f * (c_sz - 1.0 - pos)) * scale)).astype(BF)
            sf_ref[h] = jnp.exp(lg_f * span) * sf_ref[h] + _dot_tn(k_in, v)
            y = _group_norm(o, True) * ng_ref[:, sl] * _silu(g)
            o_ref[0, rows, sl] = y.astype(o_ref.dtype)

    _sweep(nblk, bwd_proj, bwd_chunk, fwd_proj, fwd_chunk, bwd_init, fwd_init, lookahead=True)


def _retention(x, w_kv, w_qg, log_rate, norm_g, cos2, sin2):
    b, t, d = x.shape
    nblk = t // TBLK
    nc = t // CHUNK
    nh = BRANCH // HEAD
    lg = -jnp.exp(log_rate.astype(F32))
    in_specs = [
        pl.BlockSpec(memory_space=pltpu.SMEM),
        _x_spec(nblk, d),
        pl.BlockSpec((TBLK, HEAD), lambda bi, ph, c: (_blk_index(ph, c, nblk), 0)),
        pl.BlockSpec((TBLK, HEAD), lambda bi, ph, c: (_blk_index(ph, c, nblk), 0)),
        _full_spec(w_kv.shape),
        _full_spec(w_qg.shape),
        _full_spec((1, BRANCH)),
    ]
    scratch = [
        pltpu.VMEM((nh, HEAD, HEAD), F32),
        pltpu.VMEM((nh, HEAD, HEAD), F32),
        pltpu.VMEM((nc, nh, HEAD, HEAD), BF),
        pltpu.VMEM((nh, CHUNK, CHUNK), F32),
        pltpu.VMEM((t, 2 * BRANCH), BF),
    ]
    return _mixer_call(_ret_body, b, t, (lg, x, cos2, sin2, w_kv, w_qg, norm_g.reshape(1, BRANCH)),
                       in_specs, scratch, "retention")


HGRN_LEVELS = int(math.log2(CHUNK))


def _hgrn_masks():
    c = CHUNK
    t = np.arange(c)
    mask = np.zeros((2, HGRN_LEVELS + 1, c, c), np.float32)
    sign = np.zeros((2, HGRN_LEVELS, c), np.float32)
    for l in range(HGRN_LEVELS):
        s = 1 << l
        hi = (t % (2 * s)) >= s
        is_hi = hi[:, None]
        same = (t[:, None] // (2 * s)) == (t[None, :] // (2 * s))
        mask[0, l] = same & is_hi & ~is_hi.T
        mask[1, l] = same & ~is_hi & is_hi.T
        lower = 0.0 if s == 1 else -1.0
        sign[0, l] = np.where(hi, 1.0, lower)
        sign[1, l] = np.where(hi, lower, 1.0)
    mask[0, HGRN_LEVELS] = np.eye(c)
    sign = np.broadcast_to((sign * math.log2(math.e))[..., None], sign.shape + (HEAD,))
    return mask, np.ascontiguousarray(sign)


def _split_delta(cum, log_f, l, reverse):
    c, w = cum.shape
    s = 1 << l
    if s == 1:
        return log_f
    edge = s if reverse else s - 1
    if 2 * s >= 8:
        blocks = cum.reshape(c // (2 * s), 2 * s, w)
        at_split = jnp.broadcast_to(blocks[:, edge:edge + 1, :], blocks.shape).reshape(c, w)
    else:
        tiles = cum.reshape(c // 8, 8, w)
        sub = _iota2(tiles.shape, 1)
        at_split = None
        for blk in range(8 // (2 * s)):
            r = blk * 2 * s + edge
            row = jnp.broadcast_to(tiles[:, r:r + 1, :], tiles.shape)
            at_split = row if at_split is None else jnp.where(sub >= blk * 2 * s, row, at_split)
        at_split = at_split.reshape(c, w)
    return cum - at_split


def _hgrn_gates(z, lb):
    s = _sigmoid(z)
    log_f = jnp.log(lb + (1.0 - lb) * s)
    k = (1.0 - lb) * (1.0 - s)
    return k, log_f


def _hgrn_body(x_ref, wzv_ref, wq_ref, lb_ref, ng_ref, mask_ref, sign_ref, o_ref,
               sf_ref, sb_ref, sbs_ref, v_ref, *, nblk):
    c_sz = CHUNK
    nh = BRANCH // HEAD
    nl = HGRN_LEVELS

    def bwd_init():
        sb_ref[...] = jnp.zeros_like(sb_ref)

    def fwd_init():
        sf_ref[...] = jnp.zeros_like(sf_ref)

    def x_chunk(r0):
        return x_ref[0, pl.ds(r0, c_sz), :]

    def bwd_chunk(r0, ci, xb):
        zv = _dot(xb, wzv_ref[...])
        kb, lfb = _hgrn_gates(zv[:, :BRANCH], lb_ref[1:2, :])
        _, suf = _prefix_suffix_rows(lfb)
        total = suf[0:1, :]
        k_in = (kb * jnp.exp(total - suf)).astype(BF)
        dec = jnp.exp(total)
        v = zv[:, BRANCH:].astype(BF)
        v_ref[_rows(ci), :] = v
        for h in range(nh):
            sl = slice(h * HEAD, (h + 1) * HEAD)
            sbs_ref[ci, h] = sb_ref[h].astype(BF)
            sb_ref[h] = dec[:, sl] * sb_ref[h] + _dot_tn(v[:, sl], k_in[:, sl])

    def fwd_chunk(r0, ci, xb):
        rows = pl.ds(r0, c_sz)
        proj = _dot(xb, wq_ref[...])
        q = proj[:, :BRANCH]
        kf, lff = _hgrn_gates(proj[:, BRANCH:2 * BRANCH], lb_ref[0:1, :])
        kb, lfb = _hgrn_gates(proj[:, 2 * BRANCH:3 * BRANCH], lb_ref[1:2, :])
        g = proj[:, 3 * BRANCH:]
        v = v_ref[_rows(ci), :]

        scores = [None] * nh
        ksum = (kf + kb).astype(BF)
        qb = q.astype(BF)
        for h in range(nh):
            sl = slice(h * HEAD, (h + 1) * HEAD)
            scores[h] = _dot_nt(qb[:, sl], ksum[:, sl]) * mask_ref[0, nl]
        pre, _ = _prefix_suffix_rows(lff)
        _, suf = _prefix_suffix_rows(lfb)
        for dr, (kd, lfd, cum) in enumerate(((kf, lff, pre), (kb, lfb, suf))):
            for l in range(nl):
                sgn = jnp.concatenate([sign_ref[dr, l]] * nh, axis=1)
                e = jnp.exp2(_split_delta(cum, lfd, l, dr == 1) * sgn)
                u = (jnp.where(sgn > 0.0, q, kd) * e).astype(BF)
                for h in range(nh):
                    sl = slice(h * HEAD, (h + 1) * HEAD)
                    scores[h] = scores[h] + _dot_nt(u[:, sl], u[:, sl]) * mask_ref[dr, l]

        q_f = (q * jnp.exp(pre)).astype(BF)
        q_b = (q * jnp.exp(suf)).astype(BF)
        total = pre[c_sz - 1:c_sz, :]
        k_in = (kf * jnp.exp(total - pre)).astype(BF)
        dec = jnp.exp(total)
        for h in range(nh):
            sl = slice(h * HEAD, (h + 1) * HEAD)
            o = _dot(scores[h].astype(BF), v[:, sl])
            q_cat = jnp.concatenate([q_f[:, sl], q_b[:, sl]], axis=1)
            s_cat = jnp.concatenate([sf_ref[h].astype(BF), sbs_ref[ci, h]], axis=1)
            o = o + _dot_nt(q_cat, s_cat)
            sf_ref[h] = dec[:, sl] * sf_ref[h] + _dot_tn(v[:, sl], k_in[:, sl])
            y = _group_norm(o, False) * ng_ref[:, sl] * _silu(g[:, sl])
            o_ref[0, rows, sl] = y.astype(o_ref.dtype)

    _sweep(nblk, x_chunk, bwd_chunk, x_chunk, fwd_chunk, bwd_init, fwd_init, lookahead=False)


def _hgrn2(x, w_zv, w_q, lb, norm_g):
    b, t, d = x.shape
    nblk = t // TBLK
    nc = t // CHUNK
    nh = BRANCH // HEAD
    mask, sign = _hgrn_masks()
    mask = jnp.asarray(mask, F32)
    sign = jnp.asarray(sign, F32)
    in_specs = [
        _x_spec(nblk, d),
        _full_spec(w_zv.shape),
        _full_spec(w_q.shape),
        _full_spec((2, BRANCH)),
        _full_spec((1, BRANCH)),
        _full_spec(mask.shape),
        _full_spec(sign.shape),
    ]
    scratch = [
        pltpu.VMEM((nh, HEAD, HEAD), F32),
        pltpu.VMEM((nh, HEAD, HEAD), F32),
        pltpu.VMEM((nc, nh, HEAD, HEAD), BF),
        pltpu.VMEM((t, BRANCH), BF),
    ]
    return _mixer_call(_hgrn_body, b, t, (x, w_zv, w_q, lb, norm_g.reshape(1, BRANCH), mask, sign),
                       in_specs, scratch, "hgrn2")


ML_HEADS = 4
ML_AUG = 2 * HEAD


def _cummax_rows(a, reverse):
    n = a.shape[0]
    row = _iota2(a.shape, 0)
    k = 1
    while k < n:
        if reverse:
            shifted = pltpu.roll(a, n - k, axis=0)
            a = jnp.where(row < n - k, jnp.maximum(a, shifted), a)
        else:
            shifted = pltpu.roll(a, k, axis=0)
            a = jnp.where(row >= k, jnp.maximum(a, shifted), a)
        k *= 2
    return a


def _ml_body(x_ref, wkv_ref, wq_ref, wgc_ref, wgr_ref, bc_ref, br_ref, ng_ref, o_ref,
             cf_ref, cb_ref, cbs_ref, m_ref, mbs_ref, kv_ref, *, nblk):
    c_sz = CHUNK
    nh = ML_HEADS

    def gates(xb):
        is_fwd = _iota2((c_sz, HEAD), 1) < nh
        gc = _dot(xb, wgc_ref[...]) + bc_ref[...]
        ipre = gc[:, :HEAD]
        pre, suf = _prefix_suffix_rows(_log_sigmoid(gc[:, HEAD:]))
        cum = jnp.where(is_fwd, pre, suf)
        a = ipre - cum
        run_max = jnp.where(is_fwd, _cummax_rows(a, False), _cummax_rows(a, True))
        return a, cum, run_max

    def edge(arr):
        return jnp.where(_iota2((1, HEAD), 1) < nh, arr[c_sz - 1:c_sz, :], arr[0:1, :])

    def ones_col():
        return (_iota2((c_sz, HEAD), 1) == 0).astype(BF)

    def update_state(state_ref, lane0, k_all, v_aug, wk, sc):
        for h in range(nh):
            l = lane0 + h
            kw = (k_all[:, h * HEAD:(h + 1) * HEAD] * wk[:, l:l + 1]).astype(BF)
            state_ref[h] = sc[:, l:l + 1] * state_ref[h] + _dot_tn(kw, v_aug[h])

    def bwd_init():
        cb_ref[...] = jnp.zeros_like(cb_ref)
        m_ref[...] = jnp.zeros_like(m_ref)

    def fwd_init():
        cf_ref[...] = jnp.zeros_like(cf_ref)
        m_ref[...] = jnp.zeros_like(m_ref)

    def x_chunk(r0):
        return x_ref[0, pl.ds(r0, c_sz), :]

    def bwd_chunk(r0, ci, xb):
        a, cum, run_max = gates(xb)
        m_prev = m_ref[...]
        mbs_ref[ci] = m_prev
        for h in range(nh):
            cbs_ref[ci, h] = cb_ref[h].astype(BF)
        mu_e = edge(jnp.maximum(m_prev, run_max))
        kv = _dot(xb, wkv_ref[...])
        kv_ref[_rows(ci), :] = kv.astype(BF)
        v_aug = [jnp.concatenate([kv[:, BRANCH + h * HEAD:BRANCH + (h + 1) * HEAD].astype(BF), ones_col()],
                                 axis=1) for h in range(nh)]
        update_state(cb_ref, nh, kv[:, :BRANCH], v_aug, jnp.exp(a - mu_e), jnp.exp(m_prev - mu_e))
        m_ref[...] = edge(cum) + mu_e

    def fwd_chunk(r0, ci, xb):
        rows = pl.ds(r0, c_sz)
        a, cum, run_max = gates(xb)
        lower, upper = _tri_masks(c_sz)
        m_prev = jnp.where(_iota2((1, HEAD), 1) < nh, m_ref[...], mbs_ref[ci])
        mu = jnp.maximum(m_prev, run_max)
        s_inter = jnp.exp(m_prev - mu)
        thr = jnp.exp(-(cum + mu))

        gr = _dot_nt(wgr_ref[...], xb) + br_ref[...]
        ipre_r = gr[:2 * nh]
        pre_r, suf_r = _prefix_suffix_lanes(_log_sigmoid(gr[2 * nh:]))
        a_r = ipre_r - jnp.where(_iota2((2 * nh, c_sz), 0) < nh, pre_r, suf_r)

        proj = _dot(xb, wq_ref[...])
        kvc = kv_ref[_rows(ci), :]
        v_aug = [jnp.concatenate([kvc[:, BRANCH + h * HEAD:BRANCH + (h + 1) * HEAD], ones_col()], axis=1)
                 for h in range(nh)]
        for h in range(nh):
            sl = slice(h * HEAD, (h + 1) * HEAD)
            q = proj[:, sl].astype(BF)
            qk = _dot_nt(q, kvc[:, sl])
            states = jnp.concatenate([cf_ref[h].astype(BF), cbs_ref[ci, h]], axis=1)
            r = _dot(q, states)
            hsum = None
            for dr in range(2):
                l = dr * nh + h
                wgt = jnp.where(lower if dr == 0 else upper,
                                jnp.exp(a_r[l:l + 1, :] - mu[:, l:l + 1]), 0.0)
                tot = s_inter[:, l:l + 1] * r[:, dr * ML_AUG:(dr + 1) * ML_AUG] + _dot((qk * wgt).astype(BF), v_aug[h])
                den = jnp.maximum(jnp.abs(tot[:, HEAD:HEAD + 1]), thr[:, l:l + 1])
                out = tot[:, :HEAD] / den
                hsum = out if hsum is None else hsum + out
            o_gate = proj[:, BRANCH + h * HEAD:BRANCH + (h + 1) * HEAD]
            g = proj[:, 2 * BRANCH + h * HEAD:2 * BRANCH + (h + 1) * HEAD]
            y = _group_norm(_sigmoid(o_gate) * hsum, True) * ng_ref[:, sl] * _silu(g)
            o_ref[0, rows, sl] = y.astype(o_ref.dtype)

        mu_e = edge(mu)
        update_state(cf_ref, 0, kvc[:, :BRANCH].astype(F32), v_aug, jnp.exp(a - mu_e), jnp.exp(m_prev - mu_e))
        m_ref[...] = edge(cum) + mu_e

    _sweep(nblk, x_chunk, bwd_chunk, x_chunk, fwd_chunk, bwd_init, fwd_init, lookahead=False)


def _mlstm(x, w_kv, w_q, w_gates, i_bias, f_bias, norm_g):
    b, t, d = x.shape
    nblk = t // TBLK
    nc = t // CHUNK
    nh = ML_HEADS
    pad = HEAD - 2 * nh
    w_i = w_gates[:, :2 * nh]
    w_f = w_gates[:, 2 * nh:]
    wgc = jnp.concatenate([jnp.pad(w_i, ((0, 0), (0, pad))), jnp.pad(w_f, ((0, 0), (0, pad)))], axis=1)
    wgr = w_gates.T
    bias = jnp.concatenate([i_bias.reshape(-1), f_bias.reshape(-1)]).astype(F32)
    bc = jnp.concatenate([jnp.pad(bias[:2 * nh], (0, pad)), jnp.pad(bias[2 * nh:], (0, pad))]).reshape(1, 2 * HEAD)
    br = bias.reshape(4 * nh, 1)
    in_specs = [
        _x_spec(nblk, d),
        _full_spec(w_kv.shape),
        _full_spec(w_q.shape),
        _full_spec(wgc.shape),
        _full_spec(wgr.shape),
        _full_spec(bc.shape),
        _full_spec(br.shape),
        _full_spec((1, BRANCH)),
    ]
    scratch = [
        pltpu.VMEM((nh, HEAD, ML_AUG), F32),
        pltpu.VMEM((nh, HEAD, ML_AUG), F32),
        pltpu.VMEM((nc, nh, HEAD, ML_AUG), BF),
        pltpu.VMEM((1, HEAD), F32),
        pltpu.VMEM((nc, 1, HEAD), F32),
        pltpu.VMEM((t, 2 * BRANCH), BF),
    ]
    return _mixer_call(_ml_body, b, t,
                       (x, w_kv, w_q, wgc.astype(BF), wgr.astype(BF), bc, br, norm_g.reshape(1, BRANCH)),
                       in_specs, scratch, "mlstm")


SSD_XBC = BRANCH + 2 * SSD_GROUPS * HEAD
SSD_GW = BRANCH // SSD_GROUPS


def _ssd_body(x_ref, xp_ref, xn_ref, wz_ref, wx_ref, wdc_ref, wdr_ref, dbc_ref, dbr_ref,
              ac_ref, ar_ref, cw_ref, cb_ref, dsk_ref, ng_ref, ex_ref, o_ref,
              xe_ref, sf_ref, sb_ref, sbs_ref, act_ref, *, nblk):
    c_sz = CHUNK
    nh = SSD_HEADS
    per = TBLK // CHUNK

    def dt_columns(xb):
        is_fwd = _iota2((c_sz, HEAD), 1) < nh
        dt = _softplus(_dot(xb, wdc_ref[...]) + dbc_ref[...])
        pre, suf = _prefix_suffix_rows(dt * ac_ref[...])
        return dt, jnp.where(is_fwd, pre, suf)

    def expand(a, dr):
        hi, mid, _ = _split3(a)
        e = ex_ref[:, dr * BRANCH:(dr + 1) * BRANCH]
        return _dot(hi, e) + _dot(mid, e)

    def update_state(state_ref, dr, xs, bm, dt, cum):
        cum_e = jnp.where(_iota2((1, HEAD), 1) < nh, cum[c_sz - 1:c_sz, :], cum[0:1, :])
        w_state = jnp.exp(cum_e - cum) * dt
        dec = jnp.exp(jnp.broadcast_to(cum_e, (8, HEAD)))
        xw = (xs * expand(w_state, dr)).astype(BF)
        d512 = expand(dec, dr)[0:1, :]
        for g in range(SSD_GROUPS):
            gs = slice(g * SSD_GW, (g + 1) * SSD_GW)
            state_ref[g] = d512[:, gs] * state_ref[g] + _dot_tn(bm[:, g * HEAD:(g + 1) * HEAD], xw[:, gs])

    def bwd_init():
        sb_ref[...] = jnp.zeros_like(sb_ref)

    def fwd_init():
        sf_ref[...] = jnp.zeros_like(sf_ref)

    def project_block():
        blk = nblk - 1 - pl.program_id(2)

        def body(j, carry):
            r0 = pl.multiple_of(j * c_sz, c_sz)
            xe_ref[pl.ds(HALO + r0, c_sz), :] = _dot(x_ref[0, pl.ds(r0, c_sz), :], wx_ref[...])
            return carry

        lax.fori_loop(0, per, body, 0)
        halo = jnp.concatenate([xp_ref[0, 0], xn_ref[0, 0]], axis=0)
        ph = _dot(halo, wx_ref[...])
        xe_ref[0:HALO, :] = ph[:HALO] * jnp.where(blk > 0, 1.0, 0.0)
        xe_ref[HALO + TBLK:, :] = ph[HALO:] * jnp.where(blk < nblk - 1, 1.0, 0.0)

    def x_chunk(r0):
        return x_ref[0, pl.ds(r0, c_sz), :]

    def bwd_chunk(r0, ci, xb):
        half = (SSD_CONV - 1) // 2
        win = xe_ref[pl.ds(r0, c_sz + 2 * HALO), :]
        conv = cb_ref[...]
        for k in range(SSD_CONV):
            conv = conv + cw_ref[k:k + 1, :] * win[HALO - half + k:HALO - half + k + c_sz, :]
        act = _silu(conv)
        act_ref[_rows(ci), :] = act.astype(BF)
        dt, cum = dt_columns(xb)
        for g in range(SSD_GROUPS):
            sbs_ref[ci, g] = sb_ref[g].astype(BF)
        update_state(sb_ref, 1, act[:, :BRANCH], act[:, BRANCH:BRANCH + SSD_GROUPS * HEAD].astype(BF), dt, cum)

    def fwd_chunk(r0, ci, xb):
        rows = pl.ds(r0, c_sz)
        act = act_ref[_rows(ci), :]
        xs = act[:, :BRANCH].astype(F32)
        bm = act[:, BRANCH:BRANCH + SSD_GROUPS * HEAD]
        cm = act[:, BRANCH + SSD_GROUPS * HEAD:]
        dt, cum = dt_columns(xb)
        dt_r = _softplus(_dot_nt(wdr_ref[...], xb) + dbr_ref[...])
        pre_r, suf_r = _prefix_suffix_lanes(dt_r * ar_ref[...])
        cum_r = jnp.where(_iota2((2 * nh, c_sz), 0) < nh, pre_r, suf_r)
        shifted_r = cum_r - jnp.log(dt_r)
        below = _iota2((c_sz, c_sz), 1) < _iota2((c_sz, c_sz), 0)
        diag = _iota2((c_sz, c_sz), 1) == _iota2((c_sz, c_sz), 0)
        z = _dot(xb, wz_ref[...])
        lane_half = _iota2((c_sz, HEAD), 1) < SSD_HEADDIM
        ecum = jnp.exp(cum)
        ecum_f = expand(ecum, 0)
        ecum_b = expand(ecum, 1)
        hpg = SSD_HEADS // SSD_GROUPS
        ys = []
        for g in range(SSD_GROUPS):
            cg = cm[:, g * HEAD:(g + 1) * HEAD]
            gmat = _dot_nt(cg, bm[:, g * HEAD:(g + 1) * HEAD])
            gs = slice(g * SSD_GW, (g + 1) * SSD_GW)
            inter = (ecum_f[:, gs] * _dot(cg, sf_ref[g].astype(BF)) +
                     ecum_b[:, gs] * _dot(cg, sbs_ref[ci, g]))
            pieces = []
            for pair in range(hpg // 2):
                acc = None
                for side in range(2):
                    h = g * hpg + 2 * pair + side
                    lf, lbk = h, nh + h
                    m = jnp.exp(jnp.where(below, cum[:, lf:lf + 1] - shifted_r[lf:lf + 1, :],
                                          cum[:, lbk:lbk + 1] - shifted_r[lbk:lbk + 1, :]))
                    m = m + jnp.where(diag, dt_r[lf:lf + 1, :], 0.0)
                    xpair = xs[:, (h // 2) * HEAD:(h // 2 + 1) * HEAD]
                    xh = jnp.where(lane_half if side == 0 else jnp.logical_not(lane_half), xpair, 0.0).astype(BF)
                    part = _dot((gmat * m).astype(BF), xh)
                    acc = part if acc is None else acc + part
                pieces.append(acc)
            ys.append(jnp.concatenate(pieces, axis=1) + inter)
        y = jnp.concatenate(ys, axis=1) + dsk_ref[...] * xs
        y = y * _silu(z)
        for g in range(SSD_GROUPS):
            gs = slice(g * SSD_GW, (g + 1) * SSD_GW)
            o_ref[0, rows, gs] = (_group_norm(y[:, gs], False) * ng_ref[:, gs]).astype(o_ref.dtype)
        update_state(sf_ref, 0, xs, bm, dt, cum)

    pl.when(pl.program_id(1) == 0)(project_block)
    _sweep(nblk, x_chunk, bwd_chunk, x_chunk, fwd_chunk, bwd_init, fwd_init, lookahead=False)


def _ssd(x, w_z, w_xbc, w_dt, conv_w, conv_b, a_log, dt_bias, d_skip, norm_g):
    b, t, d = x.shape
    nblk = t // TBLK
    nc = t // CHUNK
    nh = SSD_HEADS
    pad = HEAD - 2 * nh
    wdc = jnp.pad(w_dt, ((0, 0), (0, pad)))
    wdr = w_dt.T
    db = dt_bias.reshape(-1).astype(F32)
    a = (-jnp.exp(a_log.astype(F32))).reshape(-1)
    dbc = jnp.pad(db, (0, pad)).reshape(1, HEAD)
    ac = jnp.pad(a, (0, pad)).reshape(1, HEAD)
    dbr = db.reshape(2 * nh, 1)
    ar = a.reshape(2 * nh, 1)
    cw = jnp.pad(conv_w.astype(F32), ((0, 8 - SSD_CONV), (0, 0)))
    dsk = jnp.repeat(d_skip.astype(F32), SSD_HEADDIM).reshape(1, BRANCH)
    ex = np.zeros((HEAD, 2 * BRANCH), np.float32)
    for l in range(2 * nh):
        ex[l, l * SSD_HEADDIM:(l + 1) * SSD_HEADDIM] = 1.0
    ex = jnp.asarray(ex, BF)
    xh = x.reshape(b, t // HALO, HALO, d)
    per = TBLK // HALO
    last = t // HALO - 1
    in_specs = [
        _x_spec(nblk, d),
        pl.BlockSpec((1, 1, HALO, d),
                     lambda bi, ph, c: (bi, jnp.maximum(_blk_index(ph, c, nblk) * per - 1, 0), 0, 0)),
        pl.BlockSpec((1, 1, HALO, d),
                     lambda bi, ph, c: (bi, jnp.minimum((_blk_index(ph, c, nblk) + 1) * per, last), 0, 0)),
        _full_spec(w_z.shape),
        _full_spec(w_xbc.shape),
        _full_spec(wdc.shape),
        _full_spec(wdr.shape),
        _full_spec(dbc.shape),
        _full_spec(dbr.shape),
        _full_spec(ac.shape),
        _full_spec(ar.shape),
        _full_spec(cw.shape),
        _full_spec((1, SSD_XBC)),
        _full_spec((1, BRANCH)),
        _full_spec((1, BRANCH)),
        _full_spec(ex.shape),
    ]
    scratch = [
        pltpu.VMEM((TBLK + 2 * HALO, SSD_XBC), F32),
        pltpu.VMEM((SSD_GROUPS, HEAD, SSD_GW), F32),
        pltpu.VMEM((SSD_GROUPS, HEAD, SSD_GW), F32),
        pltpu.VMEM((nc, SSD_GROUPS, HEAD, SSD_GW), BF),
        pltpu.VMEM((t, SSD_XBC), BF),
    ]
    return _mixer_call(_ssd_body, b, t,
                       (x, xh, xh, w_z, w_xbc, wdc.astype(BF), wdr.astype(BF), dbc, dbr, ac, ar, cw,
                        conv_b.astype(F32).reshape(1, SSD_XBC), dsk, norm_g.reshape(1, BRANCH), ex),
                       in_specs, scratch, "ssd")


def _out_body(x_ref, p_ref, m0_ref, m1_ref, m2_ref, m3_ref, wo_ref, lg_ref, lb_ref, wg_ref, wp_ref,
              o_ref, ob_ref, *, alpha):
    y = None
    for i, m_ref in enumerate((m0_ref, m1_ref, m2_ref, m3_ref)):
        part = _dot(m_ref[0], wo_ref[i * BRANCH:(i + 1) * BRANCH, :])
        y = part if y is None else y + part
    r = alpha * x_ref[0] + y
    mu = jnp.mean(r, axis=-1, keepdims=True)
    rc = r - mu
    var = jnp.mean(rc * rc, axis=-1, keepdims=True)
    xn = rc * lax.rsqrt(var + NORM_EPS) * lg_ref[...] + lb_ref[...]
    gate = _sigmoid(_dot(xn.astype(BF), wg_ref[...]))
    out = xn + gate * _dot(p_ref[0].astype(BF), wp_ref[...])
    o_ref[0] = out
    ob_ref[0] = out.astype(BF)


def _out_layer(x, p, mixed, w_out, ln_g, ln_b, w_gate, w_proj, alpha):
    b, t, d = x.shape
    dp = p.shape[-1]
    rows = OUT_ROWS
    row_spec = lambda width: pl.BlockSpec((1, rows, width), lambda bi, r: (bi, r, 0))
    full = lambda shape: pl.BlockSpec(shape, lambda bi, r: (0,) * len(shape))
    return pl.pallas_call(
        functools.partial(_out_body, alpha=alpha),
        grid=(b, t // rows),
        in_specs=[row_spec(d), row_spec(dp)] + [row_spec(BRANCH)] * 4 +
                 [full(w_out.shape), full((1, d)), full((1, d)), full(w_gate.shape), full(w_proj.shape)],
        out_specs=(row_spec(d), row_spec(d)),
        out_shape=(jax.ShapeDtypeStruct((b, t, d), F32), jax.ShapeDtypeStruct((b, t, d), BF)),
        compiler_params=pltpu.CompilerParams(
            dimension_semantics=("arbitrary", "arbitrary"), vmem_limit_bytes=VMEM_LIMIT),
        name="out_proj",
    )(x, p, *mixed, w_out, ln_g.reshape(1, d), ln_b.reshape(1, d), w_gate, w_proj)


def _rope_tables(t):
    inv = ROPE_BASE ** (-jnp.arange(0, HEAD, 2, dtype=F32) / HEAD)
    ang = jnp.arange(t, dtype=F32)[:, None] * inv[None, :]
    cos = jnp.cos(ang)
    sin = jnp.sin(ang)
    return jnp.concatenate([cos, cos], axis=1), jnp.concatenate([-sin, sin], axis=1)


def _layer_weights(l, w_in, w_out, hgrn_lb_logits, w_ple_gate, w_ple_proj):
    wl = w_in[l]
    nb = BRANCH
    o1 = 4 * nb
    o2 = o1 + 5 * nb
    o3 = o2 + 5 * nb + 4 * ML_HEADS

    def cols(*ranges):
        return jnp.concatenate([wl[:, a:b] for a, b in ranges], axis=1).astype(BF)

    ret_kv = cols((nb, 3 * nb))
    ret_qg = cols((0, nb), (3 * nb, 4 * nb))
    hg = o1
    hgrn_zv = cols((hg + 2 * nb, hg + 4 * nb))
    hgrn_q = cols((hg, hg + 3 * nb), (hg + 4 * nb, hg + 5 * nb))
    ml = o2
    ml_kv = jnp.concatenate([wl[:, ml + nb:ml + 2 * nb] * (HEAD ** -0.5), wl[:, ml + 2 * nb:ml + 3 * nb]],
                            axis=1).astype(BF)
    ml_q = cols((ml, ml + nb), (ml + 3 * nb, ml + 5 * nb))
    ml_gates = wl[:, ml + 5 * nb:o3]
    w_z = cols((o3, o3 + nb))
    w_xbc = cols((o3 + nb, o3 + nb + SSD_XBC))
    w_dt = wl[:, o3 + nb + SSD_XBC:]
    lb_w = jax.nn.softmax(hgrn_lb_logits.astype(F32), axis=0)
    lb = (jnp.cumsum(lb_w, axis=0) - lb_w[0:1])[l]
    return dict(ret_kv=ret_kv, ret_qg=ret_qg, hgrn_zv=hgrn_zv, hgrn_q=hgrn_q, ml_kv=ml_kv, ml_q=ml_q,
                ml_gates=ml_gates, w_z=w_z, w_xbc=w_xbc, w_dt=w_dt, lb=lb, w_out=w_out[l].astype(BF),
                w_gate=w_ple_gate[l].astype(BF), w_proj=w_ple_proj[l].astype(BF))


def kernel(x_prompt, x_sample, p_prompt, p_sample, w_in, w_out, ln_g, ln_b, ret_log_rate, ret_norm_g,
           hgrn_lb_logits, hgrn_norm_g, mlstm_i_bias, mlstm_f_bias, mlstm_norm_g, ssd_conv_w, ssd_conv_b,
           ssd_a_log, ssd_dt_bias, ssd_d, ssd_norm_g, w_ple_gate, w_ple_proj):
    depth = w_in.shape[0]
    alpha = (2 * depth) ** 0.25
    weights = [_layer_weights(l, w_in, w_out, hgrn_lb_logits, w_ple_gate, w_ple_proj) for l in range(depth)]

    def trunk(x, p):
        cos2, sin2 = _rope_tables(x.shape[1])
        xb = x.astype(BF)
        for l in range(depth):
            w = weights[l]
            mixed = (
                _retention(xb, w["ret_kv"], w["ret_qg"], ret_log_rate[l], ret_norm_g[l], cos2, sin2),
                _hgrn2(xb, w["hgrn_zv"], w["hgrn_q"], w["lb"], hgrn_norm_g[l]),
                _mlstm(xb, w["ml_kv"], w["ml_q"], w["ml_gates"], mlstm_i_bias[l], mlstm_f_bias[l],
                       mlstm_norm_g[l]),
                _ssd(xb, w["w_z"], w["w_xbc"], w["w_dt"], ssd_conv_w[l], ssd_conv_b[l], ssd_a_log[l],
                     ssd_dt_bias[l], ssd_d[l], ssd_norm_g[l]),
            )
            x, xb = _out_layer(x, p[l], mixed, w["w_out"], ln_g[l], ln_b[l], w["w_gate"], w["w_proj"], alpha)
        return x

    return trunk(x_prompt, p_prompt), trunk(x_sample, p_sample)
```

```python
import functools
import math

import numpy as np
import jax
import jax.numpy as jnp
from jax import lax
from jax.experimental import pallas as pl
from jax.experimental.pallas import tpu as pltpu

BF = jnp.bfloat16
F32 = jnp.float32

CHUNK = 128
TBLK = 1024
SUB = 512
HEAD = 128
BRANCH = 512
SSD_HEADDIM = 64
SSD_HEADS = 8
SSD_GROUPS = 2
SSD_CONV = 5
HALO = 16
ROPE_BASE = 10000.0
NORM_EPS = 1e-5
OUT_ROWS = 512
VMEM_LIMIT = 56 * 1024 * 1024


def _dot(a, b):
    return jnp.dot(a, b, preferred_element_type=F32)


def _dot_nt(a, b):
    return lax.dot_general(a, b, (((1,), (1,)), ((), ())), preferred_element_type=F32)


def _dot_tn(a, b):
    return lax.dot_general(a, b, (((0,), (0,)), ((), ())), preferred_element_type=F32)


def _split3(x):
    hi = x.astype(BF)
    r = x - hi.astype(F32)
    mid = r.astype(BF)
    lo = (r - mid.astype(F32)).astype(BF)
    return hi, mid, lo


def _sel_dot(sel, x):
    hi, mid, lo = _split3(x)
    return _dot(sel, hi) + _dot(sel, mid) + _dot(sel, lo)


def _dot_sel(x, sel):
    hi, mid, lo = _split3(x)
    return _dot(hi, sel) + _dot(mid, sel) + _dot(lo, sel)


def _sigmoid(x):
    return 1.0 / (1.0 + jnp.exp(-x))


def _silu(x):
    return x * _sigmoid(x)


def _softplus(x):
    return jnp.maximum(x, 0.0) + jnp.log1p(jnp.exp(-jnp.abs(x)))


def _log_sigmoid(x):
    return jnp.minimum(x, 0.0) - jnp.log1p(jnp.exp(-jnp.abs(x)))


def _iota2(shape, axis):
    return lax.broadcasted_iota(jnp.int32, shape, axis)


def _tri_masks(c):
    row = _iota2((c, c), 0)
    col = _iota2((c, c), 1)
    return col <= row, col >= row


def _prefix_suffix_rows(x):
    pre = _sel_dot(_tri_masks(x.shape[0])[0].astype(BF), x)
    return pre, pre[x.shape[0] - 1:, :] - pre + x


def _prefix_suffix_lanes(x):
    n = x.shape[1]
    pre = _dot_sel(x, _tri_masks(n)[1].astype(BF))
    return pre, pre[:, n - 1:] - pre + x


def _group_norm(o, center):
    if center:
        o = o - jnp.mean(o, axis=-1, keepdims=True)
    return o * lax.rsqrt(jnp.mean(o * o, axis=-1, keepdims=True) + NORM_EPS)


def _blk_index(phase, c, nblk):
    return jnp.where(phase == 0, nblk - 1 - c, c)


def _sweep(nblk, bwd_proj, bwd_chunk, fwd_proj, fwd_chunk, bwd_init, fwd_init, lookahead):
    phase = pl.program_id(1)
    c = pl.program_id(2)
    per = SUB // CHUNK
    nsub = TBLK // SUB

    def walk(order, base, first_chunk, proj, chunk):
        if not lookahead:
            for j in order:
                chunk(base + j * CHUNK, first_chunk + j, proj(base + j * CHUNK))
            return
        ahead = proj(base + order[0] * CHUNK)
        for n, j in enumerate(order):
            cur = ahead
            if n + 1 < per:
                ahead = proj(base + order[n + 1] * CHUNK)
            chunk(base + j * CHUNK, first_chunk + j, cur)

    @pl.when(phase == 0)
    def _():
        pl.when(c == 0)(bwd_init)
        blk = nblk - 1 - c

        def body(i, carry):
            sub = nsub - 1 - i
            walk(list(reversed(range(per))), pl.multiple_of(sub * SUB, SUB), (blk * nsub + sub) * per,
                 bwd_proj, bwd_chunk)
            return carry

        lax.fori_loop(0, nsub, body, 0)

    @pl.when(phase == 1)
    def _():
        pl.when(c == 0)(fwd_init)

        def body(sub, carry):
            walk(list(range(per)), pl.multiple_of(sub * SUB, SUB), (c * nsub + sub) * per, fwd_proj, fwd_chunk)
            return carry

        lax.fori_loop(0, nsub, body, 0)


def _mixer_call(body, b, t, in_arrays, in_specs, scratch, name):
    nblk = t // TBLK
    return pl.pallas_call(
        functools.partial(body, nblk=nblk),
        grid=(b, 2, nblk),
        in_specs=in_specs,
        out_specs=pl.BlockSpec((1, TBLK, BRANCH),
                               lambda bi, ph, c: (bi, jnp.where(ph == 0, 0, c), 0)),
        out_shape=jax.ShapeDtypeStruct((b, t, BRANCH), BF),
        scratch_shapes=scratch,
        compiler_params=pltpu.CompilerParams(
            dimension_semantics=("arbitrary", "arbitrary", "arbitrary"),
            vmem_limit_bytes=VMEM_LIMIT),
        name=name,
    )(*in_arrays)


def _x_spec(nblk, d):
    return pl.BlockSpec((1, TBLK, d), lambda bi, ph, c: (bi, _blk_index(ph, c, nblk), 0))


def _full_spec(shape):
    zeros = (0,) * len(shape)
    return pl.BlockSpec(shape, lambda bi, ph, c: zeros)


def _rows(ci):
    return pl.ds(pl.multiple_of(ci * CHUNK, CHUNK), CHUNK)


def _rope(x, cos2, sin2):
    return x * cos2 + pltpu.roll(x, HEAD // 2, axis=1) * sin2


def _ret_body(lg_ref, x_ref, cos_ref, sin_ref, wkv_ref, wqg_ref, ng_ref, o_ref,
              sf_ref, sb_ref, sbs_ref, m_ref, kv_ref, *, nblk):
    c_sz = CHUNK
    nh = BRANCH // HEAD
    scale = HEAD ** -0.5

    def bwd_init():
        sb_ref[...] = jnp.zeros_like(sb_ref)

    def fwd_init():
        sf_ref[...] = jnp.zeros_like(sf_ref)
        lower, upper = _tri_masks(c_sz)
        dist = (_iota2((c_sz, c_sz), 0) - _iota2((c_sz, c_sz), 1)).astype(F32)
        for h in range(nh):
            fwd = jnp.where(lower, jnp.exp(lg_ref[0, h] * dist), 0.0)
            bwd = jnp.where(upper, jnp.exp(-lg_ref[1, h] * dist), 0.0)
            m_ref[h] = (fwd + bwd) * scale

    def bwd_proj(r0):
        return _dot(x_ref[0, pl.ds(r0, c_sz), :], wkv_ref[...])

    def fwd_proj(r0):
        return _dot(x_ref[0, pl.ds(r0, c_sz), :], wqg_ref[...])

    def bwd_chunk(r0, ci, kv):
        rows = pl.ds(r0, c_sz)
        pos = _iota2((c_sz, HEAD), 0).astype(F32)
        span = jnp.full((1, HEAD), float(c_sz), F32)
        cos2 = cos_ref[rows, :]
        sin2 = sin_ref[rows, :]
        heads = range(nh)
        k = [_rope(kv[:, h * HEAD:(h + 1) * HEAD], cos2, sin2) for h in heads]
        v = [kv[:, BRANCH + h * HEAD:BRANCH + (h + 1) * HEAD].astype(BF) for h in heads]
        for h in heads:
            kv_ref[_rows(ci), h * HEAD:(h + 1) * HEAD] = k[h].astype(BF)
            kv_ref[_rows(ci), BRANCH + h * HEAD:BRANCH + (h + 1) * HEAD] = v[h]
            sbs_ref[ci, h] = sb_ref[h].astype(BF)
        k_in = [(k[h] * (jnp.exp(lg_ref[1, h] * pos) * scale)).astype(BF) for h in heads]
        for h in heads:
            sb_ref[h] = jnp.exp(lg_ref[1, h] * span) * sb_ref[h] + _dot_tn(k_in[h], v[h])

    def fwd_chunk(r0, ci, qg):
        rows = pl.ds(r0, c_sz)
        pos = _iota2((c_sz, HEAD), 0).astype(F32)
        span = jnp.full((1, HEAD), float(c_sz), F32)
        cos2 = cos_ref[rows, :]
        sin2 = sin_ref[rows, :]
        heads = range(nh)
        hs = [slice(h * HEAD, (h + 1) * HEAD) for h in heads]
        q = [_rope(qg[:, hs[h]], cos2, sin2).astype(BF) for h in heads]
        k = [kv_ref[_rows(ci), hs[h]] for h in heads]
        v = [kv_ref[_rows(ci), BRANCH + h * HEAD:BRANCH + (h + 1) * HEAD] for h in heads]
        p = [(_dot_nt(q[h], k[h]) * m_ref[h]).astype(BF) for h in heads]
        r = [_dot(q[h], jnp.concatenate([sf_ref[h].astype(BF), sbs_ref[ci, h]], axis=1)) for h in heads]
        o = [_dot(p[h], v[h]) for h in heads]
        o = [o[h] + jnp.exp(lg_ref[0, h] * (pos + 1.0)) * r[h][:, :HEAD] +
             jnp.exp(lg_ref[1, h] * (c_sz - pos)) * r[h][:, HEAD:] for h in heads]
        k_in = [(k[h].astype(F32) * (jnp.exp(lg_ref[0, h] * (c_sz - 1.0 - pos)) * scale)).astype(BF) for h in heads]
        for h in heads:
            sf_ref[h] = jnp.exp(lg_ref[0, h] * span) * sf_ref[h] + _dot_tn(k_in[h], v[h])
        for h in heads:
            y = _group_norm(o[h], True) * ng_ref[:, hs[h]] * _silu(qg[:, BRANCH + h * HEAD:BRANCH + (h + 1) * HEAD])
            o_ref[0, rows, hs[h]] = y.astype(o_ref.dtype)

    _sweep(nblk, bwd_proj, bwd_chunk, fwd_proj, fwd_chunk, bwd_init, fwd_init, lookahead=True)


def _retention(x, w_kv, w_qg, log_rate, norm_g, cos2, sin2):
    b, t, d = x.shape
    nblk = t // TBLK
    nc = t // CHUNK
    nh = BRANCH // HEAD
    lg = -jnp.exp(log_rate.astype(F32))
    in_specs = [
        pl.BlockSpec(memory_space=pltpu.SMEM),
        _x_spec(nblk, d),
        pl.BlockSpec((TBLK, HEAD), lambda bi, ph, c: (_blk_index(ph, c, nblk), 0)),
        pl.BlockSpec((TBLK, HEAD), lambda bi, ph, c: (_blk_index(ph, c, nblk), 0)),
        _full_spec(w_kv.shape),
        _full_spec(w_qg.shape),
        _full_spec((1, BRANCH)),
    ]
    scratch = [
        pltpu.VMEM((nh, HEAD, HEAD), F32),
        pltpu.VMEM((nh, HEAD, HEAD), F32),
        pltpu.VMEM((nc, nh, HEAD, HEAD), BF),
        pltpu.VMEM((nh, CHUNK, CHUNK), F32),
        pltpu.VMEM((t, 2 * BRANCH), BF),
    ]
    return _mixer_call(_ret_body, b, t, (lg, x, cos2, sin2, w_kv, w_qg, norm_g.reshape(1, BRANCH)),
                       in_specs, scratch, "retention")


HGRN_LEVELS = int(math.log2(CHUNK))


def _hgrn_masks():
    c = CHUNK
    t = np.arange(c)
    mask = np.zeros((2, HGRN_LEVELS + 1, c, c), np.float32)
    sign = np.zeros((2, HGRN_LEVELS, c), np.float32)
    for l in range(HGRN_LEVELS):
        s = 1 << l
        hi = (t % (2 * s)) >= s
        is_hi = hi[:, None]
        same = (t[:, None] // (2 * s)) == (t[None, :] // (2 * s))
        mask[0, l] = same & is_hi & ~is_hi.T
        mask[1, l] = same & ~is_hi & is_hi.T
        lower = 0.0 if s == 1 else -1.0
        sign[0, l] = np.where(hi, 1.0, lower)
        sign[1, l] = np.where(hi, lower, 1.0)
    mask[0, HGRN_LEVELS] = np.eye(c)
    sign = np.broadcast_to((sign * math.log2(math.e))[..., None], sign.shape + (HEAD,))
    return mask, np.ascontiguousarray(sign)


def _split_delta(cum, log_f, l, reverse):
    c, w = cum.shape
    s = 1 << l
    if s == 1:
        return log_f
    edge = s if reverse else s - 1
    if 2 * s >= 8:
        blocks = cum.reshape(c // (2 * s), 2 * s, w)
        at_split = jnp.broadcast_to(blocks[:, edge:edge + 1, :], blocks.shape).reshape(c, w)
    else:
        tiles = cum.reshape(c // 8, 8, w)
        sub = _iota2(tiles.shape, 1)
        at_split = None
        for blk in range(8 // (2 * s)):
            r = blk * 2 * s + edge
            row = jnp.broadcast_to(tiles[:, r:r + 1, :], tiles.shape)
            at_split = row if at_split is None else jnp.where(sub >= blk * 2 * s, row, at_split)
        at_split = at_split.reshape(c, w)
    return cum - at_split


def _hgrn_gates(z, lb):
    s = _sigmoid(z)
    log_f = jnp.log(lb + (1.0 - lb) * s)
    k = (1.0 - lb) * (1.0 - s)
    return k, log_f


def _hgrn_body(x_ref, wzv_ref, wq_ref, lb_ref, ng_ref, mask_ref, sign_ref, o_ref,
               sf_ref, sb_ref, sbs_ref, v_ref, *, nblk):
    c_sz = CHUNK
    nh = BRANCH // HEAD
    nl = HGRN_LEVELS

    def bwd_init():
        sb_ref[...] = jnp.zeros_like(sb_ref)

    def fwd_init():
        sf_ref[...] = jnp.zeros_like(sf_ref)

    def x_chunk(r0):
        return x_ref[0, pl.ds(r0, c_sz), :]

    def bwd_chunk(r0, ci, xb):
        zv = _dot(xb, wzv_ref[...])
        kb, lfb = _hgrn_gates(zv[:, :BRANCH], lb_ref[1:2, :])
        _, suf = _prefix_suffix_rows(lfb)
        total = suf[0:1, :]
        k_in = (kb * jnp.exp(total - suf)).astype(BF)
        dec = jnp.exp(total)
        v = zv[:, BRANCH:].astype(BF)
        v_ref[_rows(ci), :] = v
        for h in range(nh):
            sl = slice(h * HEAD, (h + 1) * HEAD)
            sbs_ref[ci, h] = sb_ref[h].astype(BF)
            sb_ref[h] = dec[:, sl] * sb_ref[h] + _dot_tn(v[:, sl], k_in[:, sl])

    def fwd_chunk(r0, ci, xb):
        rows = pl.ds(r0, c_sz)
        proj = _dot(xb, wq_ref[...])
        q = proj[:, :BRANCH]
        kf, lff = _hgrn_gates(proj[:, BRANCH:2 * BRANCH], lb_ref[0:1, :])
        kb, lfb = _hgrn_gates(proj[:, 2 * BRANCH:3 * BRANCH], lb_ref[1:2, :])
        g = proj[:, 3 * BRANCH:]
        v = v_ref[_rows(ci), :]

        scores = [None] * nh
        ksum = (kf + kb).astype(BF)
        qb = q.astype(BF)
        for h in range(nh):
            sl = slice(h * HEAD, (h + 1) * HEAD)
            scores[h] = _dot_nt(qb[:, sl], ksum[:, sl]) * mask_ref[0, nl]
        pre, _ = _prefix_suffix_rows(lff)
        _, suf = _prefix_suffix_rows(lfb)
        for dr, (kd, lfd, cum) in enumerate(((kf, lff, pre), (kb, lfb, suf))):
            for l in range(nl):
                sgn = jnp.concatenate([sign_ref[dr, l]] * nh, axis=1)
                e = jnp.exp2(_split_delta(cum, lfd, l, dr == 1) * sgn)
                u = (jnp.where(sgn > 0.0, q, kd) * e).astype(BF)
                for h in range(nh):
                    sl = slice(h * HEAD, (h + 1) * HEAD)
                    scores[h] = scores[h] + _dot_nt(u[:, sl], u[:, sl]) * mask_ref[dr, l]

        q_f = (q * jnp.exp(pre)).astype(BF)
        q_b = (q * jnp.exp(suf)).astype(BF)
        total = pre[c_sz - 1:c_sz, :]
        k_in = (kf * jnp.exp(total - pre)).astype(BF)
        dec = jnp.exp(total)
        heads = range(nh)
        hs = [slice(h * HEAD, (h + 1) * HEAD) for h in heads]
        intra = [_dot(scores[h].astype(BF), v[:, hs[h]]) for h in heads]
        inter = [_dot_nt(jnp.concatenate([q_f[:, hs[h]], q_b[:, hs[h]]], axis=1),
                         jnp.concatenate([sf_ref[h].astype(BF), sbs_ref[ci, h]], axis=1)) for h in heads]
        for h in heads:
            sf_ref[h] = dec[:, hs[h]] * sf_ref[h] + _dot_tn(v[:, hs[h]], k_in[:, hs[h]])
        for h in heads:
            y = _group_norm(intra[h] + inter[h], False) * ng_ref[:, hs[h]] * _silu(g[:, hs[h]])
            o_ref[0, rows, hs[h]] = y.astype(o_ref.dtype)

    _sweep(nblk, x_chunk, bwd_chunk, x_chunk, fwd_chunk, bwd_init, fwd_init, lookahead=False)


def _hgrn2(x, w_zv, w_q, lb, norm_g):
    b, t, d = x.shape
    nblk = t // TBLK
    nc = t // CHUNK
    nh = BRANCH // HEAD
    mask, sign = _hgrn_masks()
    mask = jnp.asarray(mask, F32)
    sign = jnp.asarray(sign, F32)
    in_specs = [
        _x_spec(nblk, d),
        _full_spec(w_zv.shape),
        _full_spec(w_q.shape),
        _full_spec((2, BRANCH)),
        _full_spec((1, BRANCH)),
        _full_spec(mask.shape),
        _full_spec(sign.shape),
    ]
    scratch = [
        pltpu.VMEM((nh, HEAD, HEAD), F32),
        pltpu.VMEM((nh, HEAD, HEAD), F32),
        pltpu.VMEM((nc, nh, HEAD, HEAD), BF),
        pltpu.VMEM((t, BRANCH), BF),
    ]
    return _mixer_call(_hgrn_body, b, t, (x, w_zv, w_q, lb, norm_g.reshape(1, BRANCH), mask, sign),
                       in_specs, scratch, "hgrn2")


ML_HEADS = 4
ML_AUG = 2 * HEAD


def _cummax_rows(a, reverse):
    n = a.shape[0]
    row = _iota2(a.shape, 0)
    k = 1
    while k < n:
        if reverse:
            shifted = pltpu.roll(a, n - k, axis=0)
            a = jnp.where(row < n - k, jnp.maximum(a, shifted), a)
        else:
            shifted = pltpu.roll(a, k, axis=0)
            a = jnp.where(row >= k, jnp.maximum(a, shifted), a)
        k *= 2
    return a


def _ml_body(x_ref, wkv_ref, wq_ref, wgc_ref, wgr_ref, bc_ref, br_ref, ng_ref, o_ref,
             cf_ref, cb_ref, cbs_ref, m_ref, mbs_ref, kv_ref, *, nblk):
    c_sz = CHUNK
    nh = ML_HEADS

    def gates(xb):
        is_fwd = _iota2((c_sz, HEAD), 1) < nh
        gc = _dot(xb, wgc_ref[...]) + bc_ref[...]
        ipre = gc[:, :HEAD]
        pre, suf = _prefix_suffix_rows(_log_sigmoid(gc[:, HEAD:]))
        cum = jnp.where(is_fwd, pre, suf)
        a = ipre - cum
        run_max = jnp.where(is_fwd, _cummax_rows(a, False), _cummax_rows(a, True))
        return a, cum, run_max

    def edge(arr):
        return jnp.where(_iota2((1, HEAD), 1) < nh, arr[c_sz - 1:c_sz, :], arr[0:1, :])

    def ones_col():
        return (_iota2((c_sz, HEAD), 1) == 0).astype(BF)

    def update_state(state_ref, lane0, k_all, v_aug, wk, sc):
        for h in range(nh):
            l = lane0 + h
            kw = (k_all[:, h * HEAD:(h + 1) * HEAD] * wk[:, l:l + 1]).astype(BF)
            state_ref[h] = sc[:, l:l + 1] * state_ref[h] + _dot_tn(kw, v_aug[h])

    def bwd_init():
        cb_ref[...] = jnp.zeros_like(cb_ref)
        m_ref[...] = jnp.zeros_like(m_ref)

    def fwd_init():
        cf_ref[...] = jnp.zeros_like(cf_ref)
        m_ref[...] = jnp.zeros_like(m_ref)

    def x_chunk(r0):
        return x_ref[0, pl.ds(r0, c_sz), :]

    def bwd_chunk(r0, ci, xb):
        a, cum, run_max = gates(xb)
        m_prev = m_ref[...]
        mbs_ref[ci] = m_prev
        for h in range(nh):
            cbs_ref[ci, h] = cb_ref[h].astype(BF)
        mu_e = edge(jnp.maximum(m_prev, run_max))
        kv = _dot(xb, wkv_ref[...])
        kv_ref[_rows(ci), :] = kv.astype(BF)
        v_aug = [jnp.concatenate([kv[:, BRANCH + h * HEAD:BRANCH + (h + 1) * HEAD].astype(BF), ones_col()],
                                 axis=1) for h in range(nh)]
        update_state(cb_ref, nh, kv[:, :BRANCH], v_aug, jnp.exp(a - mu_e), jnp.exp(m_prev - mu_e))
        m_ref[...] = edge(cum) + mu_e

    def fwd_chunk(r0, ci, xb):
        rows = pl.ds(r0, c_sz)
        a, cum, run_max = gates(xb)
        lower, upper = _tri_masks(c_sz)
        m_prev = jnp.where(_iota2((1, HEAD), 1) < nh, m_ref[...], mbs_ref[ci])
        mu = jnp.maximum(m_prev, run_max)
        s_inter = jnp.exp(m_prev - mu)
        thr = jnp.exp(-(cum + mu))

        gr = _dot_nt(wgr_ref[...], xb) + br_ref[...]
        ipre_r = gr[:2 * nh]
        pre_r, suf_r = _prefix_suffix_lanes(_log_sigmoid(gr[2 * nh:]))
        a_r = ipre_r - jnp.where(_iota2((2 * nh, c_sz), 0) < nh, pre_r, suf_r)

        proj = _dot(xb, wq_ref[...])
        kvc = kv_ref[_rows(ci), :]
        v_aug = [jnp.concatenate([kvc[:, BRANCH + h * HEAD:BRANCH + (h + 1) * HEAD], ones_col()], axis=1)
                 for h in range(nh)]
        heads = range(nh)
        qs = [proj[:, h * HEAD:(h + 1) * HEAD].astype(BF) for h in heads]
        qk = [_dot_nt(qs[h], kvc[:, h * HEAD:(h + 1) * HEAD]) for h in heads]
        rs = [_dot(qs[h], jnp.concatenate([cf_ref[h].astype(BF), cbs_ref[ci, h]], axis=1)) for h in heads]
        pairs = [(dr, h) for h in heads for dr in range(2)]
        wgt = {}
        for dr, h in pairs:
            l = dr * nh + h
            wgt[dr, h] = jnp.where(lower if dr == 0 else upper, jnp.exp(a_r[l:l + 1, :] - mu[:, l:l + 1]), 0.0)
        pv = {p: _dot((qk[p[1]] * wgt[p]).astype(BF), v_aug[p[1]]) for p in pairs}
        outs = {}
        for dr, h in pairs:
            l = dr * nh + h
            tot = s_inter[:, l:l + 1] * rs[h][:, dr * ML_AUG:(dr + 1) * ML_AUG] + pv[dr, h]
            den = jnp.maximum(jnp.abs(tot[:, HEAD:HEAD + 1]), thr[:, l:l + 1])
            outs[dr, h] = tot[:, :HEAD] / den
        for h in heads:
            sl = slice(h * HEAD, (h + 1) * HEAD)
            o_gate = proj[:, BRANCH + h * HEAD:BRANCH + (h + 1) * HEAD]
            g = proj[:, 2 * BRANCH + h * HEAD:2 * BRANCH + (h + 1) * HEAD]
            y = _group_norm(_sigmoid(o_gate) * (outs[0, h] + outs[1, h]), True) * ng_ref[:, sl] * _silu(g)
            o_ref[0, rows, sl] = y.astype(o_ref.dtype)

        mu_e = edge(mu)
        update_state(cf_ref, 0, kvc[:, :BRANCH].astype(F32), v_aug, jnp.exp(a - mu_e), jnp.exp(m_prev - mu_e))
        m_ref[...] = edge(cum) + mu_e

    _sweep(nblk, x_chunk, bwd_chunk, x_chunk, fwd_chunk, bwd_init, fwd_init, lookahead=False)


def _mlstm(x, w_kv, w_q, w_gates, i_bias, f_bias, norm_g):
    b, t, d = x.shape
    nblk = t // TBLK
    nc = t // CHUNK
    nh = ML_HEADS
    pad = HEAD - 2 * nh
    w_i = w_gates[:, :2 * nh]
    w_f = w_gates[:, 2 * nh:]
    wgc = jnp.concatenate([jnp.pad(w_i, ((0, 0), (0, pad))), jnp.pad(w_f, ((0, 0), (0, pad)))], axis=1)
    wgr = w_gates.T
    bias = jnp.concatenate([i_bias.reshape(-1), f_bias.reshape(-1)]).astype(F32)
    bc = jnp.concatenate([jnp.pad(bias[:2 * nh], (0, pad)), jnp.pad(bias[2 * nh:], (0, pad))]).reshape(1, 2 * HEAD)
    br = bias.reshape(4 * nh, 1)
    in_specs = [
        _x_spec(nblk, d),
        _full_spec(w_kv.shape),
        _full_spec(w_q.shape),
        _full_spec(wgc.shape),
        _full_spec(wgr.shape),
        _full_spec(bc.shape),
        _full_spec(br.shape),
        _full_spec((1, BRANCH)),
    ]
    scratch = [
        pltpu.VMEM((nh, HEAD, ML_AUG), F32),
        pltpu.VMEM((nh, HEAD, ML_AUG), F32),
        pltpu.VMEM((nc, nh, HEAD, ML_AUG), BF),
        pltpu.VMEM((1, HEAD), F32),
        pltpu.VMEM((nc, 1, HEAD), F32),
        pltpu.VMEM((t, 2 * BRANCH), BF),
    ]
    return _mixer_call(_ml_body, b, t,
                       (x, w_kv, w_q, wgc.astype(BF), wgr.astype(BF), bc, br, norm_g.reshape(1, BRANCH)),
                       in_specs, scratch, "mlstm")


SSD_XBC = BRANCH + 2 * SSD_GROUPS * HEAD
SSD_GW = BRANCH // SSD_GROUPS


def _ssd_body(x_ref, xp_ref, xn_ref, wz_ref, wx_ref, wdc_ref, wdr_ref, dbc_ref, dbr_ref,
              ac_ref, ar_ref, cw_ref, cb_ref, dsk_ref, ng_ref, ex_ref, o_ref,
              xe_ref, sf_ref, sb_ref, sbs_ref, act_ref, *, nblk):
    c_sz = CHUNK
    nh = SSD_HEADS
    per = TBLK // CHUNK

    def dt_columns(xb):
        is_fwd = _iota2((c_sz, HEAD), 1) < nh
        dt = _softplus(_dot(xb, wdc_ref[...]) + dbc_ref[...])
        pre, suf = _prefix_suffix_rows(dt * ac_ref[...])
        return dt, jnp.where(is_fwd, pre, suf)

    def expand(a, dr):
        hi, mid, _ = _split3(a)
        e = ex_ref[:, dr * BRANCH:(dr + 1) * BRANCH]
        return _dot(hi, e) + _dot(mid, e)

    def update_state(state_ref, dr, xs, bm, dt, cum):
        cum_e = jnp.where(_iota2((1, HEAD), 1) < nh, cum[c_sz - 1:c_sz, :], cum[0:1, :])
        w_state = jnp.exp(cum_e - cum) * dt
        dec = jnp.exp(jnp.broadcast_to(cum_e, (8, HEAD)))
        xw = (xs * expand(w_state, dr)).astype(BF)
        d512 = expand(dec, dr)[0:1, :]
        for g in range(SSD_GROUPS):
            gs = slice(g * SSD_GW, (g + 1) * SSD_GW)
            state_ref[g] = d512[:, gs] * state_ref[g] + _dot_tn(bm[:, g * HEAD:(g + 1) * HEAD], xw[:, gs])

    def bwd_init():
        sb_ref[...] = jnp.zeros_like(sb_ref)

    def fwd_init():
        sf_ref[...] = jnp.zeros_like(sf_ref)

    def project_block():
        blk = nblk - 1 - pl.program_id(2)

        def body(j, carry):
            r0 = pl.multiple_of(j * c_sz, c_sz)
            xe_ref[pl.ds(HALO + r0, c_sz), :] = _dot(x_ref[0, pl.ds(r0, c_sz), :], wx_ref[...])
            return carry

        lax.fori_loop(0, per, body, 0)
        halo = jnp.concatenate([xp_ref[0, 0], xn_ref[0, 0]], axis=0)
        ph = _dot(halo, wx_ref[...])
        xe_ref[0:HALO, :] = ph[:HALO] * jnp.where(blk > 0, 1.0, 0.0)
        xe_ref[HALO + TBLK:, :] = ph[HALO:] * jnp.where(blk < nblk - 1, 1.0, 0.0)

    def x_chunk(r0):
        return x_ref[0, pl.ds(r0, c_sz), :]

    def bwd_chunk(r0, ci, xb):
        half = (SSD_CONV - 1) // 2
        win = xe_ref[pl.ds(r0, c_sz + 2 * HALO), :]
        conv = cb_ref[...]
        for k in range(SSD_CONV):
            conv = conv + cw_ref[k:k + 1, :] * win[HALO - half + k:HALO - half + k + c_sz, :]
        act = _silu(conv)
        act_ref[_rows(ci), :] = act.astype(BF)
        dt, cum = dt_columns(xb)
        for g in range(SSD_GROUPS):
            sbs_ref[ci, g] = sb_ref[g].astype(BF)
        update_state(sb_ref, 1, act[:, :BRANCH], act[:, BRANCH:BRANCH + SSD_GROUPS * HEAD].astype(BF), dt, cum)

    def fwd_chunk(r0, ci, xb):
        rows = pl.ds(r0, c_sz)
        act = act_ref[_rows(ci), :]
        xs = act[:, :BRANCH].astype(F32)
        bm = act[:, BRANCH:BRANCH + SSD_GROUPS * HEAD]
        cm = act[:, BRANCH + SSD_GROUPS * HEAD:]
        dt, cum = dt_columns(xb)
        dt_r = _softplus(_dot_nt(wdr_ref[...], xb) + dbr_ref[...])
        pre_r, suf_r = _prefix_suffix_lanes(dt_r * ar_ref[...])
        cum_r = jnp.where(_iota2((2 * nh, c_sz), 0) < nh, pre_r, suf_r)
        shifted_r = cum_r - jnp.log(dt_r)
        below = _iota2((c_sz, c_sz), 1) < _iota2((c_sz, c_sz), 0)
        diag = _iota2((c_sz, c_sz), 1) == _iota2((c_sz, c_sz), 0)
        z = _dot(xb, wz_ref[...])
        lane_half = _iota2((c_sz, HEAD), 1) < SSD_HEADDIM
        ecum = jnp.exp(cum)
        ecum_f = expand(ecum, 0)
        ecum_b = expand(ecum, 1)
        hpg = SSD_HEADS // SSD_GROUPS
        groups = range(SSD_GROUPS)
        heads = range(nh)
        cg = [cm[:, g * HEAD:(g + 1) * HEAD] for g in groups]
        gmat = [_dot_nt(cg[g], bm[:, g * HEAD:(g + 1) * HEAD]) for g in groups]
        from_f = [_dot(cg[g], sf_ref[g].astype(BF)) for g in groups]
        from_b = [_dot(cg[g], sbs_ref[ci, g]) for g in groups]
        m = [jnp.exp(jnp.where(below, cum[:, h:h + 1] - shifted_r[h:h + 1, :],
                               cum[:, nh + h:nh + h + 1] - shifted_r[nh + h:nh + h + 1, :])) +
             jnp.where(diag, dt_r[h:h + 1, :], 0.0) for h in heads]
        xh = [jnp.where(lane_half if h % 2 == 0 else jnp.logical_not(lane_half),
                        xs[:, (h // 2) * HEAD:(h // 2 + 1) * HEAD], 0.0).astype(BF) for h in heads]
        part = [_dot((gmat[h // hpg] * m[h]).astype(BF), xh[h]) for h in heads]
        ys = []
        for g in groups:
            gs = slice(g * SSD_GW, (g + 1) * SSD_GW)
            pieces = [part[g * hpg + 2 * pair] + part[g * hpg + 2 * pair + 1] for pair in range(hpg // 2)]
            ys.append(jnp.concatenate(pieces, axis=1) + ecum_f[:, gs] * from_f[g] + ecum_b[:, gs] * from_b[g])
        y = jnp.concatenate(ys, axis=1) + dsk_ref[...] * xs
        y = y * _silu(z)
        for g in range(SSD_GROUPS):
            gs = slice(g * SSD_GW, (g + 1) * SSD_GW)
            o_ref[0, rows, gs] = (_group_norm(y[:, gs], False) * ng_ref[:, gs]).astype(o_ref.dtype)
        update_state(sf_ref, 0, xs, bm, dt, cum)

    pl.when(pl.program_id(1) == 0)(project_block)
    _sweep(nblk, x_chunk, bwd_chunk, x_chunk, fwd_chunk, bwd_init, fwd_init, lookahead=False)


def _ssd(x, w_z, w_xbc, w_dt, conv_w, conv_b, a_log, dt_bias, d_skip, norm_g):
    b, t, d = x.shape
    nblk = t // TBLK
    nc = t // CHUNK
    nh = SSD_HEADS
    pad = HEAD - 2 * nh
    wdc = jnp.pad(w_dt, ((0, 0), (0, pad)))
    wdr = w_dt.T
    db = dt_bias.reshape(-1).astype(F32)
    a = (-jnp.exp(a_log.astype(F32))).reshape(-1)
    dbc = jnp.pad(db, (0, pad)).reshape(1, HEAD)
    ac = jnp.pad(a, (0, pad)).reshape(1, HEAD)
    dbr = db.reshape(2 * nh, 1)
    ar = a.reshape(2 * nh, 1)
    cw = jnp.pad(conv_w.astype(F32), ((0, 8 - SSD_CONV), (0, 0)))
    dsk = jnp.repeat(d_skip.astype(F32), SSD_HEADDIM).reshape(1, BRANCH)
    ex = np.zeros((HEAD, 2 * BRANCH), np.float32)
    for l in range(2 * nh):
        ex[l, l * SSD_HEADDIM:(l + 1) * SSD_HEADDIM] = 1.0
    ex = jnp.asarray(ex, BF)
    xh = x.reshape(b, t // HALO, HALO, d)
    per = TBLK // HALO
    last = t // HALO - 1
    in_specs = [
        _x_spec(nblk, d),
        pl.BlockSpec((1, 1, HALO, d),
                     lambda bi, ph, c: (bi, jnp.maximum(_blk_index(ph, c, nblk) * per - 1, 0), 0, 0)),
        pl.BlockSpec((1, 1, HALO, d),
                     lambda bi, ph, c: (bi, jnp.minimum((_blk_index(ph, c, nblk) + 1) * per, last), 0, 0)),
        _full_spec(w_z.shape),
        _full_spec(w_xbc.shape),
        _full_spec(wdc.shape),
        _full_spec(wdr.shape),
        _full_spec(dbc.shape),
        _full_spec(dbr.shape),
        _full_spec(ac.shape),
        _full_spec(ar.shape),
        _full_spec(cw.shape),
        _full_spec((1, SSD_XBC)),
        _full_spec((1, BRANCH)),
        _full_spec((1, BRANCH)),
        _full_spec(ex.shape),
    ]
    scratch = [
        pltpu.VMEM((TBLK + 2 * HALO, SSD_XBC), F32),
        pltpu.VMEM((SSD_GROUPS, HEAD, SSD_GW), F32),
        pltpu.VMEM((SSD_GROUPS, HEAD, SSD_GW), F32),
        pltpu.VMEM((nc, SSD_GROUPS, HEAD, SSD_GW), BF),
        pltpu.VMEM((t, SSD_XBC), BF),
    ]
    return _mixer_call(_ssd_body, b, t,
                       (x, xh, xh, w_z, w_xbc, wdc.astype(BF), wdr.astype(BF), dbc, dbr, ac, ar, cw,
                        conv_b.astype(F32).reshape(1, SSD_XBC), dsk, norm_g.reshape(1, BRANCH), ex),
                       in_specs, scratch, "ssd")


def _out_body(x_ref, p_ref, m0_ref, m1_ref, m2_ref, m3_ref, wo_ref, lg_ref, lb_ref, wg_ref, wp_ref,
              o_ref, ob_ref, *, alpha):
    y = None
    for i, m_ref in enumerate((m0_ref, m1_ref, m2_ref, m3_ref)):
        part = _dot(m_ref[0], wo_ref[i * BRANCH:(i + 1) * BRANCH, :])
        y = part if y is None else y + part
    r = alpha * x_ref[0] + y
    mu = jnp.mean(r, axis=-1, keepdims=True)
    rc = r - mu
    var = jnp.mean(rc * rc, axis=-1, keepdims=True)
    xn = rc * lax.rsqrt(var + NORM_EPS) * lg_ref[...] + lb_ref[...]
    gate = _sigmoid(_dot(xn.astype(BF), wg_ref[...]))
    out = xn + gate * _dot(p_ref[0].astype(BF), wp_ref[...])
    o_ref[0] = out
    ob_ref[0] = out.astype(BF)


def _out_layer(x, p, mixed, w_out, ln_g, ln_b, w_gate, w_proj, alpha):
    b, t, d = x.shape
    dp = p.shape[-1]
    rows = OUT_ROWS
    row_spec = lambda width: pl.BlockSpec((1, rows, width), lambda bi, r: (bi, r, 0))
    full = lambda shape: pl.BlockSpec(shape, lambda bi, r: (0,) * len(shape))
    return pl.pallas_call(
        functools.partial(_out_body, alpha=alpha),
        grid=(b, t // rows),
        in_specs=[row_spec(d), row_spec(dp)] + [row_spec(BRANCH)] * 4 +
                 [full(w_out.shape), full((1, d)), full((1, d)), full(w_gate.shape), full(w_proj.shape)],
        out_specs=(row_spec(d), row_spec(d)),
        out_shape=(jax.ShapeDtypeStruct((b, t, d), F32), jax.ShapeDtypeStruct((b, t, d), BF)),
        compiler_params=pltpu.CompilerParams(
            dimension_semantics=("arbitrary", "arbitrary"), vmem_limit_bytes=VMEM_LIMIT),
        name="out_proj",
    )(x, p, *mixed, w_out, ln_g.reshape(1, d), ln_b.reshape(1, d), w_gate, w_proj)


def _rope_tables(t):
    inv = ROPE_BASE ** (-jnp.arange(0, HEAD, 2, dtype=F32) / HEAD)
    ang = jnp.arange(t, dtype=F32)[:, None] * inv[None, :]
    cos = jnp.cos(ang)
    sin = jnp.sin(ang)
    return jnp.concatenate([cos, cos], axis=1), jnp.concatenate([-sin, sin], axis=1)


def _layer_weights(l, w_in, w_out, hgrn_lb_logits, w_ple_gate, w_ple_proj):
    wl = w_in[l]
    nb = BRANCH
    o1 = 4 * nb
    o2 = o1 + 5 * nb
    o3 = o2 + 5 * nb + 4 * ML_HEADS

    def cols(*ranges):
        return jnp.concatenate([wl[:, a:b] for a, b in ranges], axis=1).astype(BF)

    ret_kv = cols((nb, 3 * nb))
    ret_qg = cols((0, nb), (3 * nb, 4 * nb))
    hg = o1
    hgrn_zv = cols((hg + 2 * nb, hg + 4 * nb))
    hgrn_q = cols((hg, hg + 3 * nb), (hg + 4 * nb, hg + 5 * nb))
    ml = o2
    ml_kv = jnp.concatenate([wl[:, ml + nb:ml + 2 * nb] * (HEAD ** -0.5), wl[:, ml + 2 * nb:ml + 3 * nb]],
                            axis=1).astype(BF)
    ml_q = cols((ml, ml + nb), (ml + 3 * nb, ml + 5 * nb))
    ml_gates = wl[:, ml + 5 * nb:o3]
    w_z = cols((o3, o3 + nb))
    w_xbc = cols((o3 + nb, o3 + nb + SSD_XBC))
    w_dt = wl[:, o3 + nb + SSD_XBC:]
    lb_w = jax.nn.softmax(hgrn_lb_logits.astype(F32), axis=0)
    lb = (jnp.cumsum(lb_w, axis=0) - lb_w[0:1])[l]
    return dict(ret_kv=ret_kv, ret_qg=ret_qg, hgrn_zv=hgrn_zv, hgrn_q=hgrn_q, ml_kv=ml_kv, ml_q=ml_q,
                ml_gates=ml_gates, w_z=w_z, w_xbc=w_xbc, w_dt=w_dt, lb=lb, w_out=w_out[l].astype(BF),
                w_gate=w_ple_gate[l].astype(BF), w_proj=w_ple_proj[l].astype(BF))


def kernel(x_prompt, x_sample, p_prompt, p_sample, w_in, w_out, ln_g, ln_b, ret_log_rate, ret_norm_g,
           hgrn_lb_logits, hgrn_norm_g, mlstm_i_bias, mlstm_f_bias, mlstm_norm_g, ssd_conv_w, ssd_conv_b,
           ssd_a_log, ssd_dt_bias, ssd_d, ssd_norm_g, w_ple_gate, w_ple_proj):
    depth = w_in.shape[0]
    alpha = (2 * depth) ** 0.25
    weights = [_layer_weights(l, w_in, w_out, hgrn_lb_logits, w_ple_gate, w_ple_proj) for l in range(depth)]

    def trunk(x, p):
        cos2, sin2 = _rope_tables(x.shape[1])
        xb = x.astype(BF)
        for l in range(depth):
            w = weights[l]
            mixed = (
                _retention(xb, w["ret_kv"], w["ret_qg"], ret_log_rate[l], ret_norm_g[l], cos2, sin2),
                _hgrn2(xb, w["hgrn_zv"], w["hgrn_q"], w["lb"], hgrn_norm_g[l]),
                _mlstm(xb, w["ml_kv"], w["ml_q"], w["ml_gates"], mlstm_i_bias[l], mlstm_f_bias[l],
                       mlstm_norm_g[l]),
                _ssd(xb, w["w_z"], w["w_xbc"], w["w_dt"], ssd_conv_w[l], ssd_conv_b[l], ssd_a_log[l],
                     ssd_dt_bias[l], ssd_d[l], ssd_norm_g[l]),
            )
            x, xb = _out_layer(x, p[l], mixed, w["w_out"], ln_g[l], ln_b[l], w["w_gate"], w["w_proj"], alpha)
        return x

    return trunk(x_prompt, p_prompt), trunk(x_sample, p_sample)
```

```python
import functools
import math

import numpy as np
import jax
import jax.numpy as jnp
from jax import lax
from jax.experimental import pallas as pl
from jax.experimental.pallas import tpu as pltpu

BF = jnp.bfloat16
F32 = jnp.float32

CHUNK = 128
TBLK = 1024
SUB = 512
HEAD = 128
BRANCH = 512
SSD_HEADDIM = 64
SSD_HEADS = 8
SSD_GROUPS = 2
SSD_CONV = 5
HALO = 16
ROPE_BASE = 10000.0
NORM_EPS = 1e-5
OUT_ROWS = 512
VMEM_LIMIT = 56 * 1024 * 1024


def _dot(a, b):
    return jnp.dot(a, b, preferred_element_type=F32)


def _dot_nt(a, b):
    return lax.dot_general(a, b, (((1,), (1,)), ((), ())), preferred_element_type=F32)


def _dot_tn(a, b):
    return lax.dot_general(a, b, (((0,), (0,)), ((), ())), preferred_element_type=F32)


def _split3(x):
    hi = x.astype(BF)
    r = x - hi.astype(F32)
    mid = r.astype(BF)
    lo = (r - mid.astype(F32)).astype(BF)
    return hi, mid, lo


def _sel_dot(sel, x):
    hi, mid, lo = _split3(x)
    return _dot(sel, hi) + _dot(sel, mid) + _dot(sel, lo)


def _dot_sel(x, sel):
    hi, mid, lo = _split3(x)
    return _dot(hi, sel) + _dot(mid, sel) + _dot(lo, sel)


def _sigmoid(x):
    return 1.0 / (1.0 + jnp.exp(-x))


def _silu(x):
    return x * _sigmoid(x)


def _softplus(x):
    return jnp.maximum(x, 0.0) + jnp.log1p(jnp.exp(-jnp.abs(x)))


def _log_sigmoid(x):
    return jnp.minimum(x, 0.0) - jnp.log1p(jnp.exp(-jnp.abs(x)))


def _iota2(shape, axis):
    return lax.broadcasted_iota(jnp.int32, shape, axis)


def _tri_masks(c):
    row = _iota2((c, c), 0)
    col = _iota2((c, c), 1)
    return col <= row, col >= row


def _prefix_suffix_rows(x):
    pre = _sel_dot(_tri_masks(x.shape[0])[0].astype(BF), x)
    return pre, pre[x.shape[0] - 1:, :] - pre + x


def _prefix_suffix_lanes(x):
    n = x.shape[1]
    pre = _dot_sel(x, _tri_masks(n)[1].astype(BF))
    return pre, pre[:, n - 1:] - pre + x


def _group_norm(o, center):
    if center:
        o = o - jnp.mean(o, axis=-1, keepdims=True)
    return o * lax.rsqrt(jnp.mean(o * o, axis=-1, keepdims=True) + NORM_EPS)


def _blk_index(phase, c, nblk):
    return jnp.where(phase == 0, nblk - 1 - c, c)


def _sweep(nblk, bwd_proj, bwd_chunk, fwd_proj, fwd_chunk, bwd_init, fwd_init, lookahead):
    phase = pl.program_id(1)
    c = pl.program_id(2)
    per = SUB // CHUNK
    nsub = TBLK // SUB

    def walk(order, base, first_chunk, proj, chunk):
        if not lookahead:
            for j in order:
                chunk(base + j * CHUNK, first_chunk + j, proj(base + j * CHUNK))
            return
        ahead = proj(base + order[0] * CHUNK)
        for n, j in enumerate(order):
            cur = ahead
            if n + 1 < per:
                ahead = proj(base + order[n + 1] * CHUNK)
            chunk(base + j * CHUNK, first_chunk + j, cur)

    @pl.when(phase == 0)
    def _():
        pl.when(c == 0)(bwd_init)
        blk = nblk - 1 - c

        def body(i, carry):
            sub = nsub - 1 - i
            walk(list(reversed(range(per))), pl.multiple_of(sub * SUB, SUB), (blk * nsub + sub) * per,
                 bwd_proj, bwd_chunk)
            return carry

        lax.fori_loop(0, nsub, body, 0)

    @pl.when(phase == 1)
    def _():
        pl.when(c == 0)(fwd_init)

        def body(sub, carry):
            walk(list(range(per)), pl.multiple_of(sub * SUB, SUB), (c * nsub + sub) * per, fwd_proj, fwd_chunk)
            return carry

        lax.fori_loop(0, nsub, body, 0)


def _mixer_call(body, b, t, in_arrays, in_specs, scratch, name):
    nblk = t // TBLK
    return pl.pallas_call(
        functools.partial(body, nblk=nblk),
        grid=(b, 2, nblk),
        in_specs=in_specs,
        out_specs=pl.BlockSpec((1, TBLK, BRANCH),
                               lambda bi, ph, c: (bi, jnp.where(ph == 0, 0, c), 0)),
        out_shape=jax.ShapeDtypeStruct((b, t, BRANCH), BF),
        scratch_shapes=scratch,
        compiler_params=pltpu.CompilerParams(
            dimension_semantics=("arbitrary", "arbitrary", "arbitrary"),
            vmem_limit_bytes=VMEM_LIMIT),
        name=name,
    )(*in_arrays)


def _x_spec(nblk, d):
    return pl.BlockSpec((1, TBLK, d), lambda bi, ph, c: (bi, _blk_index(ph, c, nblk), 0))


def _full_spec(shape):
    zeros = (0,) * len(shape)
    return pl.BlockSpec(shape, lambda bi, ph, c: zeros)


def _rows(ci):
    return pl.ds(pl.multiple_of(ci * CHUNK, CHUNK), CHUNK)


def _rope(x, cos2, sin2):
    return x * cos2 + pltpu.roll(x, HEAD // 2, axis=1) * sin2


def _ret_body(lg_ref, x_ref, cos_ref, sin_ref, wkv_ref, wqg_ref, ng_ref, o_ref,
              sf_ref, sb_ref, sbs_ref, m_ref, kv_ref, *, nblk):
    c_sz = CHUNK
    nh = BRANCH // HEAD
    scale = HEAD ** -0.5

    def bwd_init():
        sb_ref[...] = jnp.zeros_like(sb_ref)

    def fwd_init():
        sf_ref[...] = jnp.zeros_like(sf_ref)
        lower, upper = _tri_masks(c_sz)
        dist = (_iota2((c_sz, c_sz), 0) - _iota2((c_sz, c_sz), 1)).astype(F32)
        for h in range(nh):
            fwd = jnp.where(lower, jnp.exp(lg_ref[0, h] * dist), 0.0)
            bwd = jnp.where(upper, jnp.exp(-lg_ref[1, h] * dist), 0.0)
            m_ref[h] = (fwd + bwd) * scale

    def bwd_proj(r0):
        return _dot(x_ref[0, pl.ds(r0, c_sz), :], wkv_ref[...])

    def fwd_proj(r0):
        return _dot(x_ref[0, pl.ds(r0, c_sz), :], wqg_ref[...])

    def bwd_chunk(r0, ci, kv):
        rows = pl.ds(r0, c_sz)
        pos = _iota2((c_sz, HEAD), 0).astype(F32)
        span = jnp.full((1, HEAD), float(c_sz), F32)
        cos2 = cos_ref[rows, :]
        sin2 = sin_ref[rows, :]
        heads = range(nh)
        k = [_rope(kv[:, h * HEAD:(h + 1) * HEAD], cos2, sin2) for h in heads]
        v = [kv[:, BRANCH + h * HEAD:BRANCH + (h + 1) * HEAD].astype(BF) for h in heads]
        for h in heads:
            kv_ref[_rows(ci), h * HEAD:(h + 1) * HEAD] = k[h].astype(BF)
            kv_ref[_rows(ci), BRANCH + h * HEAD:BRANCH + (h + 1) * HEAD] = v[h]
            sbs_ref[ci, h] = sb_ref[h].astype(BF)
        k_in = [(k[h] * (jnp.exp(lg_ref[1, h] * pos) * scale)).astype(BF) for h in heads]
        for h in heads:
            sb_ref[h] = jnp.exp(lg_ref[1, h] * span) * sb_ref[h] + _dot_tn(k_in[h], v[h])

    def fwd_chunk(r0, ci, qg):
        rows = pl.ds(r0, c_sz)
        pos = _iota2((c_sz, HEAD), 0).astype(F32)
        span = jnp.full((1, HEAD), float(c_sz), F32)
        cos2 = cos_ref[rows, :]
        sin2 = sin_ref[rows, :]
        heads = range(nh)
        hs = [slice(h * HEAD, (h + 1) * HEAD) for h in heads]
        q = [_rope(qg[:, hs[h]], cos2, sin2).astype(BF) for h in heads]
        k = [kv_ref[_rows(ci), hs[h]] for h in heads]
        v = [kv_ref[_rows(ci), BRANCH + h * HEAD:BRANCH + (h + 1) * HEAD] for h in heads]
        p = [(_dot_nt(q[h], k[h]) * m_ref[h]).astype(BF) for h in heads]
        r = [_dot(q[h], jnp.concatenate([sf_ref[h].astype(BF), sbs_ref[ci, h]], axis=1)) for h in heads]
        o = [_dot(p[h], v[h]) for h in heads]
        o = [o[h] + jnp.exp(lg_ref[0, h] * (pos + 1.0)) * r[h][:, :HEAD] +
             jnp.exp(lg_ref[1, h] * (c_sz - pos)) * r[h][:, HEAD:] for h in heads]
        k_in = [(k[h].astype(F32) * (jnp.exp(lg_ref[0, h] * (c_sz - 1.0 - pos)) * scale)).astype(BF) for h in heads]
        for h in heads:
            sf_ref[h] = jnp.exp(lg_ref[0, h] * span) * sf_ref[h] + _dot_tn(k_in[h], v[h])
        for h in heads:
            y = _group_norm(o[h], True) * ng_ref[:, hs[h]] * _silu(qg[:, BRANCH + h * HEAD:BRANCH + (h + 1) * HEAD])
            o_ref[0, rows, hs[h]] = y.astype(o_ref.dtype)

    _sweep(nblk, bwd_proj, bwd_chunk, fwd_proj, fwd_chunk, bwd_init, fwd_init, lookahead=True)


def _retention(x, w_kv, w_qg, log_rate, norm_g, cos2, sin2):
    b, t, d = x.shape
    nblk = t // TBLK
    nc = t // CHUNK
    nh = BRANCH // HEAD
    lg = -jnp.exp(log_rate.astype(F32))
    in_specs = [
        pl.BlockSpec(memory_space=pltpu.SMEM),
        _x_spec(nblk, d),
        pl.BlockSpec((TBLK, HEAD), lambda bi, ph, c: (_blk_index(ph, c, nblk), 0)),
        pl.BlockSpec((TBLK, HEAD), lambda bi, ph, c: (_blk_index(ph, c, nblk), 0)),
        _full_spec(w_kv.shape),
        _full_spec(w_qg.shape),
        _full_spec((1, BRANCH)),
    ]
    scratch = [
        pltpu.VMEM((nh, HEAD, HEAD), F32),
        pltpu.VMEM((nh, HEAD, HEAD), F32),
        pltpu.VMEM((nc, nh, HEAD, HEAD), BF),
        pltpu.VMEM((nh, CHUNK, CHUNK), F32),
        pltpu.VMEM((t, 2 * BRANCH), BF),
    ]
    return _mixer_call(_ret_body, b, t, (lg, x, cos2, sin2, w_kv, w_qg, norm_g.reshape(1, BRANCH)),
                       in_specs, scratch, "retention")


HGRN_LEVELS = int(math.log2(CHUNK))


def _hgrn_masks():
    c = CHUNK
    t = np.arange(c)
    mask = np.zeros((2, HGRN_LEVELS + 1, c, c), np.float32)
    sign = np.zeros((2, HGRN_LEVELS, c), np.float32)
    for l in range(HGRN_LEVELS):
        s = 1 << l
        hi = (t % (2 * s)) >= s
        is_hi = hi[:, None]
        same = (t[:, None] // (2 * s)) == (t[None, :] // (2 * s))
        mask[0, l] = same & is_hi & ~is_hi.T
        mask[1, l] = same & ~is_hi & is_hi.T
        lower = 0.0 if s == 1 else -1.0
        sign[0, l] = np.where(hi, 1.0, lower)
        sign[1, l] = np.where(hi, lower, 1.0)
    mask[0, HGRN_LEVELS] = np.eye(c)
    sign = np.broadcast_to((sign * math.log2(math.e))[..., None], sign.shape + (HEAD,))
    return mask, np.ascontiguousarray(sign)


def _split_delta(cum, log_f, l, reverse):
    c, w = cum.shape
    s = 1 << l
    if s == 1:
        return log_f
    edge = s if reverse else s - 1
    if 2 * s >= 8:
        blocks = cum.reshape(c // (2 * s), 2 * s, w)
        at_split = jnp.broadcast_to(blocks[:, edge:edge + 1, :], blocks.shape).reshape(c, w)
    else:
        tiles = cum.reshape(c // 8, 8, w)
        sub = _iota2(tiles.shape, 1)
        at_split = None
        for blk in range(8 // (2 * s)):
            r = blk * 2 * s + edge
            row = jnp.broadcast_to(tiles[:, r:r + 1, :], tiles.shape)
            at_split = row if at_split is None else jnp.where(sub >= blk * 2 * s, row, at_split)
        at_split = at_split.reshape(c, w)
    return cum - at_split


def _hgrn_gates(z, lb):
    s = _sigmoid(z)
    log_f = jnp.log(lb + (1.0 - lb) * s)
    k = (1.0 - lb) * (1.0 - s)
    return k, log_f


def _hgrn_body(x_ref, wzv_ref, wq_ref, lb_ref, ng_ref, mask_ref, sign_ref, o_ref,
               sf_ref, sb_ref, sbs_ref, v_ref, *, nblk):
    c_sz = CHUNK
    nh = BRANCH // HEAD
    nl = HGRN_LEVELS

    def bwd_init():
        sb_ref[...] = jnp.zeros_like(sb_ref)

    def fwd_init():
        sf_ref[...] = jnp.zeros_like(sf_ref)

    def x_chunk(r0):
        return x_ref[0, pl.ds(r0, c_sz), :]

    def bwd_chunk(r0, ci, xb):
        zv = _dot(xb, wzv_ref[...])
        kb, lfb = _hgrn_gates(zv[:, :BRANCH], lb_ref[1:2, :])
        _, suf = _prefix_suffix_rows(lfb)
        total = suf[0:1, :]
        k_in = (kb * jnp.exp(total - suf)).astype(BF)
        dec = jnp.exp(total)
        v = zv[:, BRANCH:].astype(BF)
        v_ref[_rows(ci), :] = v
        for h in range(nh):
            sl = slice(h * HEAD, (h + 1) * HEAD)
            sbs_ref[ci, h] = sb_ref[h].astype(BF)
            sb_ref[h] = dec[:, sl] * sb_ref[h] + _dot_tn(v[:, sl], k_in[:, sl])

    def fwd_chunk(r0, ci, xb):
        rows = pl.ds(r0, c_sz)
        proj = _dot(xb, wq_ref[...])
        q = proj[:, :BRANCH]
        kf, lff = _hgrn_gates(proj[:, BRANCH:2 * BRANCH], lb_ref[0:1, :])
        kb, lfb = _hgrn_gates(proj[:, 2 * BRANCH:3 * BRANCH], lb_ref[1:2, :])
        g = proj[:, 3 * BRANCH:]
        v = v_ref[_rows(ci), :]

        scores = [None] * nh
        ksum = (kf + kb).astype(BF)
        qb = q.astype(BF)
        for h in range(nh):
            sl = slice(h * HEAD, (h + 1) * HEAD)
            scores[h] = _dot_nt(qb[:, sl], ksum[:, sl]) * mask_ref[0, nl]
        pre, _ = _prefix_suffix_rows(lff)
        _, suf = _prefix_suffix_rows(lfb)
        for dr, (kd, lfd, cum) in enumerate(((kf, lff, pre), (kb, lfb, suf))):
            for l in range(nl):
                sgn = jnp.concatenate([sign_ref[dr, l]] * nh, axis=1)
                e = jnp.exp2(_split_delta(cum, lfd, l, dr == 1) * sgn)
                u = (jnp.where(sgn > 0.0, q, kd) * e).astype(BF)
                for h in range(nh):
                    sl = slice(h * HEAD, (h + 1) * HEAD)
                    scores[h] = scores[h] + _dot_nt(u[:, sl], u[:, sl]) * mask_ref[dr, l]

        q_f = (q * jnp.exp(pre)).astype(BF)
        q_b = (q * jnp.exp(suf)).astype(BF)
        total = pre[c_sz - 1:c_sz, :]
        k_in = (kf * jnp.exp(total - pre)).astype(BF)
        dec = jnp.exp(total)
        for h in range(nh):
            sl = slice(h * HEAD, (h + 1) * HEAD)
            o = _dot(scores[h].astype(BF), v[:, sl])
            q_cat = jnp.concatenate([q_f[:, sl], q_b[:, sl]], axis=1)
            s_cat = jnp.concatenate([sf_ref[h].astype(BF), sbs_ref[ci, h]], axis=1)
            o = o + _dot_nt(q_cat, s_cat)
            sf_ref[h] = dec[:, sl] * sf_ref[h] + _dot_tn(v[:, sl], k_in[:, sl])
            y = _group_norm(o, False) * ng_ref[:, sl] * _silu(g[:, sl])
            o_ref[0, rows, sl] = y.astype(o_ref.dtype)

    _sweep(nblk, x_chunk, bwd_chunk, x_chunk, fwd_chunk, bwd_init, fwd_init, lookahead=False)


def _hgrn2(x, w_zv, w_q, lb, norm_g):
    b, t, d = x.shape
    nblk = t // TBLK
    nc = t // CHUNK
    nh = BRANCH // HEAD
    mask, sign = _hgrn_masks()
    mask = jnp.asarray(mask, F32)
    sign = jnp.asarray(sign, F32)
    in_specs = [
        _x_spec(nblk, d),
        _full_spec(w_zv.shape),
        _full_spec(w_q.shape),
        _full_spec((2, BRANCH)),
        _full_spec((1, BRANCH)),
        _full_spec(mask.shape),
        _full_spec(sign.shape),
    ]
    scratch = [
        pltpu.VMEM((nh, HEAD, HEAD), F32),
        pltpu.VMEM((nh, HEAD, HEAD), F32),
        pltpu.VMEM((nc, nh, HEAD, HEAD), BF),
        pltpu.VMEM((t, BRANCH), BF),
    ]
    return _mixer_call(_hgrn_body, b, t, (x, w_zv, w_q, lb, norm_g.reshape(1, BRANCH), mask, sign),
                       in_specs, scratch, "hgrn2")


ML_HEADS = 4
ML_AUG = 2 * HEAD


def _cummax_rows(a, reverse):
    n = a.shape[0]
    row = _iota2(a.shape, 0)
    k = 1
    while k < n:
        if reverse:
            shifted = pltpu.roll(a, n - k, axis=0)
            a = jnp.where(row < n - k, jnp.maximum(a, shifted), a)
        else:
            shifted = pltpu.roll(a, k, axis=0)
            a = jnp.where(row >= k, jnp.maximum(a, shifted), a)
        k *= 2
    return a


def _ml_body(x_ref, wkv_ref, wq_ref, wgc_ref, wgr_ref, bc_ref, br_ref, ng_ref, o_ref,
             cf_ref, cb_ref, cbs_ref, m_ref, mbs_ref, kv_ref, *, nblk):
    c_sz = CHUNK
    nh = ML_HEADS

    def gates(xb):
        is_fwd = _iota2((c_sz, HEAD), 1) < nh
        gc = _dot(xb, wgc_ref[...]) + bc_ref[...]
        ipre = gc[:, :HEAD]
        pre, suf = _prefix_suffix_rows(_log_sigmoid(gc[:, HEAD:]))
        cum = jnp.where(is_fwd, pre, suf)
        a = ipre - cum
        run_max = jnp.where(is_fwd, _cummax_rows(a, False), _cummax_rows(a, True))
        return a, cum, run_max

    def edge(arr):
        return jnp.where(_iota2((1, HEAD), 1) < nh, arr[c_sz - 1:c_sz, :], arr[0:1, :])

    def ones_col():
        return (_iota2((c_sz, HEAD), 1) == 0).astype(BF)

    def update_state(state_ref, lane0, k_all, v_aug, wk, sc):
        for h in range(nh):
            l = lane0 + h
            kw = (k_all[:, h * HEAD:(h + 1) * HEAD] * wk[:, l:l + 1]).astype(BF)
            state_ref[h] = sc[:, l:l + 1] * state_ref[h] + _dot_tn(kw, v_aug[h])

    def bwd_init():
        cb_ref[...] = jnp.zeros_like(cb_ref)
        m_ref[...] = jnp.zeros_like(m_ref)

    def fwd_init():
        cf_ref[...] = jnp.zeros_like(cf_ref)
        m_ref[...] = jnp.zeros_like(m_ref)

    def x_chunk(r0):
        return x_ref[0, pl.ds(r0, c_sz), :]

    def bwd_chunk(r0, ci, xb):
        a, cum, run_max = gates(xb)
        m_prev = m_ref[...]
        mbs_ref[ci] = m_prev
        for h in range(nh):
            cbs_ref[ci, h] = cb_ref[h].astype(BF)
        mu_e = edge(jnp.maximum(m_prev, run_max))
        kv = _dot(xb, wkv_ref[...])
        kv_ref[_rows(ci), :] = kv.astype(BF)
        v_aug = [jnp.concatenate([kv[:, BRANCH + h * HEAD:BRANCH + (h + 1) * HEAD].astype(BF), ones_col()],
                                 axis=1) for h in range(nh)]
        update_state(cb_ref, nh, kv[:, :BRANCH], v_aug, jnp.exp(a - mu_e), jnp.exp(m_prev - mu_e))
        m_ref[...] = edge(cum) + mu_e

    def fwd_chunk(r0, ci, xb):
        rows = pl.ds(r0, c_sz)
        a, cum, run_max = gates(xb)
        lower, upper = _tri_masks(c_sz)
        m_prev = jnp.where(_iota2((1, HEAD), 1) < nh, m_ref[...], mbs_ref[ci])
        mu = jnp.maximum(m_prev, run_max)
        s_inter = jnp.exp(m_prev - mu)
        thr = jnp.exp(-(cum + mu))

        gr = _dot_nt(wgr_ref[...], xb) + br_ref[...]
        ipre_r = gr[:2 * nh]
        pre_r, suf_r = _prefix_suffix_lanes(_log_sigmoid(gr[2 * nh:]))
        a_r = ipre_r - jnp.where(_iota2((2 * nh, c_sz), 0) < nh, pre_r, suf_r)

        proj = _dot(xb, wq_ref[...])
        kvc = kv_ref[_rows(ci), :]
        v_aug = [jnp.concatenate([kvc[:, BRANCH + h * HEAD:BRANCH + (h + 1) * HEAD], ones_col()], axis=1)
                 for h in range(nh)]
        heads = range(nh)
        qs = [proj[:, h * HEAD:(h + 1) * HEAD].astype(BF) for h in heads]
        qk = [_dot_nt(qs[h], kvc[:, h * HEAD:(h + 1) * HEAD]) for h in heads]
        rs = [_dot(qs[h], jnp.concatenate([cf_ref[h].astype(BF), cbs_ref[ci, h]], axis=1)) for h in heads]
        pairs = [(dr, h) for h in heads for dr in range(2)]
        wgt = {}
        for dr, h in pairs:
            l = dr * nh + h
            wgt[dr, h] = jnp.where(lower if dr == 0 else upper, jnp.exp(a_r[l:l + 1, :] - mu[:, l:l + 1]), 0.0)
        pv = {p: _dot((qk[p[1]] * wgt[p]).astype(BF), v_aug[p[1]]) for p in pairs}
        outs = {}
        for dr, h in pairs:
            l = dr * nh + h
            tot = s_inter[:, l:l + 1] * rs[h][:, dr * ML_AUG:(dr + 1) * ML_AUG] + pv[dr, h]
            den = jnp.maximum(jnp.abs(tot[:, HEAD:HEAD + 1]), thr[:, l:l + 1])
            outs[dr, h] = tot[:, :HEAD] / den
        for h in heads:
            sl = slice(h * HEAD, (h + 1) * HEAD)
            o_gate = proj[:, BRANCH + h * HEAD:BRANCH + (h + 1) * HEAD]
            g = proj[:, 2 * BRANCH + h * HEAD:2 * BRANCH + (h + 1) * HEAD]
            y = _group_norm(_sigmoid(o_gate) * (outs[0, h] + outs[1, h]), True) * ng_ref[:, sl] * _silu(g)
            o_ref[0, rows, sl] = y.astype(o_ref.dtype)

        mu_e = edge(mu)
        update_state(cf_ref, 0, kvc[:, :BRANCH].astype(F32), v_aug, jnp.exp(a - mu_e), jnp.exp(m_prev - mu_e))
        m_ref[...] = edge(cum) + mu_e

    _sweep(nblk, x_chunk, bwd_chunk, x_chunk, fwd_chunk, bwd_init, fwd_init, lookahead=False)


def _mlstm(x, w_kv, w_q, w_gates, i_bias, f_bias, norm_g):
    b, t, d = x.shape
    nblk = t // TBLK
    nc = t // CHUNK
    nh = ML_HEADS
    pad = HEAD - 2 * nh
    w_i = w_gates[:, :2 * nh]
    w_f = w_gates[:, 2 * nh:]
    wgc = jnp.concatenate([jnp.pad(w_i, ((0, 0), (0, pad))), jnp.pad(w_f, ((0, 0), (0, pad)))], axis=1)
    wgr = w_gates.T
    bias = jnp.concatenate([i_bias.reshape(-1), f_bias.reshape(-1)]).astype(F32)
    bc = jnp.concatenate([jnp.pad(bias[:2 * nh], (0, pad)), jnp.pad(bias[2 * nh:], (0, pad))]).reshape(1, 2 * HEAD)
    br = bias.reshape(4 * nh, 1)
    in_specs = [
        _x_spec(nblk, d),
        _full_spec(w_kv.shape),
        _full_spec(w_q.shape),
        _full_spec(wgc.shape),
        _full_spec(wgr.shape),
        _full_spec(bc.shape),
        _full_spec(br.shape),
        _full_spec((1, BRANCH)),
    ]
    scratch = [
        pltpu.VMEM((nh, HEAD, ML_AUG), F32),
        pltpu.VMEM((nh, HEAD, ML_AUG), F32),
        pltpu.VMEM((nc, nh, HEAD, ML_AUG), BF),
        pltpu.VMEM((1, HEAD), F32),
        pltpu.VMEM((nc, 1, HEAD), F32),
        pltpu.VMEM((t, 2 * BRANCH), BF),
    ]
    return _mixer_call(_ml_body, b, t,
                       (x, w_kv, w_q, wgc.astype(BF), wgr.astype(BF), bc, br, norm_g.reshape(1, BRANCH)),
                       in_specs, scratch, "mlstm")


SSD_XBC = BRANCH + 2 * SSD_GROUPS * HEAD
SSD_GW = BRANCH // SSD_GROUPS


def _ssd_body(x_ref, xp_ref, xn_ref, wz_ref, wx_ref, wdc_ref, wdr_ref, dbc_ref, dbr_ref,
              ac_ref, ar_ref, cw_ref, cb_ref, dsk_ref, ng_ref, ex_ref, o_ref,
              xe_ref, sf_ref, sb_ref, sbs_ref, act_ref, *, nblk):
    c_sz = CHUNK
    nh = SSD_HEADS
    per = TBLK // CHUNK

    def dt_columns(xb):
        is_fwd = _iota2((c_sz, HEAD), 1) < nh
        dt = _softplus(_dot(xb, wdc_ref[...]) + dbc_ref[...])
        pre, suf = _prefix_suffix_rows(dt * ac_ref[...])
        return dt, jnp.where(is_fwd, pre, suf)

    def expand(a, dr):
        hi, mid, _ = _split3(a)
        e = ex_ref[:, dr * BRANCH:(dr + 1) * BRANCH]
        return _dot(hi, e) + _dot(mid, e)

    def update_state(state_ref, dr, xs, bm, dt, cum):
        cum_e = jnp.where(_iota2((1, HEAD), 1) < nh, cum[c_sz - 1:c_sz, :], cum[0:1, :])
        w_state = jnp.exp(cum_e - cum) * dt
        dec = jnp.exp(jnp.broadcast_to(cum_e, (8, HEAD)))
        xw = (xs * expand(w_state, dr)).astype(BF)
        d512 = expand(dec, dr)[0:1, :]
        for g in range(SSD_GROUPS):
            gs = slice(g * SSD_GW, (g + 1) * SSD_GW)
            state_ref[g] = d512[:, gs] * state_ref[g] + _dot_tn(bm[:, g * HEAD:(g + 1) * HEAD], xw[:, gs])

    def bwd_init():
        sb_ref[...] = jnp.zeros_like(sb_ref)

    def fwd_init():
        sf_ref[...] = jnp.zeros_like(sf_ref)

    def project_block():
        blk = nblk - 1 - pl.program_id(2)

        def body(j, carry):
            r0 = pl.multiple_of(j * c_sz, c_sz)
            xe_ref[pl.ds(HALO + r0, c_sz), :] = _dot(x_ref[0, pl.ds(r0, c_sz), :], wx_ref[...])
            return carry

        lax.fori_loop(0, per, body, 0)
        halo = jnp.concatenate([xp_ref[0, 0], xn_ref[0, 0]], axis=0)
        ph = _dot(halo, wx_ref[...])
        xe_ref[0:HALO, :] = ph[:HALO] * jnp.where(blk > 0, 1.0, 0.0)
        xe_ref[HALO + TBLK:, :] = ph[HALO:] * jnp.where(blk < nblk - 1, 1.0, 0.0)

    def x_chunk(r0):
        return x_ref[0, pl.ds(r0, c_sz), :]

    def bwd_chunk(r0, ci, xb):
        half = (SSD_CONV - 1) // 2
        win = xe_ref[pl.ds(r0, c_sz + 2 * HALO), :]
        conv = cb_ref[...]
        for k in range(SSD_CONV):
            conv = conv + cw_ref[k:k + 1, :] * win[HALO - half + k:HALO - half + k + c_sz, :]
        act = _silu(conv)
        act_ref[_rows(ci), :] = act.astype(BF)
        dt, cum = dt_columns(xb)
        for g in range(SSD_GROUPS):
            sbs_ref[ci, g] = sb_ref[g].astype(BF)
        update_state(sb_ref, 1, act[:, :BRANCH], act[:, BRANCH:BRANCH + SSD_GROUPS * HEAD].astype(BF), dt, cum)

    def fwd_chunk(r0, ci, xb):
        rows = pl.ds(r0, c_sz)
        act = act_ref[_rows(ci), :]
        xs = act[:, :BRANCH].astype(F32)
        bm = act[:, BRANCH:BRANCH + SSD_GROUPS * HEAD]
        cm = act[:, BRANCH + SSD_GROUPS * HEAD:]
        dt, cum = dt_columns(xb)
        dt_r = _softplus(_dot_nt(wdr_ref[...], xb) + dbr_ref[...])
        pre_r, suf_r = _prefix_suffix_lanes(dt_r * ar_ref[...])
        cum_r = jnp.where(_iota2((2 * nh, c_sz), 0) < nh, pre_r, suf_r)
        shifted_r = cum_r - jnp.log(dt_r)
        below = _iota2((c_sz, c_sz), 1) < _iota2((c_sz, c_sz), 0)
        diag = _iota2((c_sz, c_sz), 1) == _iota2((c_sz, c_sz), 0)
        z = _dot(xb, wz_ref[...])
        lane_half = _iota2((c_sz, HEAD), 1) < SSD_HEADDIM
        ecum = jnp.exp(cum)
        ecum_f = expand(ecum, 0)
        ecum_b = expand(ecum, 1)
        hpg = SSD_HEADS // SSD_GROUPS
        groups = range(SSD_GROUPS)
        heads = range(nh)
        cg = [cm[:, g * HEAD:(g + 1) * HEAD] for g in groups]
        gmat = [_dot_nt(cg[g], bm[:, g * HEAD:(g + 1) * HEAD]) for g in groups]
        from_f = [_dot(cg[g], sf_ref[g].astype(BF)) for g in groups]
        from_b = [_dot(cg[g], sbs_ref[ci, g]) for g in groups]
        m = [jnp.exp(jnp.where(below, cum[:, h:h + 1] - shifted_r[h:h + 1, :],
                               cum[:, nh + h:nh + h + 1] - shifted_r[nh + h:nh + h + 1, :])) +
             jnp.where(diag, dt_r[h:h + 1, :], 0.0) for h in heads]
        xh = [jnp.where(lane_half if h % 2 == 0 else jnp.logical_not(lane_half),
                        xs[:, (h // 2) * HEAD:(h // 2 + 1) * HEAD], 0.0).astype(BF) for h in heads]
        part = [_dot((gmat[h // hpg] * m[h]).astype(BF), xh[h]) for h in heads]
        ys = []
        for g in groups:
            gs = slice(g * SSD_GW, (g + 1) * SSD_GW)
            pieces = [part[g * hpg + 2 * pair] + part[g * hpg + 2 * pair + 1] for pair in range(hpg // 2)]
            ys.append(jnp.concatenate(pieces, axis=1) + ecum_f[:, gs] * from_f[g] + ecum_b[:, gs] * from_b[g])
        y = jnp.concatenate(ys, axis=1) + dsk_ref[...] * xs
        y = y * _silu(z)
        for g in range(SSD_GROUPS):
            gs = slice(g * SSD_GW, (g + 1) * SSD_GW)
            o_ref[0, rows, gs] = (_group_norm(y[:, gs], False) * ng_ref[:, gs]).astype(o_ref.dtype)
        update_state(sf_ref, 0, xs, bm, dt, cum)

    pl.when(pl.program_id(1) == 0)(project_block)
    _sweep(nblk, x_chunk, bwd_chunk, x_chunk, fwd_chunk, bwd_init, fwd_init, lookahead=False)


def _ssd(x, w_z, w_xbc, w_dt, conv_w, conv_b, a_log, dt_bias, d_skip, norm_g):
    b, t, d = x.shape
    nblk = t // TBLK
    nc = t // CHUNK
    nh = SSD_HEADS
    pad = HEAD - 2 * nh
    wdc = jnp.pad(w_dt, ((0, 0), (0, pad)))
    wdr = w_dt.T
    db = dt_bias.reshape(-1).astype(F32)
    a = (-jnp.exp(a_log.astype(F32))).reshape(-1)
    dbc = jnp.pad(db, (0, pad)).reshape(1, HEAD)
    ac = jnp.pad(a, (0, pad)).reshape(1, HEAD)
    dbr = db.reshape(2 * nh, 1)
    ar = a.reshape(2 * nh, 1)
    cw = jnp.pad(conv_w.astype(F32), ((0, 8 - SSD_CONV), (0, 0)))
    dsk = jnp.repeat(d_skip.astype(F32), SSD_HEADDIM).reshape(1, BRANCH)
    ex = np.zeros((HEAD, 2 * BRANCH), np.float32)
    for l in range(2 * nh):
        ex[l, l * SSD_HEADDIM:(l + 1) * SSD_HEADDIM] = 1.0
    ex = jnp.asarray(ex, BF)
    xh = x.reshape(b, t // HALO, HALO, d)
    per = TBLK // HALO
    last = t // HALO - 1
    in_specs = [
        _x_spec(nblk, d),
        pl.BlockSpec((1, 1, HALO, d),
                     lambda bi, ph, c: (bi, jnp.maximum(_blk_index(ph, c, nblk) * per - 1, 0), 0, 0)),
        pl.BlockSpec((1, 1, HALO, d),
                     lambda bi, ph, c: (bi, jnp.minimum((_blk_index(ph, c, nblk) + 1) * per, last), 0, 0)),
        _full_spec(w_z.shape),
        _full_spec(w_xbc.shape),
        _full_spec(wdc.shape),
        _full_spec(wdr.shape),
        _full_spec(dbc.shape),
        _full_spec(dbr.shape),
        _full_spec(ac.shape),
        _full_spec(ar.shape),
        _full_spec(cw.shape),
        _full_spec((1, SSD_XBC)),
        _full_spec((1, BRANCH)),
        _full_spec((1, BRANCH)),
        _full_spec(ex.shape),
    ]
    scratch = [
        pltpu.VMEM((TBLK + 2 * HALO, SSD_XBC), F32),
        pltpu.VMEM((SSD_GROUPS, HEAD, SSD_GW), F32),
        pltpu.VMEM((SSD_GROUPS, HEAD, SSD_GW), F32),
        pltpu.VMEM((nc, SSD_GROUPS, HEAD, SSD_GW), BF),
        pltpu.VMEM((t, SSD_XBC), BF),
    ]
    return _mixer_call(_ssd_body, b, t,
                       (x, xh, xh, w_z, w_xbc, wdc.astype(BF), wdr.astype(BF), dbc, dbr, ac, ar, cw,
                        conv_b.astype(F32).reshape(1, SSD_XBC), dsk, norm_g.reshape(1, BRANCH), ex),
                       in_specs, scratch, "ssd")


def _out_body(x_ref, p_ref, m0_ref, m1_ref, m2_ref, m3_ref, wo_ref, lg_ref, lb_ref, wg_ref, wp_ref,
              o_ref, ob_ref, *, alpha):
    y = None
    for i, m_ref in enumerate((m0_ref, m1_ref, m2_ref, m3_ref)):
        part = _dot(m_ref[0], wo_ref[i * BRANCH:(i + 1) * BRANCH, :])
        y = part if y is None else y + part
    r = alpha * x_ref[0] + y
    mu = jnp.mean(r, axis=-1, keepdims=True)
    rc = r - mu
    var = jnp.mean(rc * rc, axis=-1, keepdims=True)
    xn = rc * lax.rsqrt(var + NORM_EPS) * lg_ref[...] + lb_ref[...]
    gate = _sigmoid(_dot(xn.astype(BF), wg_ref[...]))
    out = xn + gate * _dot(p_ref[0].astype(BF), wp_ref[...])
    o_ref[0] = out
    ob_ref[0] = out.astype(BF)


def _out_layer(x, p, mixed, w_out, ln_g, ln_b, w_gate, w_proj, alpha):
    b, t, d = x.shape
    dp = p.shape[-1]
    rows = OUT_ROWS
    row_spec = lambda width: pl.BlockSpec((1, rows, width), lambda bi, r: (bi, r, 0))
    full = lambda shape: pl.BlockSpec(shape, lambda bi, r: (0,) * len(shape))
    return pl.pallas_call(
        functools.partial(_out_body, alpha=alpha),
        grid=(b, t // rows),
        in_specs=[row_spec(d), row_spec(dp)] + [row_spec(BRANCH)] * 4 +
                 [full(w_out.shape), full((1, d)), full((1, d)), full(w_gate.shape), full(w_proj.shape)],
        out_specs=(row_spec(d), row_spec(d)),
        out_shape=(jax.ShapeDtypeStruct((b, t, d), F32), jax.ShapeDtypeStruct((b, t, d), BF)),
        compiler_params=pltpu.CompilerParams(
            dimension_semantics=("arbitrary", "arbitrary"), vmem_limit_bytes=VMEM_LIMIT),
        name="out_proj",
    )(x, p, *mixed, w_out, ln_g.reshape(1, d), ln_b.reshape(1, d), w_gate, w_proj)


def _rope_tables(t):
    inv = ROPE_BASE ** (-jnp.arange(0, HEAD, 2, dtype=F32) / HEAD)
    ang = jnp.arange(t, dtype=F32)[:, None] * inv[None, :]
    cos = jnp.cos(ang)
    sin = jnp.sin(ang)
    return jnp.concatenate([cos, cos], axis=1), jnp.concatenate([-sin, sin], axis=1)


def _layer_weights(l, w_in, w_out, hgrn_lb_logits, w_ple_gate, w_ple_proj):
    wl = w_in[l]
    nb = BRANCH
    o1 = 4 * nb
    o2 = o1 + 5 * nb
    o3 = o2 + 5 * nb + 4 * ML_HEADS

    def cols(*ranges):
        return jnp.concatenate([wl[:, a:b] for a, b in ranges], axis=1).astype(BF)

    ret_kv = cols((nb, 3 * nb))
    ret_qg = cols((0, nb), (3 * nb, 4 * nb))
    hg = o1
    hgrn_zv = cols((hg + 2 * nb, hg + 4 * nb))
    hgrn_q = cols((hg, hg + 3 * nb), (hg + 4 * nb, hg + 5 * nb))
    ml = o2
    ml_kv = jnp.concatenate([wl[:, ml + nb:ml + 2 * nb] * (HEAD ** -0.5), wl[:, ml + 2 * nb:ml + 3 * nb]],
                            axis=1).astype(BF)
    ml_q = cols((ml, ml + nb), (ml + 3 * nb, ml + 5 * nb))
    ml_gates = wl[:, ml + 5 * nb:o3]
    w_z = cols((o3, o3 + nb))
    w_xbc = cols((o3 + nb, o3 + nb + SSD_XBC))
    w_dt = wl[:, o3 + nb + SSD_XBC:]
    lb_w = jax.nn.softmax(hgrn_lb_logits.astype(F32), axis=0)
    lb = (jnp.cumsum(lb_w, axis=0) - lb_w[0:1])[l]
    return dict(ret_kv=ret_kv, ret_qg=ret_qg, hgrn_zv=hgrn_zv, hgrn_q=hgrn_q, ml_kv=ml_kv, ml_q=ml_q,
                ml_gates=ml_gates, w_z=w_z, w_xbc=w_xbc, w_dt=w_dt, lb=lb, w_out=w_out[l].astype(BF),
                w_gate=w_ple_gate[l].astype(BF), w_proj=w_ple_proj[l].astype(BF))


def kernel(x_prompt, x_sample, p_prompt, p_sample, w_in, w_out, ln_g, ln_b, ret_log_rate, ret_norm_g,
           hgrn_lb_logits, hgrn_norm_g, mlstm_i_bias, mlstm_f_bias, mlstm_norm_g, ssd_conv_w, ssd_conv_b,
           ssd_a_log, ssd_dt_bias, ssd_d, ssd_norm_g, w_ple_gate, w_ple_proj):
    depth = w_in.shape[0]
    alpha = (2 * depth) ** 0.25
    weights = [_layer_weights(l, w_in, w_out, hgrn_lb_logits, w_ple_gate, w_ple_proj) for l in range(depth)]

    def trunk(x, p):
        cos2, sin2 = _rope_tables(x.shape[1])
        xb = x.astype(BF)
        for l in range(depth):
            w = weights[l]
            mixed = (
                _retention(xb, w["ret_kv"], w["ret_qg"], ret_log_rate[l], ret_norm_g[l], cos2, sin2),
                _hgrn2(xb, w["hgrn_zv"], w["hgrn_q"], w["lb"], hgrn_norm_g[l]),
                _mlstm(xb, w["ml_kv"], w["ml_q"], w["ml_gates"], mlstm_i_bias[l], mlstm_f_bias[l],
                       mlstm_norm_g[l]),
                _ssd(xb, w["w_z"], w["w_xbc"], w["w_dt"], ssd_conv_w[l], ssd_conv_b[l], ssd_a_log[l],
                     ssd_dt_bias[l], ssd_d[l], ssd_norm_g[l]),
            )
            x, xb = _out_layer(x, p[l], mixed, w["w_out"], ln_g[l], ln_b[l], w["w_gate"], w["w_proj"], alpha)
        return x

    return trunk(x_prompt, p_prompt), trunk(x_sample, p_sample)
```

```python
import functools
import math

import numpy as np
import jax
import jax.numpy as jnp
from jax import lax
from jax.experimental import pallas as pl
from jax.experimental.pallas import tpu as pltpu

BF = jnp.bfloat16
F32 = jnp.float32

CHUNK = 128
TBLK = 1024
SUB = 512
HEAD = 128
BRANCH = 512
SSD_HEADDIM = 64
SSD_HEADS = 8
SSD_GROUPS = 2
SSD_CONV = 5
HALO = 16
ROPE_BASE = 10000.0
NORM_EPS = 1e-5
OUT_ROWS = 512
OUT_PARTS = 4
VMEM_LIMIT = 56 * 1024 * 1024


def _dot(a, b):
    return jnp.dot(a, b, preferred_element_type=F32)


def _dot_nt(a, b):
    return lax.dot_general(a, b, (((1,), (1,)), ((), ())), preferred_element_type=F32)


def _dot_tn(a, b):
    return lax.dot_general(a, b, (((0,), (0,)), ((), ())), preferred_element_type=F32)


def _split3(x):
    hi = x.astype(BF)
    r = x - hi.astype(F32)
    mid = r.astype(BF)
    lo = (r - mid.astype(F32)).astype(BF)
    return hi, mid, lo


def _sel_dot(sel, x):
    hi, mid, lo = _split3(x)
    return _dot(sel, hi) + _dot(sel, mid) + _dot(sel, lo)


def _dot_sel(x, sel):
    hi, mid, lo = _split3(x)
    return _dot(hi, sel) + _dot(mid, sel) + _dot(lo, sel)


def _sigmoid(x):
    return 1.0 / (1.0 + jnp.exp(-x))


def _silu(x):
    return x * _sigmoid(x)


def _softplus(x):
    return jnp.maximum(x, 0.0) + jnp.log1p(jnp.exp(-jnp.abs(x)))


def _log_sigmoid(x):
    return jnp.minimum(x, 0.0) - jnp.log1p(jnp.exp(-jnp.abs(x)))


def _iota2(shape, axis):
    return lax.broadcasted_iota(jnp.int32, shape, axis)


def _tri_masks(c):
    row = _iota2((c, c), 0)
    col = _iota2((c, c), 1)
    return col <= row, col >= row


def _prefix_suffix_rows(x):
    pre = _sel_dot(_tri_masks(x.shape[0])[0].astype(BF), x)
    return pre, pre[x.shape[0] - 1:, :] - pre + x


def _prefix_suffix_lanes(x):
    n = x.shape[1]
    pre = _dot_sel(x, _tri_masks(n)[1].astype(BF))
    return pre, pre[:, n - 1:] - pre + x


def _group_norm(o, center):
    if center:
        o = o - jnp.mean(o, axis=-1, keepdims=True)
    return o * lax.rsqrt(jnp.mean(o * o, axis=-1, keepdims=True) + NORM_EPS)


def _blk_index(phase, c, nblk):
    return jnp.where(phase == 0, nblk - 1 - c, c)


def _sweep(nblk, bwd_proj, bwd_chunk, fwd_proj, fwd_chunk, bwd_init, fwd_init, lookahead):
    phase = pl.program_id(1)
    c = pl.program_id(2)
    per = SUB // CHUNK
    nsub = TBLK // SUB

    def walk(order, base, first_chunk, proj, chunk):
        if not lookahead:
            for j in order:
                chunk(base + j * CHUNK, first_chunk + j, proj(base + j * CHUNK))
            return
        ahead = proj(base + order[0] * CHUNK)
        for n, j in enumerate(order):
            cur = ahead
            if n + 1 < per:
                ahead = proj(base + order[n + 1] * CHUNK)
            chunk(base + j * CHUNK, first_chunk + j, cur)

    @pl.when(phase == 0)
    def _():
        pl.when(c == 0)(bwd_init)
        blk = nblk - 1 - c

        def body(i, carry):
            sub = nsub - 1 - i
            walk(list(reversed(range(per))), pl.multiple_of(sub * SUB, SUB), (blk * nsub + sub) * per,
                 bwd_proj, bwd_chunk)
            return carry

        lax.fori_loop(0, nsub, body, 0)

    @pl.when(phase == 1)
    def _():
        pl.when(c == 0)(fwd_init)

        def body(sub, carry):
            walk(list(range(per)), pl.multiple_of(sub * SUB, SUB), (c * nsub + sub) * per, fwd_proj, fwd_chunk)
            return carry

        lax.fori_loop(0, nsub, body, 0)


def _mixer_call(body, b, t, in_arrays, in_specs, scratch, name):
    nblk = t // TBLK
    return pl.pallas_call(
        functools.partial(body, nblk=nblk),
        grid=(b, 2, nblk),
        in_specs=in_specs,
        out_specs=pl.BlockSpec((1, TBLK, BRANCH),
                               lambda bi, ph, c: (bi, jnp.where(ph == 0, 0, c), 0)),
        out_shape=jax.ShapeDtypeStruct((b, t, BRANCH), BF),
        scratch_shapes=scratch,
        compiler_params=pltpu.CompilerParams(
            dimension_semantics=("arbitrary", "arbitrary", "arbitrary"),
            vmem_limit_bytes=VMEM_LIMIT),
        name=name,
    )(*in_arrays)


def _x_spec(nblk, d):
    return pl.BlockSpec((1, TBLK, d), lambda bi, ph, c: (bi, _blk_index(ph, c, nblk), 0))


def _full_spec(shape):
    zeros = (0,) * len(shape)
    return pl.BlockSpec(shape, lambda bi, ph, c: zeros)


def _rows(ci):
    return pl.ds(pl.multiple_of(ci * CHUNK, CHUNK), CHUNK)


def _rope(x, cos2, sin2):
    return x * cos2 + pltpu.roll(x, HEAD // 2, axis=1) * sin2


def _ret_body(lg_ref, x_ref, cos_ref, sin_ref, wkv_ref, wqg_ref, ng_ref, o_ref,
              sf_ref, sb_ref, sbs_ref, m_ref, kv_ref, *, nblk):
    c_sz = CHUNK
    nh = BRANCH // HEAD
    scale = HEAD ** -0.5

    def bwd_init():
        sb_ref[...] = jnp.zeros_like(sb_ref)

    def fwd_init():
        sf_ref[...] = jnp.zeros_like(sf_ref)
        lower, upper = _tri_masks(c_sz)
        dist = (_iota2((c_sz, c_sz), 0) - _iota2((c_sz, c_sz), 1)).astype(F32)
        for h in range(nh):
            fwd = jnp.where(lower, jnp.exp(lg_ref[0, h] * dist), 0.0)
            bwd = jnp.where(upper, jnp.exp(-lg_ref[1, h] * dist), 0.0)
            m_ref[h] = (fwd + bwd) * scale

    def bwd_proj(r0):
        return _dot(x_ref[0, pl.ds(r0, c_sz), :], wkv_ref[...])

    def fwd_proj(r0):
        return _dot(x_ref[0, pl.ds(r0, c_sz), :], wqg_ref[...])

    def bwd_chunk(r0, ci, kv):
        rows = pl.ds(r0, c_sz)
        pos = _iota2((c_sz, HEAD), 0).astype(F32)
        span = jnp.full((1, HEAD), float(c_sz), F32)
        cos2 = cos_ref[rows, :]
        sin2 = sin_ref[rows, :]
        heads = range(nh)
        k = [_rope(kv[:, h * HEAD:(h + 1) * HEAD], cos2, sin2) for h in heads]
        v = [kv[:, BRANCH + h * HEAD:BRANCH + (h + 1) * HEAD].astype(BF) for h in heads]
        for h in heads:
            kv_ref[_rows(ci), h * HEAD:(h + 1) * HEAD] = k[h].astype(BF)
            kv_ref[_rows(ci), BRANCH + h * HEAD:BRANCH + (h + 1) * HEAD] = v[h]
            sbs_ref[ci, h] = sb_ref[h].astype(BF)
        k_in = [(k[h] * (jnp.exp(lg_ref[1, h] * pos) * scale)).astype(BF) for h in heads]
        for h in heads:
            sb_ref[h] = jnp.exp(lg_ref[1, h] * span) * sb_ref[h] + _dot_tn(k_in[h], v[h])

    def fwd_chunk(r0, ci, qg):
        rows = pl.ds(r0, c_sz)
        pos = _iota2((c_sz, HEAD), 0).astype(F32)
        span = jnp.full((1, HEAD), float(c_sz), F32)
        cos2 = cos_ref[rows, :]
        sin2 = sin_ref[rows, :]
        heads = range(nh)
        hs = [slice(h * HEAD, (h + 1) * HEAD) for h in heads]
        q = [_rope(qg[:, hs[h]], cos2, sin2).astype(BF) for h in heads]
        k = [kv_ref[_rows(ci), hs[h]] for h in heads]
        v = [kv_ref[_rows(ci), BRANCH + h * HEAD:BRANCH + (h + 1) * HEAD] for h in heads]
        p = [(_dot_nt(q[h], k[h]) * m_ref[h]).astype(BF) for h in heads]
        r = [_dot(q[h], jnp.concatenate([sf_ref[h].astype(BF), sbs_ref[ci, h]], axis=1)) for h in heads]
        o = [_dot(p[h], v[h]) for h in heads]
        o = [o[h] + jnp.exp(lg_ref[0, h] * (pos + 1.0)) * r[h][:, :HEAD] +
             jnp.exp(lg_ref[1, h] * (c_sz - pos)) * r[h][:, HEAD:] for h in heads]
        k_in = [(k[h].astype(F32) * (jnp.exp(lg_ref[0, h] * (c_sz - 1.0 - pos)) * scale)).astype(BF) for h in heads]
        for h in heads:
            sf_ref[h] = jnp.exp(lg_ref[0, h] * span) * sf_ref[h] + _dot_tn(k_in[h], v[h])
        for h in heads:
            y = _group_norm(o[h], True) * ng_ref[:, hs[h]] * _silu(qg[:, BRANCH + h * HEAD:BRANCH + (h + 1) * HEAD])
            o_ref[0, rows, hs[h]] = y.astype(o_ref.dtype)

    _sweep(nblk, bwd_proj, bwd_chunk, fwd_proj, fwd_chunk, bwd_init, fwd_init, lookahead=True)


def _retention(x, w_kv, w_qg, log_rate, norm_g, cos2, sin2):
    b, t, d = x.shape
    nblk = t // TBLK
    nc = t // CHUNK
    nh = BRANCH // HEAD
    lg = -jnp.exp(log_rate.astype(F32))
    in_specs = [
        pl.BlockSpec(memory_space=pltpu.SMEM),
        _x_spec(nblk, d),
        pl.BlockSpec((TBLK, HEAD), lambda bi, ph, c: (_blk_index(ph, c, nblk), 0)),
        pl.BlockSpec((TBLK, HEAD), lambda bi, ph, c: (_blk_index(ph, c, nblk), 0)),
        _full_spec(w_kv.shape),
        _full_spec(w_qg.shape),
        _full_spec((1, BRANCH)),
    ]
    scratch = [
        pltpu.VMEM((nh, HEAD, HEAD), F32),
        pltpu.VMEM((nh, HEAD, HEAD), F32),
        pltpu.VMEM((nc, nh, HEAD, HEAD), BF),
        pltpu.VMEM((nh, CHUNK, CHUNK), F32),
        pltpu.VMEM((t, 2 * BRANCH), BF),
    ]
    return _mixer_call(_ret_body, b, t, (lg, x, cos2, sin2, w_kv, w_qg, norm_g.reshape(1, BRANCH)),
                       in_specs, scratch, "retention")


HGRN_LEVELS = int(math.log2(CHUNK))


def _hgrn_masks():
    c = CHUNK
    t = np.arange(c)
    mask = np.zeros((2, HGRN_LEVELS + 1, c, c), np.float32)
    sign = np.zeros((2, HGRN_LEVELS, c), np.float32)
    for l in range(HGRN_LEVELS):
        s = 1 << l
        hi = (t % (2 * s)) >= s
        is_hi = hi[:, None]
        same = (t[:, None] // (2 * s)) == (t[None, :] // (2 * s))
        mask[0, l] = same & is_hi & ~is_hi.T
        mask[1, l] = same & ~is_hi & is_hi.T
        lower = 0.0 if s == 1 else -1.0
        sign[0, l] = np.where(hi, 1.0, lower)
        sign[1, l] = np.where(hi, lower, 1.0)
    mask[0, HGRN_LEVELS] = np.eye(c)
    sign = np.broadcast_to((sign * math.log2(math.e))[..., None], sign.shape + (HEAD,))
    return mask, np.ascontiguousarray(sign)


def _split_delta(cum, log_f, l, reverse):
    c, w = cum.shape
    s = 1 << l
    if s == 1:
        return log_f
    edge = s if reverse else s - 1
    if 2 * s >= 8:
        blocks = cum.reshape(c // (2 * s), 2 * s, w)
        at_split = jnp.broadcast_to(blocks[:, edge:edge + 1, :], blocks.shape).reshape(c, w)
    else:
        tiles = cum.reshape(c // 8, 8, w)
        sub = _iota2(tiles.shape, 1)
        at_split = None
        for blk in range(8 // (2 * s)):
            r = blk * 2 * s + edge
            row = jnp.broadcast_to(tiles[:, r:r + 1, :], tiles.shape)
            at_split = row if at_split is None else jnp.where(sub >= blk * 2 * s, row, at_split)
        at_split = at_split.reshape(c, w)
    return cum - at_split


def _hgrn_gates(z, lb):
    s = _sigmoid(z)
    log_f = jnp.log(lb + (1.0 - lb) * s)
    k = (1.0 - lb) * (1.0 - s)
    return k, log_f


def _hgrn_body(x_ref, wzv_ref, wq_ref, lb_ref, ng_ref, mask_ref, sign_ref, o_ref,
               sf_ref, sb_ref, sbs_ref, v_ref, *, nblk):
    c_sz = CHUNK
    nh = BRANCH // HEAD
    nl = HGRN_LEVELS

    def bwd_init():
        sb_ref[...] = jnp.zeros_like(sb_ref)

    def fwd_init():
        sf_ref[...] = jnp.zeros_like(sf_ref)

    def x_chunk(r0):
        return x_ref[0, pl.ds(r0, c_sz), :]

    def bwd_chunk(r0, ci, xb):
        zv = _dot(xb, wzv_ref[...])
        kb, lfb = _hgrn_gates(zv[:, :BRANCH], lb_ref[1:2, :])
        _, suf = _prefix_suffix_rows(lfb)
        total = suf[0:1, :]
        k_in = (kb * jnp.exp(total - suf)).astype(BF)
        dec = jnp.exp(total)
        v = zv[:, BRANCH:].astype(BF)
        v_ref[_rows(ci), :] = v
        for h in range(nh):
            sl = slice(h * HEAD, (h + 1) * HEAD)
            sbs_ref[ci, h] = sb_ref[h].astype(BF)
            sb_ref[h] = dec[:, sl] * sb_ref[h] + _dot_tn(v[:, sl], k_in[:, sl])

    def fwd_chunk(r0, ci, xb):
        rows = pl.ds(r0, c_sz)
        proj = _dot(xb, wq_ref[...])
        q = proj[:, :BRANCH]
        kf, lff = _hgrn_gates(proj[:, BRANCH:2 * BRANCH], lb_ref[0:1, :])
        kb, lfb = _hgrn_gates(proj[:, 2 * BRANCH:3 * BRANCH], lb_ref[1:2, :])
        g = proj[:, 3 * BRANCH:]
        v = v_ref[_rows(ci), :]

        scores = [None] * nh
        ksum = (kf + kb).astype(BF)
        qb = q.astype(BF)
        for h in range(nh):
            sl = slice(h * HEAD, (h + 1) * HEAD)
            scores[h] = _dot_nt(qb[:, sl], ksum[:, sl]).astype(BF) * mask_ref[0, nl]
        pre, _ = _prefix_suffix_rows(lff)
        _, suf = _prefix_suffix_rows(lfb)
        for dr, (kd, lfd, cum) in enumerate(((kf, lff, pre), (kb, lfb, suf))):
            for l in range(nl):
                sgn = jnp.concatenate([sign_ref[dr, l]] * nh, axis=1)
                e = jnp.exp2(_split_delta(cum, lfd, l, dr == 1) * sgn)
                u = (jnp.where(sgn > 0.0, q, kd) * e).astype(BF)
                for h in range(nh):
                    sl = slice(h * HEAD, (h + 1) * HEAD)
                    scores[h] = scores[h] + _dot_nt(u[:, sl], u[:, sl]).astype(BF) * mask_ref[dr, l]

        q_f = (q * jnp.exp(pre)).astype(BF)
        q_b = (q * jnp.exp(suf)).astype(BF)
        total = pre[c_sz - 1:c_sz, :]
        k_in = (kf * jnp.exp(total - pre)).astype(BF)
        dec = jnp.exp(total)
        for h in range(nh):
            sl = slice(h * HEAD, (h + 1) * HEAD)
            o = _dot(scores[h], v[:, sl])
            q_cat = jnp.concatenate([q_f[:, sl], q_b[:, sl]], axis=1)
            s_cat = jnp.concatenate([sf_ref[h].astype(BF), sbs_ref[ci, h]], axis=1)
            o = o + _dot_nt(q_cat, s_cat)
            sf_ref[h] = dec[:, sl] * sf_ref[h] + _dot_tn(v[:, sl], k_in[:, sl])
            y = _group_norm(o, False) * ng_ref[:, sl] * _silu(g[:, sl])
            o_ref[0, rows, sl] = y.astype(o_ref.dtype)

    _sweep(nblk, x_chunk, bwd_chunk, x_chunk, fwd_chunk, bwd_init, fwd_init, lookahead=False)


def _hgrn2(x, w_zv, w_q, lb, norm_g):
    b, t, d = x.shape
    nblk = t // TBLK
    nc = t // CHUNK
    nh = BRANCH // HEAD
    mask, sign = _hgrn_masks()
    mask = jnp.asarray(mask, BF)
    sign = jnp.asarray(sign, F32)
    in_specs = [
        _x_spec(nblk, d),
        _full_spec(w_zv.shape),
        _full_spec(w_q.shape),
        _full_spec((2, BRANCH)),
        _full_spec((1, BRANCH)),
        _full_spec(mask.shape),
        _full_spec(sign.shape),
    ]
    scratch = [
        pltpu.VMEM((nh, HEAD, HEAD), F32),
        pltpu.VMEM((nh, HEAD, HEAD), F32),
        pltpu.VMEM((nc, nh, HEAD, HEAD), BF),
        pltpu.VMEM((t, BRANCH), BF),
    ]
    return _mixer_call(_hgrn_body, b, t, (x, w_zv, w_q, lb, norm_g.reshape(1, BRANCH), mask, sign),
                       in_specs, scratch, "hgrn2")


ML_HEADS = 4
ML_AUG = 2 * HEAD


def _cummax_rows(a, reverse):
    n = a.shape[0]
    row = _iota2(a.shape, 0)
    k = 1
    while k < n:
        if reverse:
            shifted = pltpu.roll(a, n - k, axis=0)
            a = jnp.where(row < n - k, jnp.maximum(a, shifted), a)
        else:
            shifted = pltpu.roll(a, k, axis=0)
            a = jnp.where(row >= k, jnp.maximum(a, shifted), a)
        k *= 2
    return a


def _ml_body(x_ref, wkv_ref, wq_ref, wgc_ref, wgr_ref, bc_ref, br_ref, ng_ref, o_ref,
             cf_ref, cb_ref, cbs_ref, m_ref, mbs_ref, kv_ref, *, nblk):
    c_sz = CHUNK
    nh = ML_HEADS

    def gates(xb):
        is_fwd = _iota2((c_sz, HEAD), 1) < nh
        gc = _dot(xb, wgc_ref[...]) + bc_ref[...]
        ipre = gc[:, :HEAD]
        pre, suf = _prefix_suffix_rows(_log_sigmoid(gc[:, HEAD:]))
        cum = jnp.where(is_fwd, pre, suf)
        a = ipre - cum
        run_max = jnp.where(is_fwd, _cummax_rows(a, False), _cummax_rows(a, True))
        return a, cum, run_max

    def edge(arr):
        return jnp.where(_iota2((1, HEAD), 1) < nh, arr[c_sz - 1:c_sz, :], arr[0:1, :])

    def ones_col():
        return (_iota2((c_sz, HEAD), 1) == 0).astype(BF)

    def update_state(state_ref, lane0, k_all, v_aug, wk, sc):
        for h in range(nh):
            l = lane0 + h
            kw = (k_all[:, h * HEAD:(h + 1) * HEAD] * wk[:, l:l + 1]).astype(BF)
            state_ref[h] = sc[:, l:l + 1] * state_ref[h] + _dot_tn(kw, v_aug[h])

    def bwd_init():
        cb_ref[...] = jnp.zeros_like(cb_ref)
        m_ref[...] = jnp.zeros_like(m_ref)

    def fwd_init():
        cf_ref[...] = jnp.zeros_like(cf_ref)
        m_ref[...] = jnp.zeros_like(m_ref)

    def x_chunk(r0):
        return x_ref[0, pl.ds(r0, c_sz), :]

    def bwd_chunk(r0, ci, xb):
        a, cum, run_max = gates(xb)
        m_prev = m_ref[...]
        mbs_ref[ci] = m_prev
        for h in range(nh):
            cbs_ref[ci, h] = cb_ref[h].astype(BF)
        mu_e = edge(jnp.maximum(m_prev, run_max))
        kv = _dot(xb, wkv_ref[...])
        kv_ref[_rows(ci), :] = kv.astype(BF)
        v_aug = [jnp.concatenate([kv[:, BRANCH + h * HEAD:BRANCH + (h + 1) * HEAD].astype(BF), ones_col()],
                                 axis=1) for h in range(nh)]
        update_state(cb_ref, nh, kv[:, :BRANCH], v_aug, jnp.exp(a - mu_e), jnp.exp(m_prev - mu_e))
        m_ref[...] = edge(cum) + mu_e

    def fwd_chunk(r0, ci, xb):
        rows = pl.ds(r0, c_sz)
        a, cum, run_max = gates(xb)
        lower, upper = _tri_masks(c_sz)
        m_prev = jnp.where(_iota2((1, HEAD), 1) < nh, m_ref[...], mbs_ref[ci])
        mu = jnp.maximum(m_prev, run_max)
        s_inter = jnp.exp(m_prev - mu)
        thr = jnp.exp(-(cum + mu))

        gr = _dot_nt(wgr_ref[...], xb) + br_ref[...]
        ipre_r = gr[:2 * nh]
        pre_r, suf_r = _prefix_suffix_lanes(_log_sigmoid(gr[2 * nh:]))
        a_r = ipre_r - jnp.where(_iota2((2 * nh, c_sz), 0) < nh, pre_r, suf_r)

        proj = _dot(xb, wq_ref[...])
        kvc = kv_ref[_rows(ci), :]
        v_aug = [jnp.concatenate([kvc[:, BRANCH + h * HEAD:BRANCH + (h + 1) * HEAD], ones_col()], axis=1)
                 for h in range(nh)]
        heads = range(nh)
        qs = [proj[:, h * HEAD:(h + 1) * HEAD].astype(BF) for h in heads]
        qk = [_dot_nt(qs[h], kvc[:, h * HEAD:(h + 1) * HEAD]) for h in heads]
        rs = [_dot(qs[h], jnp.concatenate([cf_ref[h].astype(BF), cbs_ref[ci, h]], axis=1)) for h in heads]
        pairs = [(dr, h) for h in heads for dr in range(2)]
        wgt = {}
        for dr, h in pairs:
            l = dr * nh + h
            wgt[dr, h] = jnp.where(lower if dr == 0 else upper, jnp.exp(a_r[l:l + 1, :] - mu[:, l:l + 1]), 0.0)
        pv = {p: _dot((qk[p[1]] * wgt[p]).astype(BF), v_aug[p[1]]) for p in pairs}
        outs = {}
        for dr, h in pairs:
            l = dr * nh + h
            tot = s_inter[:, l:l + 1] * rs[h][:, dr * ML_AUG:(dr + 1) * ML_AUG] + pv[dr, h]
            den = jnp.maximum(jnp.abs(tot[:, HEAD:HEAD + 1]), thr[:, l:l + 1])
            outs[dr, h] = tot[:, :HEAD] / den
        for h in heads:
            sl = slice(h * HEAD, (h + 1) * HEAD)
            o_gate = proj[:, BRANCH + h * HEAD:BRANCH + (h + 1) * HEAD]
            g = proj[:, 2 * BRANCH + h * HEAD:2 * BRANCH + (h + 1) * HEAD]
            y = _group_norm(_sigmoid(o_gate) * (outs[0, h] + outs[1, h]), True) * ng_ref[:, sl] * _silu(g)
            o_ref[0, rows, sl] = y.astype(o_ref.dtype)

        mu_e = edge(mu)
        update_state(cf_ref, 0, kvc[:, :BRANCH].astype(F32), v_aug, jnp.exp(a - mu_e), jnp.exp(m_prev - mu_e))
        m_ref[...] = edge(cum) + mu_e

    _sweep(nblk, x_chunk, bwd_chunk, x_chunk, fwd_chunk, bwd_init, fwd_init, lookahead=False)


def _mlstm(x, w_kv, w_q, w_gates, i_bias, f_bias, norm_g):
    b, t, d = x.shape
    nblk = t // TBLK
    nc = t // CHUNK
    nh = ML_HEADS
    pad = HEAD - 2 * nh
    w_i = w_gates[:, :2 * nh]
    w_f = w_gates[:, 2 * nh:]
    wgc = jnp.concatenate([jnp.pad(w_i, ((0, 0), (0, pad))), jnp.pad(w_f, ((0, 0), (0, pad)))], axis=1)
    wgr = w_gates.T
    bias = jnp.concatenate([i_bias.reshape(-1), f_bias.reshape(-1)]).astype(F32)
    bc = jnp.concatenate([jnp.pad(bias[:2 * nh], (0, pad)), jnp.pad(bias[2 * nh:], (0, pad))]).reshape(1, 2 * HEAD)
    br = bias.reshape(4 * nh, 1)
    in_specs = [
        _x_spec(nblk, d),
        _full_spec(w_kv.shape),
        _full_spec(w_q.shape),
        _full_spec(wgc.shape),
        _full_spec(wgr.shape),
        _full_spec(bc.shape),
        _full_spec(br.shape),
        _full_spec((1, BRANCH)),
    ]
    scratch = [
        pltpu.VMEM((nh, HEAD, ML_AUG), F32),
        pltpu.VMEM((nh, HEAD, ML_AUG), F32),
        pltpu.VMEM((nc, nh, HEAD, ML_AUG), BF),
        pltpu.VMEM((1, HEAD), F32),
        pltpu.VMEM((nc, 1, HEAD), F32),
        pltpu.VMEM((t, 2 * BRANCH), BF),
    ]
    return _mixer_call(_ml_body, b, t,
                       (x, w_kv, w_q, wgc.astype(BF), wgr.astype(BF), bc, br, norm_g.reshape(1, BRANCH)),
                       in_specs, scratch, "mlstm")


SSD_XBC = BRANCH + 2 * SSD_GROUPS * HEAD
SSD_GW = BRANCH // SSD_GROUPS


def _ssd_body(x_ref, xp_ref, xn_ref, wz_ref, wx_ref, wdc_ref, wdr_ref, dbc_ref, dbr_ref,
              ac_ref, ar_ref, cw_ref, cb_ref, dsk_ref, ng_ref, ex_ref, o_ref,
              xe_ref, sf_ref, sb_ref, sbs_ref, act_ref, *, nblk):
    c_sz = CHUNK
    nh = SSD_HEADS
    per = TBLK // CHUNK

    def dt_columns(xb):
        is_fwd = _iota2((c_sz, HEAD), 1) < nh
        dt = _softplus(_dot(xb, wdc_ref[...]) + dbc_ref[...])
        pre, suf = _prefix_suffix_rows(dt * ac_ref[...])
        return dt, jnp.where(is_fwd, pre, suf)

    def expand(a, dr):
        hi, mid, _ = _split3(a)
        e = ex_ref[:, dr * BRANCH:(dr + 1) * BRANCH]
        return _dot(hi, e) + _dot(mid, e)

    def update_state(state_ref, dr, xs, bm, dt, cum):
        cum_e = jnp.where(_iota2((1, HEAD), 1) < nh, cum[c_sz - 1:c_sz, :], cum[0:1, :])
        w_state = jnp.exp(cum_e - cum) * dt
        dec = jnp.exp(jnp.broadcast_to(cum_e, (8, HEAD)))
        xw = (xs * expand(w_state, dr)).astype(BF)
        d512 = expand(dec, dr)[0:1, :]
        for g in range(SSD_GROUPS):
            gs = slice(g * SSD_GW, (g + 1) * SSD_GW)
            state_ref[g] = d512[:, gs] * state_ref[g] + _dot_tn(bm[:, g * HEAD:(g + 1) * HEAD], xw[:, gs])

    def bwd_init():
        sb_ref[...] = jnp.zeros_like(sb_ref)

    def fwd_init():
        sf_ref[...] = jnp.zeros_like(sf_ref)

    def project_block():
        blk = nblk - 1 - pl.program_id(2)

        def body(j, carry):
            r0 = pl.multiple_of(j * c_sz, c_sz)
            xe_ref[pl.ds(HALO + r0, c_sz), :] = _dot(x_ref[0, pl.ds(r0, c_sz), :], wx_ref[...])
            return carry

        lax.fori_loop(0, per, body, 0)
        halo = jnp.concatenate([xp_ref[0, 0], xn_ref[0, 0]], axis=0)
        ph = _dot(halo, wx_ref[...])
        xe_ref[0:HALO, :] = ph[:HALO] * jnp.where(blk > 0, 1.0, 0.0)
        xe_ref[HALO + TBLK:, :] = ph[HALO:] * jnp.where(blk < nblk - 1, 1.0, 0.0)

    def x_chunk(r0):
        return x_ref[0, pl.ds(r0, c_sz), :]

    def bwd_chunk(r0, ci, xb):
        half = (SSD_CONV - 1) // 2
        win = xe_ref[pl.ds(r0, c_sz + 2 * HALO), :]
        conv = cb_ref[...]
        for k in range(SSD_CONV):
            conv = conv + cw_ref[k:k + 1, :] * win[HALO - half + k:HALO - half + k + c_sz, :]
        act = _silu(conv)
        act_ref[_rows(ci), :] = act.astype(BF)
        dt, cum = dt_columns(xb)
        for g in range(SSD_GROUPS):
            sbs_ref[ci, g] = sb_ref[g].astype(BF)
        update_state(sb_ref, 1, act[:, :BRANCH], act[:, BRANCH:BRANCH + SSD_GROUPS * HEAD].astype(BF), dt, cum)

    def fwd_chunk(r0, ci, xb):
        rows = pl.ds(r0, c_sz)
        act = act_ref[_rows(ci), :]
        xs = act[:, :BRANCH].astype(F32)
        bm = act[:, BRANCH:BRANCH + SSD_GROUPS * HEAD]
        cm = act[:, BRANCH + SSD_GROUPS * HEAD:]
        dt, cum = dt_columns(xb)
        dt_r = _softplus(_dot_nt(wdr_ref[...], xb) + dbr_ref[...])
        pre_r, suf_r = _prefix_suffix_lanes(dt_r * ar_ref[...])
        cum_r = jnp.where(_iota2((2 * nh, c_sz), 0) < nh, pre_r, suf_r)
        shifted_r = cum_r - jnp.log(dt_r)
        below = _iota2((c_sz, c_sz), 1) < _iota2((c_sz, c_sz), 0)
        diag = _iota2((c_sz, c_sz), 1) == _iota2((c_sz, c_sz), 0)
        z = _dot(xb, wz_ref[...])
        lane_half = _iota2((c_sz, HEAD), 1) < SSD_HEADDIM
        ecum = jnp.exp(cum)
        ecum_f = expand(ecum, 0)
        ecum_b = expand(ecum, 1)
        hpg = SSD_HEADS // SSD_GROUPS
        groups = range(SSD_GROUPS)
        heads = range(nh)
        cg = [cm[:, g * HEAD:(g + 1) * HEAD] for g in groups]
        gmat = [_dot_nt(cg[g], bm[:, g * HEAD:(g + 1) * HEAD]) for g in groups]
        from_f = [_dot(cg[g], sf_ref[g].astype(BF)) for g in groups]
        from_b = [_dot(cg[g], sbs_ref[ci, g]) for g in groups]
        m = [jnp.exp(jnp.where(below, cum[:, h:h + 1] - shifted_r[h:h + 1, :],
                               cum[:, nh + h:nh + h + 1] - shifted_r[nh + h:nh + h + 1, :])) +
             jnp.where(diag, dt_r[h:h + 1, :], 0.0) for h in heads]
        xh = [jnp.where(lane_half if h % 2 == 0 else jnp.logical_not(lane_half),
                        xs[:, (h // 2) * HEAD:(h // 2 + 1) * HEAD], 0.0).astype(BF) for h in heads]
        part = [_dot((gmat[h // hpg] * m[h]).astype(BF), xh[h]) for h in heads]
        ys = []
        for g in groups:
            gs = slice(g * SSD_GW, (g + 1) * SSD_GW)
            pieces = [part[g * hpg + 2 * pair] + part[g * hpg + 2 * pair + 1] for pair in range(hpg // 2)]
            ys.append(jnp.concatenate(pieces, axis=1) + ecum_f[:, gs] * from_f[g] + ecum_b[:, gs] * from_b[g])
        y = jnp.concatenate(ys, axis=1) + dsk_ref[...] * xs
        y = y * _silu(z)
        for g in range(SSD_GROUPS):
            gs = slice(g * SSD_GW, (g + 1) * SSD_GW)
            o_ref[0, rows, gs] = (_group_norm(y[:, gs], False) * ng_ref[:, gs]).astype(o_ref.dtype)
        update_state(sf_ref, 0, xs, bm, dt, cum)

    pl.when(pl.program_id(1) == 0)(project_block)
    _sweep(nblk, x_chunk, bwd_chunk, x_chunk, fwd_chunk, bwd_init, fwd_init, lookahead=False)


def _ssd(x, w_z, w_xbc, w_dt, conv_w, conv_b, a_log, dt_bias, d_skip, norm_g):
    b, t, d = x.shape
    nblk = t // TBLK
    nc = t // CHUNK
    nh = SSD_HEADS
    pad = HEAD - 2 * nh
    wdc = jnp.pad(w_dt, ((0, 0), (0, pad)))
    wdr = w_dt.T
    db = dt_bias.reshape(-1).astype(F32)
    a = (-jnp.exp(a_log.astype(F32))).reshape(-1)
    dbc = jnp.pad(db, (0, pad)).reshape(1, HEAD)
    ac = jnp.pad(a, (0, pad)).reshape(1, HEAD)
    dbr = db.reshape(2 * nh, 1)
    ar = a.reshape(2 * nh, 1)
    cw = jnp.pad(conv_w.astype(F32), ((0, 8 - SSD_CONV), (0, 0)))
    dsk = jnp.repeat(d_skip.astype(F32), SSD_HEADDIM).reshape(1, BRANCH)
    ex = np.zeros((HEAD, 2 * BRANCH), np.float32)
    for l in range(2 * nh):
        ex[l, l * SSD_HEADDIM:(l + 1) * SSD_HEADDIM] = 1.0
    ex = jnp.asarray(ex, BF)
    xh = x.reshape(b, t // HALO, HALO, d)
    per = TBLK // HALO
    last = t // HALO - 1
    in_specs = [
        _x_spec(nblk, d),
        pl.BlockSpec((1, 1, HALO, d),
                     lambda bi, ph, c: (bi, jnp.maximum(_blk_index(ph, c, nblk) * per - 1, 0), 0, 0)),
        pl.BlockSpec((1, 1, HALO, d),
                     lambda bi, ph, c: (bi, jnp.minimum((_blk_index(ph, c, nblk) + 1) * per, last), 0, 0)),
        _full_spec(w_z.shape),
        _full_spec(w_xbc.shape),
        _full_spec(wdc.shape),
        _full_spec(wdr.shape),
        _full_spec(dbc.shape),
        _full_spec(dbr.shape),
        _full_spec(ac.shape),
        _full_spec(ar.shape),
        _full_spec(cw.shape),
        _full_spec((1, SSD_XBC)),
        _full_spec((1, BRANCH)),
        _full_spec((1, BRANCH)),
        _full_spec(ex.shape),
    ]
    scratch = [
        pltpu.VMEM((TBLK + 2 * HALO, SSD_XBC), F32),
        pltpu.VMEM((SSD_GROUPS, HEAD, SSD_GW), F32),
        pltpu.VMEM((SSD_GROUPS, HEAD, SSD_GW), F32),
        pltpu.VMEM((nc, SSD_GROUPS, HEAD, SSD_GW), BF),
        pltpu.VMEM((t, SSD_XBC), BF),
    ]
    return _mixer_call(_ssd_body, b, t,
                       (x, xh, xh, w_z, w_xbc, wdc.astype(BF), wdr.astype(BF), dbc, dbr, ac, ar, cw,
                        conv_b.astype(F32).reshape(1, SSD_XBC), dsk, norm_g.reshape(1, BRANCH), ex),
                       in_specs, scratch, "ssd")


def _out_body(x_ref, p_ref, m0_ref, m1_ref, m2_ref, m3_ref, wo_ref, lg_ref, lb_ref, wg_ref, wp_ref,
              o_ref, ob_ref, *, alpha):
    step = OUT_ROWS // OUT_PARTS
    parts = [slice(i * step, (i + 1) * step) for i in range(OUT_PARTS)]

    def mix(rows):
        y = None
        for i, m_ref in enumerate((m0_ref, m1_ref, m2_ref, m3_ref)):
            part = _dot(m_ref[0, rows, :], wo_ref[i * BRANCH:(i + 1) * BRANCH, :])
            y = part if y is None else y + part
        return y

    def norm(rows, y):
        r = alpha * x_ref[0, rows, :] + y
        mu = jnp.mean(r, axis=-1, keepdims=True)
        rc = r - mu
        var = jnp.mean(rc * rc, axis=-1, keepdims=True)
        return rc * lax.rsqrt(var + NORM_EPS) * lg_ref[...] + lb_ref[...]

    def finish(rows, xn):
        gate = _sigmoid(_dot(xn.astype(BF), wg_ref[...]))
        out = xn + gate * _dot(p_ref[0, rows, :].astype(BF), wp_ref[...])
        o_ref[0, rows, :] = out
        ob_ref[0, rows, :] = out.astype(BF)

    y = mix(parts[0])
    xn = None
    for i in range(OUT_PARTS):
        y_next = mix(parts[i + 1]) if i + 1 < OUT_PARTS else None
        xn_i = norm(parts[i], y)
        if xn is not None:
            finish(parts[i - 1], xn)
        xn = xn_i
        y = y_next
    finish(parts[OUT_PARTS - 1], xn)


def _out_layer(x, p, mixed, w_out, ln_g, ln_b, w_gate, w_proj, alpha):
    b, t, d = x.shape
    dp = p.shape[-1]
    rows = OUT_ROWS
    row_spec = lambda width: pl.BlockSpec((1, rows, width), lambda bi, r: (bi, r, 0))
    full = lambda shape: pl.BlockSpec(shape, lambda bi, r: (0,) * len(shape))
    return pl.pallas_call(
        functools.partial(_out_body, alpha=alpha),
        grid=(b, t // rows),
        in_specs=[row_spec(d), row_spec(dp)] + [row_spec(BRANCH)] * 4 +
                 [full(w_out.shape), full((1, d)), full((1, d)), full(w_gate.shape), full(w_proj.shape)],
        out_specs=(row_spec(d), row_spec(d)),
        out_shape=(jax.ShapeDtypeStruct((b, t, d), F32), jax.ShapeDtypeStruct((b, t, d), BF)),
        compiler_params=pltpu.CompilerParams(
            dimension_semantics=("arbitrary", "arbitrary"), vmem_limit_bytes=VMEM_LIMIT),
        name="out_proj",
    )(x, p, *mixed, w_out, ln_g.reshape(1, d), ln_b.reshape(1, d), w_gate, w_proj)


def _rope_tables(t):
    inv = ROPE_BASE ** (-jnp.arange(0, HEAD, 2, dtype=F32) / HEAD)
    ang = jnp.arange(t, dtype=F32)[:, None] * inv[None, :]
    cos = jnp.cos(ang)
    sin = jnp.sin(ang)
    return jnp.concatenate([cos, cos], axis=1), jnp.concatenate([-sin, sin], axis=1)


def _layer_weights(l, w_in, w_out, hgrn_lb_logits, w_ple_gate, w_ple_proj):
    wl = w_in[l]
    nb = BRANCH
    o1 = 4 * nb
    o2 = o1 + 5 * nb
    o3 = o2 + 5 * nb + 4 * ML_HEADS

    def cols(*ranges):
        return jnp.concatenate([wl[:, a:b] for a, b in ranges], axis=1).astype(BF)

    ret_kv = cols((nb, 3 * nb))
    ret_qg = cols((0, nb), (3 * nb, 4 * nb))
    hg = o1
    hgrn_zv = cols((hg + 2 * nb, hg + 4 * nb))
    hgrn_q = cols((hg, hg + 3 * nb), (hg + 4 * nb, hg + 5 * nb))
    ml = o2
    ml_kv = jnp.concatenate([wl[:, ml + nb:ml + 2 * nb] * (HEAD ** -0.5), wl[:, ml + 2 * nb:ml + 3 * nb]],
                            axis=1).astype(BF)
    ml_q = cols((ml, ml + nb), (ml + 3 * nb, ml + 5 * nb))
    ml_gates = wl[:, ml + 5 * nb:o3]
    w_z = cols((o3, o3 + nb))
    w_xbc = cols((o3 + nb, o3 + nb + SSD_XBC))
    w_dt = wl[:, o3 + nb + SSD_XBC:]
    lb_w = jax.nn.softmax(hgrn_lb_logits.astype(F32), axis=0)
    lb = (jnp.cumsum(lb_w, axis=0) - lb_w[0:1])[l]
    return dict(ret_kv=ret_kv, ret_qg=ret_qg, hgrn_zv=hgrn_zv, hgrn_q=hgrn_q, ml_kv=ml_kv, ml_q=ml_q,
                ml_gates=ml_gates, w_z=w_z, w_xbc=w_xbc, w_dt=w_dt, lb=lb, w_out=w_out[l].astype(BF),
                w_gate=w_ple_gate[l].astype(BF), w_proj=w_ple_proj[l].astype(BF))


def kernel(x_prompt, x_sample, p_prompt, p_sample, w_in, w_out, ln_g, ln_b, ret_log_rate, ret_norm_g,
           hgrn_lb_logits, hgrn_norm_g, mlstm_i_bias, mlstm_f_bias, mlstm_norm_g, ssd_conv_w, ssd_conv_b,
           ssd_a_log, ssd_dt_bias, ssd_d, ssd_norm_g, w_ple_gate, w_ple_proj):
    depth = w_in.shape[0]
    alpha = (2 * depth) ** 0.25
    weights = [_layer_weights(l, w_in, w_out, hgrn_lb_logits, w_ple_gate, w_ple_proj) for l in range(depth)]

    def trunk(x, p):
        cos2, sin2 = _rope_tables(x.shape[1])
        xb = x.astype(BF)
        for l in range(depth):
            w = weights[l]
            mixed = (
                _retention(xb, w["ret_kv"], w["ret_qg"], ret_log_rate[l], ret_norm_g[l], cos2, sin2),
                _hgrn2(xb, w["hgrn_zv"], w["hgrn_q"], w["lb"], hgrn_norm_g[l]),
                _mlstm(xb, w["ml_kv"], w["ml_q"], w["ml_gates"], mlstm_i_bias[l], mlstm_f_bias[l],
                       mlstm_norm_g[l]),
                _ssd(xb, w["w_z"], w["w_xbc"], w["w_dt"], ssd_conv_w[l], ssd_conv_b[l], ssd_a_log[l],
                     ssd_dt_bias[l], ssd_d[l], ssd_norm_g[l]),
            )
            x, xb = _out_layer(x, p[l], mixed, w["w_out"], ln_g[l], ln_b[l], w["w_gate"], w["w_proj"], alpha)
        return x

    return trunk(x_prompt, p_prompt), trunk(x_sample, p_sample)
```

```python
import functools
import math

import numpy as np
import jax
import jax.numpy as jnp
from jax import lax
from jax.experimental import pallas as pl
from jax.experimental.pallas import tpu as pltpu

BF = jnp.bfloat16
F32 = jnp.float32

CHUNK = 128
TBLK = 1024
SUB = 512
HEAD = 128
BRANCH = 512
SSD_HEADDIM = 64
SSD_HEADS = 8
SSD_GROUPS = 2
SSD_CONV = 5
HALO = 16
ROPE_BASE = 10000.0
NORM_EPS = 1e-5
OUT_ROWS = 512
OUT_PARTS = 4
VMEM_LIMIT = 56 * 1024 * 1024


def _dot(a, b):
    return jnp.dot(a, b, preferred_element_type=F32)


def _dot_nt(a, b):
    return lax.dot_general(a, b, (((1,), (1,)), ((), ())), preferred_element_type=F32)


def _dot_tn(a, b):
    return lax.dot_general(a, b, (((0,), (0,)), ((), ())), preferred_element_type=F32)


def _split3(x):
    hi = x.astype(BF)
    r = x - hi.astype(F32)
    mid = r.astype(BF)
    lo = (r - mid.astype(F32)).astype(BF)
    return hi, mid, lo


def _sel_dot(sel, x):
    hi, mid, lo = _split3(x)
    return _dot(sel, hi) + _dot(sel, mid) + _dot(sel, lo)


def _dot_sel(x, sel):
    hi, mid, lo = _split3(x)
    return _dot(hi, sel) + _dot(mid, sel) + _dot(lo, sel)


def _sigmoid(x):
    return 1.0 / (1.0 + jnp.exp(-x))


def _silu(x):
    return x * _sigmoid(x)


def _softplus(x):
    return jnp.maximum(x, 0.0) + jnp.log1p(jnp.exp(-jnp.abs(x)))


def _log_sigmoid(x):
    return jnp.minimum(x, 0.0) - jnp.log1p(jnp.exp(-jnp.abs(x)))


def _iota2(shape, axis):
    return lax.broadcasted_iota(jnp.int32, shape, axis)


def _tri_masks(c):
    row = _iota2((c, c), 0)
    col = _iota2((c, c), 1)
    return col <= row, col >= row


def _prefix_suffix_rows(x):
    pre = _sel_dot(_tri_masks(x.shape[0])[0].astype(BF), x)
    return pre, pre[x.shape[0] - 1:, :] - pre + x


def _prefix_suffix_lanes(x):
    n = x.shape[1]
    pre = _dot_sel(x, _tri_masks(n)[1].astype(BF))
    return pre, pre[:, n - 1:] - pre + x


def _group_norm(o, center):
    if center:
        o = o - jnp.mean(o, axis=-1, keepdims=True)
    return o * lax.rsqrt(jnp.mean(o * o, axis=-1, keepdims=True) + NORM_EPS)


def _blk_index(phase, c, nblk):
    return jnp.where(phase == 0, nblk - 1 - c, c)


def _sweep(nblk, bwd_proj, bwd_chunk, fwd_proj, fwd_chunk, bwd_init, fwd_init, lookahead):
    phase = pl.program_id(1)
    c = pl.program_id(2)
    per = SUB // CHUNK
    nsub = TBLK // SUB

    def walk(order, base, first_chunk, proj, chunk):
        if not lookahead:
            for j in order:
                chunk(base + j * CHUNK, first_chunk + j, proj(base + j * CHUNK))
            return
        ahead = proj(base + order[0] * CHUNK)
        for n, j in enumerate(order):
            cur = ahead
            if n + 1 < per:
                ahead = proj(base + order[n + 1] * CHUNK)
            chunk(base + j * CHUNK, first_chunk + j, cur)

    @pl.when(phase == 0)
    def _():
        pl.when(c == 0)(bwd_init)
        blk = nblk - 1 - c

        def body(i, carry):
            sub = nsub - 1 - i
            walk(list(reversed(range(per))), pl.multiple_of(sub * SUB, SUB), (blk * nsub + sub) * per,
                 bwd_proj, bwd_chunk)
            return carry

        lax.fori_loop(0, nsub, body, 0)

    @pl.when(phase == 1)
    def _():
        pl.when(c == 0)(fwd_init)

        def body(sub, carry):
            walk(list(range(per)), pl.multiple_of(sub * SUB, SUB), (c * nsub + sub) * per, fwd_proj, fwd_chunk)
            return carry

        lax.fori_loop(0, nsub, body, 0)


def _mixer_call(body, b, t, in_arrays, in_specs, scratch, name):
    nblk = t // TBLK
    return pl.pallas_call(
        functools.partial(body, nblk=nblk),
        grid=(b, 2, nblk),
        in_specs=in_specs,
        out_specs=pl.BlockSpec((1, TBLK, BRANCH),
                               lambda bi, ph, c: (bi, jnp.where(ph == 0, 0, c), 0)),
        out_shape=jax.ShapeDtypeStruct((b, t, BRANCH), BF),
        scratch_shapes=scratch,
        compiler_params=pltpu.CompilerParams(
            dimension_semantics=("arbitrary", "arbitrary", "arbitrary"),
            vmem_limit_bytes=VMEM_LIMIT),
        name=name,
    )(*in_arrays)


def _x_spec(nblk, d):
    return pl.BlockSpec((1, TBLK, d), lambda bi, ph, c: (bi, _blk_index(ph, c, nblk), 0))


def _full_spec(shape):
    zeros = (0,) * len(shape)
    return pl.BlockSpec(shape, lambda bi, ph, c: zeros)


def _rows(ci):
    return pl.ds(pl.multiple_of(ci * CHUNK, CHUNK), CHUNK)


def _rope(x, cos2, sin2):
    return x * cos2 + pltpu.roll(x, HEAD // 2, axis=1) * sin2


def _ret_body(lg_ref, x_ref, cos_ref, sin_ref, wkv_ref, wqg_ref, ng_ref, o_ref,
              sf_ref, sb_ref, sbs_ref, m_ref, kv_ref, *, nblk):
    c_sz = CHUNK
    nh = BRANCH // HEAD
    scale = HEAD ** -0.5

    def bwd_init():
        sb_ref[...] = jnp.zeros_like(sb_ref)

    def fwd_init():
        sf_ref[...] = jnp.zeros_like(sf_ref)
        lower, upper = _tri_masks(c_sz)
        dist = (_iota2((c_sz, c_sz), 0) - _iota2((c_sz, c_sz), 1)).astype(F32)
        for h in range(nh):
            fwd = jnp.where(lower, jnp.exp(lg_ref[0, h] * dist), 0.0)
            bwd = jnp.where(upper, jnp.exp(-lg_ref[1, h] * dist), 0.0)
            m_ref[h] = (fwd + bwd) * scale

    def bwd_proj(r0):
        return _dot(x_ref[0, pl.ds(r0, c_sz), :], wkv_ref[...])

    def fwd_proj(r0):
        return _dot(x_ref[0, pl.ds(r0, c_sz), :], wqg_ref[...])

    def bwd_chunk(r0, ci, kv):
        rows = pl.ds(r0, c_sz)
        pos = _iota2((c_sz, HEAD), 0).astype(F32)
        span = jnp.full((1, HEAD), float(c_sz), F32)
        cos2 = cos_ref[rows, :]
        sin2 = sin_ref[rows, :]
        heads = range(nh)
        k = [_rope(kv[:, h * HEAD:(h + 1) * HEAD], cos2, sin2) for h in heads]
        v = [kv[:, BRANCH + h * HEAD:BRANCH + (h + 1) * HEAD].astype(BF) for h in heads]
        for h in heads:
            kv_ref[_rows(ci), h * HEAD:(h + 1) * HEAD] = k[h].astype(BF)
            kv_ref[_rows(ci), BRANCH + h * HEAD:BRANCH + (h + 1) * HEAD] = v[h]
            sbs_ref[ci, h] = sb_ref[h].astype(BF)
        k_in = [(k[h] * (jnp.exp(lg_ref[1, h] * pos) * scale)).astype(BF) for h in heads]
        for h in heads:
            sb_ref[h] = jnp.exp(lg_ref[1, h] * span) * sb_ref[h] + _dot_tn(k_in[h], v[h])

    def fwd_chunk(r0, ci, qg):
        rows = pl.ds(r0, c_sz)
        pos = _iota2((c_sz, HEAD), 0).astype(F32)
        span = jnp.full((1, HEAD), float(c_sz), F32)
        cos2 = cos_ref[rows, :]
        sin2 = sin_ref[rows, :]
        heads = range(nh)
        hs = [slice(h * HEAD, (h + 1) * HEAD) for h in heads]
        q = [_rope(qg[:, hs[h]], cos2, sin2).astype(BF) for h in heads]
        k = [kv_ref[_rows(ci), hs[h]] for h in heads]
        v = [kv_ref[_rows(ci), BRANCH + h * HEAD:BRANCH + (h + 1) * HEAD] for h in heads]
        p = [(_dot_nt(q[h], k[h]) * m_ref[h]).astype(BF) for h in heads]
        r = [_dot(q[h], jnp.concatenate([sf_ref[h].astype(BF), sbs_ref[ci, h]], axis=1)) for h in heads]
        o = [_dot(p[h], v[h]) for h in heads]
        o = [o[h] + jnp.exp(lg_ref[0, h] * (pos + 1.0)) * r[h][:, :HEAD] +
             jnp.exp(lg_ref[1, h] * (c_sz - pos)) * r[h][:, HEAD:] for h in heads]
        k_in = [(k[h].astype(F32) * (jnp.exp(lg_ref[0, h] * (c_sz - 1.0 - pos)) * scale)).astype(BF) for h in heads]
        for h in heads:
            sf_ref[h] = jnp.exp(lg_ref[0, h] * span) * sf_ref[h] + _dot_tn(k_in[h], v[h])
        for h in heads:
            y = _group_norm(o[h], True) * ng_ref[:, hs[h]] * _silu(qg[:, BRANCH + h * HEAD:BRANCH + (h + 1) * HEAD])
            o_ref[0, rows, hs[h]] = y.astype(o_ref.dtype)

    _sweep(nblk, bwd_proj, bwd_chunk, fwd_proj, fwd_chunk, bwd_init, fwd_init, lookahead=True)


def _retention(x, w_kv, w_qg, log_rate, norm_g, cos2, sin2):
    b, t, d = x.shape
    nblk = t // TBLK
    nc = t // CHUNK
    nh = BRANCH // HEAD
    lg = -jnp.exp(log_rate.astype(F32))
    in_specs = [
        pl.BlockSpec(memory_space=pltpu.SMEM),
        _x_spec(nblk, d),
        pl.BlockSpec((TBLK, HEAD), lambda bi, ph, c: (_blk_index(ph, c, nblk), 0)),
        pl.BlockSpec((TBLK, HEAD), lambda bi, ph, c: (_blk_index(ph, c, nblk), 0)),
        _full_spec(w_kv.shape),
        _full_spec(w_qg.shape),
        _full_spec((1, BRANCH)),
    ]
    scratch = [
        pltpu.VMEM((nh, HEAD, HEAD), F32),
        pltpu.VMEM((nh, HEAD, HEAD), F32),
        pltpu.VMEM((nc, nh, HEAD, HEAD), BF),
        pltpu.VMEM((nh, CHUNK, CHUNK), F32),
        pltpu.VMEM((t, 2 * BRANCH), BF),
    ]
    return _mixer_call(_ret_body, b, t, (lg, x, cos2, sin2, w_kv, w_qg, norm_g.reshape(1, BRANCH)),
                       in_specs, scratch, "retention")


HGRN_LEVELS = int(math.log2(CHUNK))


def _hgrn_masks():
    c = CHUNK
    t = np.arange(c)
    mask = np.zeros((2, HGRN_LEVELS + 1, c, c), np.float32)
    sign = np.zeros((2, HGRN_LEVELS, c), np.float32)
    for l in range(HGRN_LEVELS):
        s = 1 << l
        hi = (t % (2 * s)) >= s
        is_hi = hi[:, None]
        same = (t[:, None] // (2 * s)) == (t[None, :] // (2 * s))
        mask[0, l] = same & is_hi & ~is_hi.T
        mask[1, l] = same & ~is_hi & is_hi.T
        lower = 0.0 if s == 1 else -1.0
        sign[0, l] = np.where(hi, 1.0, lower)
        sign[1, l] = np.where(hi, lower, 1.0)
    mask[0, HGRN_LEVELS] = np.eye(c)
    sign = np.broadcast_to((sign * math.log2(math.e))[..., None], sign.shape + (HEAD,))
    return mask, np.ascontiguousarray(sign)


def _split_delta(cum, log_f, l, reverse):
    c, w = cum.shape
    s = 1 << l
    if s == 1:
        return log_f
    edge = s if reverse else s - 1
    if 2 * s >= 8:
        blocks = cum.reshape(c // (2 * s), 2 * s, w)
        at_split = jnp.broadcast_to(blocks[:, edge:edge + 1, :], blocks.shape).reshape(c, w)
    else:
        tiles = cum.reshape(c // 8, 8, w)
        sub = _iota2(tiles.shape, 1)
        at_split = None
        for blk in range(8 // (2 * s)):
            r = blk * 2 * s + edge
            row = jnp.broadcast_to(tiles[:, r:r + 1, :], tiles.shape)
            at_split = row if at_split is None else jnp.where(sub >= blk * 2 * s, row, at_split)
        at_split = at_split.reshape(c, w)
    return cum - at_split


def _hgrn_gates(z, lb):
    s = _sigmoid(z)
    log_f = jnp.log(lb + (1.0 - lb) * s)
    k = (1.0 - lb) * (1.0 - s)
    return k, log_f


def _hgrn_body(x_ref, wzv_ref, wq_ref, lb_ref, ng_ref, mask_ref, sign_ref, signb_ref, o_ref,
               sf_ref, sb_ref, sbs_ref, v_ref, *, nblk):
    c_sz = CHUNK
    nh = BRANCH // HEAD
    nl = HGRN_LEVELS

    def bwd_init():
        sb_ref[...] = jnp.zeros_like(sb_ref)

    def fwd_init():
        sf_ref[...] = jnp.zeros_like(sf_ref)

    def x_chunk(r0):
        return x_ref[0, pl.ds(r0, c_sz), :]

    def bwd_chunk(r0, ci, xb):
        zv = _dot(xb, wzv_ref[...])
        kb, lfb = _hgrn_gates(zv[:, :BRANCH], lb_ref[1:2, :])
        _, suf = _prefix_suffix_rows(lfb)
        total = suf[0:1, :]
        k_in = (kb * jnp.exp(total - suf)).astype(BF)
        dec = jnp.exp(total)
        v = zv[:, BRANCH:].astype(BF)
        v_ref[_rows(ci), :] = v
        for h in range(nh):
            sl = slice(h * HEAD, (h + 1) * HEAD)
            sbs_ref[ci, h] = sb_ref[h].astype(BF)
            sb_ref[h] = dec[:, sl] * sb_ref[h] + _dot_tn(v[:, sl], k_in[:, sl])

    def fwd_chunk(r0, ci, xb):
        rows = pl.ds(r0, c_sz)
        proj = _dot(xb, wq_ref[...])
        q = proj[:, :BRANCH]
        kf, lff = _hgrn_gates(proj[:, BRANCH:2 * BRANCH], lb_ref[0:1, :])
        kb, lfb = _hgrn_gates(proj[:, 2 * BRANCH:3 * BRANCH], lb_ref[1:2, :])
        g = proj[:, 3 * BRANCH:]
        v = v_ref[_rows(ci), :]

        scores = [None] * nh
        ksum = (kf + kb).astype(BF)
        qb = q.astype(BF)
        for h in range(nh):
            sl = slice(h * HEAD, (h + 1) * HEAD)
            scores[h] = _dot_nt(qb[:, sl], ksum[:, sl]).astype(BF) * mask_ref[0, nl]
        pre, _ = _prefix_suffix_rows(lff)
        _, suf = _prefix_suffix_rows(lfb)
        kfb = kf.astype(BF)
        kbb = kb.astype(BF)
        for dr, (kd, lfd, cum) in enumerate(((kfb, lff, pre), (kbb, lfb, suf))):
            for l in range(nl):
                sgn = jnp.concatenate([sign_ref[dr, l]] * nh, axis=1)
                e = jnp.exp2(_split_delta(cum, lfd, l, dr == 1) * sgn).astype(BF)
                is_query = jnp.concatenate([signb_ref[dr, l]] * nh, axis=1) > 0
                u = jnp.where(is_query, qb, kd) * e
                for h in range(nh):
                    sl = slice(h * HEAD, (h + 1) * HEAD)
                    scores[h] = scores[h] + _dot_nt(u[:, sl], u[:, sl]).astype(BF) * mask_ref[dr, l]

        q_f = (q * jnp.exp(pre)).astype(BF)
        q_b = (q * jnp.exp(suf)).astype(BF)
        total = pre[c_sz - 1:c_sz, :]
        k_in = (kf * jnp.exp(total - pre)).astype(BF)
        dec = jnp.exp(total)
        for h in range(nh):
            sl = slice(h * HEAD, (h + 1) * HEAD)
            o = _dot(scores[h], v[:, sl])
            q_cat = jnp.concatenate([q_f[:, sl], q_b[:, sl]], axis=1)
            s_cat = jnp.concatenate([sf_ref[h].astype(BF), sbs_ref[ci, h]], axis=1)
            o = o + _dot_nt(q_cat, s_cat)
            sf_ref[h] = dec[:, sl] * sf_ref[h] + _dot_tn(v[:, sl], k_in[:, sl])
            y = _group_norm(o, False) * ng_ref[:, sl] * _silu(g[:, sl])
            o_ref[0, rows, sl] = y.astype(o_ref.dtype)

    _sweep(nblk, x_chunk, bwd_chunk, x_chunk, fwd_chunk, bwd_init, fwd_init, lookahead=False)


def _hgrn2(x, w_zv, w_q, lb, norm_g):
    b, t, d = x.shape
    nblk = t // TBLK
    nc = t // CHUNK
    nh = BRANCH // HEAD
    mask, sign = _hgrn_masks()
    mask = jnp.asarray(mask, BF)
    sign = jnp.asarray(sign, F32)
    in_specs = [
        _x_spec(nblk, d),
        _full_spec(w_zv.shape),
        _full_spec(w_q.shape),
        _full_spec((2, BRANCH)),
        _full_spec((1, BRANCH)),
        _full_spec(mask.shape),
        _full_spec(sign.shape),
        _full_spec(sign.shape),
    ]
    scratch = [
        pltpu.VMEM((nh, HEAD, HEAD), F32),
        pltpu.VMEM((nh, HEAD, HEAD), F32),
        pltpu.VMEM((nc, nh, HEAD, HEAD), BF),
        pltpu.VMEM((t, BRANCH), BF),
    ]
    return _mixer_call(_hgrn_body, b, t, (x, w_zv, w_q, lb, norm_g.reshape(1, BRANCH), mask, sign, sign.astype(BF)),
                       in_specs, scratch, "hgrn2")


ML_HEADS = 4
ML_AUG = 2 * HEAD


def _cummax_rows(a, reverse):
    n = a.shape[0]
    row = _iota2(a.shape, 0)
    k = 1
    while k < n:
        if reverse:
            shifted = pltpu.roll(a, n - k, axis=0)
            a = jnp.where(row < n - k, jnp.maximum(a, shifted), a)
        else:
            shifted = pltpu.roll(a, k, axis=0)
            a = jnp.where(row >= k, jnp.maximum(a, shifted), a)
        k *= 2
    return a


def _ml_body(x_ref, wkv_ref, wq_ref, wgc_ref, wgr_ref, bc_ref, br_ref, ng_ref, o_ref,
             cf_ref, cb_ref, cbs_ref, m_ref, mbs_ref, kv_ref, *, nblk):
    c_sz = CHUNK
    nh = ML_HEADS

    def gates(xb):
        is_fwd = _iota2((c_sz, HEAD), 1) < nh
        gc = _dot(xb, wgc_ref[...]) + bc_ref[...]
        ipre = gc[:, :HEAD]
        pre, suf = _prefix_suffix_rows(_log_sigmoid(gc[:, HEAD:]))
        cum = jnp.where(is_fwd, pre, suf)
        a = ipre - cum
        run_max = jnp.where(is_fwd, _cummax_rows(a, False), _cummax_rows(a, True))
        return a, cum, run_max

    def edge(arr):
        return jnp.where(_iota2((1, HEAD), 1) < nh, arr[c_sz - 1:c_sz, :], arr[0:1, :])

    def ones_col():
        return (_iota2((c_sz, HEAD), 1) == 0).astype(BF)

    def update_state(state_ref, lane0, k_all, v_aug, wk, sc):
        for h in range(nh):
            l = lane0 + h
            kw = (k_all[:, h * HEAD:(h + 1) * HEAD] * wk[:, l:l + 1]).astype(BF)
            state_ref[h] = sc[:, l:l + 1] * state_ref[h] + _dot_tn(kw, v_aug[h])

    def bwd_init():
        cb_ref[...] = jnp.zeros_like(cb_ref)
        m_ref[...] = jnp.zeros_like(m_ref)

    def fwd_init():
        cf_ref[...] = jnp.zeros_like(cf_ref)
        m_ref[...] = jnp.zeros_like(m_ref)

    def x_chunk(r0):
        return x_ref[0, pl.ds(r0, c_sz), :]

    def bwd_chunk(r0, ci, xb):
        a, cum, run_max = gates(xb)
        m_prev = m_ref[...]
        mbs_ref[ci] = m_prev
        for h in range(nh):
            cbs_ref[ci, h] = cb_ref[h].astype(BF)
        mu_e = edge(jnp.maximum(m_prev, run_max))
        kv = _dot(xb, wkv_ref[...])
        kv_ref[_rows(ci), :] = kv.astype(BF)
        v_aug = [jnp.concatenate([kv[:, BRANCH + h * HEAD:BRANCH + (h + 1) * HEAD].astype(BF), ones_col()],
                                 axis=1) for h in range(nh)]
        update_state(cb_ref, nh, kv[:, :BRANCH], v_aug, jnp.exp(a - mu_e), jnp.exp(m_prev - mu_e))
        m_ref[...] = edge(cum) + mu_e

    def fwd_chunk(r0, ci, xb):
        rows = pl.ds(r0, c_sz)
        a, cum, run_max = gates(xb)
        lower, upper = _tri_masks(c_sz)
        m_prev = jnp.where(_iota2((1, HEAD), 1) < nh, m_ref[...], mbs_ref[ci])
        mu = jnp.maximum(m_prev, run_max)
        s_inter = jnp.exp(m_prev - mu)
        thr = jnp.exp(-(cum + mu))

        gr = _dot_nt(wgr_ref[...], xb) + br_ref[...]
        ipre_r = gr[:2 * nh]
        pre_r, suf_r = _prefix_suffix_lanes(_log_sigmoid(gr[2 * nh:]))
        a_r = ipre_r - jnp.where(_iota2((2 * nh, c_sz), 0) < nh, pre_r, suf_r)

        proj = _dot(xb, wq_ref[...])
        kvc = kv_ref[_rows(ci), :]
        v_aug = [jnp.concatenate([kvc[:, BRANCH + h * HEAD:BRANCH + (h + 1) * HEAD], ones_col()], axis=1)
                 for h in range(nh)]
        heads = range(nh)
        qs = [proj[:, h * HEAD:(h + 1) * HEAD].astype(BF) for h in heads]
        qk = [_dot_nt(qs[h], kvc[:, h * HEAD:(h + 1) * HEAD]) for h in heads]
        rs = [_dot(qs[h], jnp.concatenate([cf_ref[h].astype(BF), cbs_ref[ci, h]], axis=1)) for h in heads]
        pairs = [(dr, h) for h in heads for dr in range(2)]
        wgt = {}
        for dr, h in pairs:
            l = dr * nh + h
            wgt[dr, h] = jnp.where(lower if dr == 0 else upper, jnp.exp(a_r[l:l + 1, :] - mu[:, l:l + 1]), 0.0)
        pv = {p: _dot((qk[p[1]] * wgt[p]).astype(BF), v_aug[p[1]]) for p in pairs}
        outs = {}
        for dr, h in pairs:
            l = dr * nh + h
            tot = s_inter[:, l:l + 1] * rs[h][:, dr * ML_AUG:(dr + 1) * ML_AUG] + pv[dr, h]
            den = jnp.maximum(jnp.abs(tot[:, HEAD:HEAD + 1]), thr[:, l:l + 1])
            outs[dr, h] = tot[:, :HEAD] / den
        for h in heads:
            sl = slice(h * HEAD, (h + 1) * HEAD)
            o_gate = proj[:, BRANCH + h * HEAD:BRANCH + (h + 1) * HEAD]
            g = proj[:, 2 * BRANCH + h * HEAD:2 * BRANCH + (h + 1) * HEAD]
            y = _group_norm(_sigmoid(o_gate) * (outs[0, h] + outs[1, h]), True) * ng_ref[:, sl] * _silu(g)
            o_ref[0, rows, sl] = y.astype(o_ref.dtype)

        mu_e = edge(mu)
        update_state(cf_ref, 0, kvc[:, :BRANCH].astype(F32), v_aug, jnp.exp(a - mu_e), jnp.exp(m_prev - mu_e))
        m_ref[...] = edge(cum) + mu_e

    _sweep(nblk, x_chunk, bwd_chunk, x_chunk, fwd_chunk, bwd_init, fwd_init, lookahead=False)


def _mlstm(x, w_kv, w_q, w_gates, i_bias, f_bias, norm_g):
    b, t, d = x.shape
    nblk = t // TBLK
    nc = t // CHUNK
    nh = ML_HEADS
    pad = HEAD - 2 * nh
    w_i = w_gates[:, :2 * nh]
    w_f = w_gates[:, 2 * nh:]
    wgc = jnp.concatenate([jnp.pad(w_i, ((0, 0), (0, pad))), jnp.pad(w_f, ((0, 0), (0, pad)))], axis=1)
    wgr = w_gates.T
    bias = jnp.concatenate([i_bias.reshape(-1), f_bias.reshape(-1)]).astype(F32)
    bc = jnp.concatenate([jnp.pad(bias[:2 * nh], (0, pad)), jnp.pad(bias[2 * nh:], (0, pad))]).reshape(1, 2 * HEAD)
    br = bias.reshape(4 * nh, 1)
    in_specs = [
        _x_spec(nblk, d),
        _full_spec(w_kv.shape),
        _full_spec(w_q.shape),
        _full_spec(wgc.shape),
        _full_spec(wgr.shape),
        _full_spec(bc.shape),
        _full_spec(br.shape),
        _full_spec((1, BRANCH)),
    ]
    scratch = [
        pltpu.VMEM((nh, HEAD, ML_AUG), F32),
        pltpu.VMEM((nh, HEAD, ML_AUG), F32),
        pltpu.VMEM((nc, nh, HEAD, ML_AUG), BF),
        pltpu.VMEM((1, HEAD), F32),
        pltpu.VMEM((nc, 1, HEAD), F32),
        pltpu.VMEM((t, 2 * BRANCH), BF),
    ]
    return _mixer_call(_ml_body, b, t,
                       (x, w_kv, w_q, wgc.astype(BF), wgr.astype(BF), bc, br, norm_g.reshape(1, BRANCH)),
                       in_specs, scratch, "mlstm")


SSD_XBC = BRANCH + 2 * SSD_GROUPS * HEAD
SSD_GW = BRANCH // SSD_GROUPS


def _ssd_body(x_ref, xp_ref, xn_ref, wz_ref, wx_ref, wdc_ref, wdr_ref, dbc_ref, dbr_ref,
              ac_ref, ar_ref, cw_ref, cb_ref, dsk_ref, ng_ref, ex_ref, o_ref,
              xe_ref, sf_ref, sb_ref, sbs_ref, act_ref, *, nblk):
    c_sz = CHUNK
    nh = SSD_HEADS
    per = TBLK // CHUNK

    def dt_columns(xb):
        is_fwd = _iota2((c_sz, HEAD), 1) < nh
        dt = _softplus(_dot(xb, wdc_ref[...]) + dbc_ref[...])
        pre, suf = _prefix_suffix_rows(dt * ac_ref[...])
        return dt, jnp.where(is_fwd, pre, suf)

    def expand(a, dr):
        hi, mid, _ = _split3(a)
        e = ex_ref[:, dr * BRANCH:(dr + 1) * BRANCH]
        return _dot(hi, e) + _dot(mid, e)

    def update_state(state_ref, dr, xs, bm, dt, cum):
        cum_e = jnp.where(_iota2((1, HEAD), 1) < nh, cum[c_sz - 1:c_sz, :], cum[0:1, :])
        w_state = jnp.exp(cum_e - cum) * dt
        dec = jnp.exp(jnp.broadcast_to(cum_e, (8, HEAD)))
        xw = (xs * expand(w_state, dr)).astype(BF)
        d512 = expand(dec, dr)[0:1, :]
        for g in range(SSD_GROUPS):
            gs = slice(g * SSD_GW, (g + 1) * SSD_GW)
            state_ref[g] = d512[:, gs] * state_ref[g] + _dot_tn(bm[:, g * HEAD:(g + 1) * HEAD], xw[:, gs])

    def bwd_init():
        sb_ref[...] = jnp.zeros_like(sb_ref)

    def fwd_init():
        sf_ref[...] = jnp.zeros_like(sf_ref)

    def project_block():
        blk = nblk - 1 - pl.program_id(2)

        def body(j, carry):
            r0 = pl.multiple_of(j * c_sz, c_sz)
            xe_ref[pl.ds(HALO + r0, c_sz), :] = _dot(x_ref[0, pl.ds(r0, c_sz), :], wx_ref[...])
            return carry

        lax.fori_loop(0, per, body, 0)
        halo = jnp.concatenate([xp_ref[0, 0], xn_ref[0, 0]], axis=0)
        ph = _dot(halo, wx_ref[...])
        xe_ref[0:HALO, :] = ph[:HALO] * jnp.where(blk > 0, 1.0, 0.0)
        xe_ref[HALO + TBLK:, :] = ph[HALO:] * jnp.where(blk < nblk - 1, 1.0, 0.0)

    def x_chunk(r0):
        return x_ref[0, pl.ds(r0, c_sz), :]

    def bwd_chunk(r0, ci, xb):
        half = (SSD_CONV - 1) // 2
        win = xe_ref[pl.ds(r0, c_sz + 2 * HALO), :]
        conv = cb_ref[...]
        for k in range(SSD_CONV):
            conv = conv + cw_ref[k:k + 1, :] * win[HALO - half + k:HALO - half + k + c_sz, :]
        act = _silu(conv)
        act_ref[_rows(ci), :] = act.astype(BF)
        dt, cum = dt_columns(xb)
        for g in range(SSD_GROUPS):
            sbs_ref[ci, g] = sb_ref[g].astype(BF)
        update_state(sb_ref, 1, act[:, :BRANCH], act[:, BRANCH:BRANCH + SSD_GROUPS * HEAD].astype(BF), dt, cum)

    def fwd_chunk(r0, ci, xb):
        rows = pl.ds(r0, c_sz)
        act = act_ref[_rows(ci), :]
        xs = act[:, :BRANCH].astype(F32)
        bm = act[:, BRANCH:BRANCH + SSD_GROUPS * HEAD]
        cm = act[:, BRANCH + SSD_GROUPS * HEAD:]
        dt, cum = dt_columns(xb)
        dt_r = _softplus(_dot_nt(wdr_ref[...], xb) + dbr_ref[...])
        pre_r, suf_r = _prefix_suffix_lanes(dt_r * ar_ref[...])
        cum_r = jnp.where(_iota2((2 * nh, c_sz), 0) < nh, pre_r, suf_r)
        shifted_r = cum_r - jnp.log(dt_r)
        below = _iota2((c_sz, c_sz), 1) < _iota2((c_sz, c_sz), 0)
        diag = _iota2((c_sz, c_sz), 1) == _iota2((c_sz, c_sz), 0)
        z = _dot(xb, wz_ref[...])
        lane_half = _iota2((c_sz, HEAD), 1) < SSD_HEADDIM
        ecum = jnp.exp(cum)
        ecum_f = expand(ecum, 0)
        ecum_b = expand(ecum, 1)
        hpg = SSD_HEADS // SSD_GROUPS
        groups = range(SSD_GROUPS)
        heads = range(nh)
        cg = [cm[:, g * HEAD:(g + 1) * HEAD] for g in groups]
        gmat = [_dot_nt(cg[g], bm[:, g * HEAD:(g + 1) * HEAD]) for g in groups]
        from_f = [_dot(cg[g], sf_ref[g].astype(BF)) for g in groups]
        from_b = [_dot(cg[g], sbs_ref[ci, g]) for g in groups]
        m = [jnp.exp(jnp.where(below, cum[:, h:h + 1] - shifted_r[h:h + 1, :],
                               cum[:, nh + h:nh + h + 1] - shifted_r[nh + h:nh + h + 1, :])) +
             jnp.where(diag, dt_r[h:h + 1, :], 0.0) for h in heads]
        xh = [jnp.where(lane_half if h % 2 == 0 else jnp.logical_not(lane_half),
                        xs[:, (h // 2) * HEAD:(h // 2 + 1) * HEAD], 0.0).astype(BF) for h in heads]
        part = [_dot((gmat[h // hpg] * m[h]).astype(BF), xh[h]) for h in heads]
        ys = []
        for g in groups:
            gs = slice(g * SSD_GW, (g + 1) * SSD_GW)
            pieces = [part[g * hpg + 2 * pair] + part[g * hpg + 2 * pair + 1] for pair in range(hpg // 2)]
            ys.append(jnp.concatenate(pieces, axis=1) + ecum_f[:, gs] * from_f[g] + ecum_b[:, gs] * from_b[g])
        y = jnp.concatenate(ys, axis=1) + dsk_ref[...] * xs
        y = y * _silu(z)
        for g in range(SSD_GROUPS):
            gs = slice(g * SSD_GW, (g + 1) * SSD_GW)
            o_ref[0, rows, gs] = (_group_norm(y[:, gs], False) * ng_ref[:, gs]).astype(o_ref.dtype)
        update_state(sf_ref, 0, xs, bm, dt, cum)

    pl.when(pl.program_id(1) == 0)(project_block)
    _sweep(nblk, x_chunk, bwd_chunk, x_chunk, fwd_chunk, bwd_init, fwd_init, lookahead=False)


def _ssd(x, w_z, w_xbc, w_dt, conv_w, conv_b, a_log, dt_bias, d_skip, norm_g):
    b, t, d = x.shape
    nblk = t // TBLK
    nc = t // CHUNK
    nh = SSD_HEADS
    pad = HEAD - 2 * nh
    wdc = jnp.pad(w_dt, ((0, 0), (0, pad)))
    wdr = w_dt.T
    db = dt_bias.reshape(-1).astype(F32)
    a = (-jnp.exp(a_log.astype(F32))).reshape(-1)
    dbc = jnp.pad(db, (0, pad)).reshape(1, HEAD)
    ac = jnp.pad(a, (0, pad)).reshape(1, HEAD)
    dbr = db.reshape(2 * nh, 1)
    ar = a.reshape(2 * nh, 1)
    cw = jnp.pad(conv_w.astype(F32), ((0, 8 - SSD_CONV), (0, 0)))
    dsk = jnp.repeat(d_skip.astype(F32), SSD_HEADDIM).reshape(1, BRANCH)
    ex = np.zeros((HEAD, 2 * BRANCH), np.float32)
    for l in range(2 * nh):
        ex[l, l * SSD_HEADDIM:(l + 1) * SSD_HEADDIM] = 1.0
    ex = jnp.asarray(ex, BF)
    xh = x.reshape(b, t // HALO, HALO, d)
    per = TBLK // HALO
    last = t // HALO - 1
    in_specs = [
        _x_spec(nblk, d),
        pl.BlockSpec((1, 1, HALO, d),
                     lambda bi, ph, c: (bi, jnp.maximum(_blk_index(ph, c, nblk) * per - 1, 0), 0, 0)),
        pl.BlockSpec((1, 1, HALO, d),
                     lambda bi, ph, c: (bi, jnp.minimum((_blk_index(ph, c, nblk) + 1) * per, last), 0, 0)),
        _full_spec(w_z.shape),
        _full_spec(w_xbc.shape),
        _full_spec(wdc.shape),
        _full_spec(wdr.shape),
        _full_spec(dbc.shape),
        _full_spec(dbr.shape),
        _full_spec(ac.shape),
        _full_spec(ar.shape),
        _full_spec(cw.shape),
        _full_spec((1, SSD_XBC)),
        _full_spec((1, BRANCH)),
        _full_spec((1, BRANCH)),
        _full_spec(ex.shape),
    ]
    scratch = [
        pltpu.VMEM((TBLK + 2 * HALO, SSD_XBC), F32),
        pltpu.VMEM((SSD_GROUPS, HEAD, SSD_GW), F32),
        pltpu.VMEM((SSD_GROUPS, HEAD, SSD_GW), F32),
        pltpu.VMEM((nc, SSD_GROUPS, HEAD, SSD_GW), BF),
        pltpu.VMEM((t, SSD_XBC), BF),
    ]
    return _mixer_call(_ssd_body, b, t,
                       (x, xh, xh, w_z, w_xbc, wdc.astype(BF), wdr.astype(BF), dbc, dbr, ac, ar, cw,
                        conv_b.astype(F32).reshape(1, SSD_XBC), dsk, norm_g.reshape(1, BRANCH), ex),
                       in_specs, scratch, "ssd")


def _out_body(x_ref, p_ref, m0_ref, m1_ref, m2_ref, m3_ref, wo_ref, lg_ref, lb_ref, wg_ref, wp_ref,
              o_ref, ob_ref, *, alpha):
    step = OUT_ROWS // OUT_PARTS
    parts = [slice(i * step, (i + 1) * step) for i in range(OUT_PARTS)]

    def mix(rows):
        y = None
        for i, m_ref in enumerate((m0_ref, m1_ref, m2_ref, m3_ref)):
            part = _dot(m_ref[0, rows, :], wo_ref[i * BRANCH:(i + 1) * BRANCH, :])
            y = part if y is None else y + part
        return y

    def norm(rows, y):
        r = alpha * x_ref[0, rows, :] + y
        mu = jnp.mean(r, axis=-1, keepdims=True)
        rc = r - mu
        var = jnp.mean(rc * rc, axis=-1, keepdims=True)
        return rc * lax.rsqrt(var + NORM_EPS) * lg_ref[...] + lb_ref[...]

    def finish(rows, xn):
        gate = _sigmoid(_dot(xn.astype(BF), wg_ref[...]))
        out = xn + gate * _dot(p_ref[0, rows, :].astype(BF), wp_ref[...])
        o_ref[0, rows, :] = out
        ob_ref[0, rows, :] = out.astype(BF)

    y = mix(parts[0])
    xn = None
    for i in range(OUT_PARTS):
        y_next = mix(parts[i + 1]) if i + 1 < OUT_PARTS else None
        xn_i = norm(parts[i], y)
        if xn is not None:
            finish(parts[i - 1], xn)
        xn = xn_i
        y = y_next
    finish(parts[OUT_PARTS - 1], xn)


def _out_layer(x, p, mixed, w_out, ln_g, ln_b, w_gate, w_proj, alpha):
    b, t, d = x.shape
    dp = p.shape[-1]
    rows = OUT_ROWS
    row_spec = lambda width: pl.BlockSpec((1, rows, width), lambda bi, r: (bi, r, 0))
    full = lambda shape: pl.BlockSpec(shape, lambda bi, r: (0,) * len(shape))
    return pl.pallas_call(
        functools.partial(_out_body, alpha=alpha),
        grid=(b, t // rows),
        in_specs=[row_spec(d), row_spec(dp)] + [row_spec(BRANCH)] * 4 +
                 [full(w_out.shape), full((1, d)), full((1, d)), full(w_gate.shape), full(w_proj.shape)],
        out_specs=(row_spec(d), row_spec(d)),
        out_shape=(jax.ShapeDtypeStruct((b, t, d), F32), jax.ShapeDtypeStruct((b, t, d), BF)),
        compiler_params=pltpu.CompilerParams(
            dimension_semantics=("arbitrary", "arbitrary"), vmem_limit_bytes=VMEM_LIMIT),
        name="out_proj",
    )(x, p, *mixed, w_out, ln_g.reshape(1, d), ln_b.reshape(1, d), w_gate, w_proj)


def _rope_tables(t):
    inv = ROPE_BASE ** (-jnp.arange(0, HEAD, 2, dtype=F32) / HEAD)
    ang = jnp.arange(t, dtype=F32)[:, None] * inv[None, :]
    cos = jnp.cos(ang)
    sin = jnp.sin(ang)
    return jnp.concatenate([cos, cos], axis=1), jnp.concatenate([-sin, sin], axis=1)


def _layer_weights(l, w_in, w_out, hgrn_lb_logits, w_ple_gate, w_ple_proj):
    wl = w_in[l]
    nb = BRANCH
    o1 = 4 * nb
    o2 = o1 + 5 * nb
    o3 = o2 + 5 * nb + 4 * ML_HEADS

    def cols(*ranges):
        return jnp.concatenate([wl[:, a:b] for a, b in ranges], axis=1).astype(BF)

    ret_kv = cols((nb, 3 * nb))
    ret_qg = cols((0, nb), (3 * nb, 4 * nb))
    hg = o1
    hgrn_zv = cols((hg + 2 * nb, hg + 4 * nb))
    hgrn_q = cols((hg, hg + 3 * nb), (hg + 4 * nb, hg + 5 * nb))
    ml = o2
    ml_kv = jnp.concatenate([wl[:, ml + nb:ml + 2 * nb] * (HEAD ** -0.5), wl[:, ml + 2 * nb:ml + 3 * nb]],
                            axis=1).astype(BF)
    ml_q = cols((ml, ml + nb), (ml + 3 * nb, ml + 5 * nb))
    ml_gates = wl[:, ml + 5 * nb:o3]
    w_z = cols((o3, o3 + nb))
    w_xbc = cols((o3 + nb, o3 + nb + SSD_XBC))
    w_dt = wl[:, o3 + nb + SSD_XBC:]
    lb_w = jax.nn.softmax(hgrn_lb_logits.astype(F32), axis=0)
    lb = (jnp.cumsum(lb_w, axis=0) - lb_w[0:1])[l]
    return dict(ret_kv=ret_kv, ret_qg=ret_qg, hgrn_zv=hgrn_zv, hgrn_q=hgrn_q, ml_kv=ml_kv, ml_q=ml_q,
                ml_gates=ml_gates, w_z=w_z, w_xbc=w_xbc, w_dt=w_dt, lb=lb, w_out=w_out[l].astype(BF),
                w_gate=w_ple_gate[l].astype(BF), w_proj=w_ple_proj[l].astype(BF))


def kernel(x_prompt, x_sample, p_prompt, p_sample, w_in, w_out, ln_g, ln_b, ret_log_rate, ret_norm_g,
           hgrn_lb_logits, hgrn_norm_g, mlstm_i_bias, mlstm_f_bias, mlstm_norm_g, ssd_conv_w, ssd_conv_b,
           ssd_a_log, ssd_dt_bias, ssd_d, ssd_norm_g, w_ple_gate, w_ple_proj):
    depth = w_in.shape[0]
    alpha = (2 * depth) ** 0.25
    weights = [_layer_weights(l, w_in, w_out, hgrn_lb_logits, w_ple_gate, w_ple_proj) for l in range(depth)]

    def trunk(x, p):
        cos2, sin2 = _rope_tables(x.shape[1])
        xb = x.astype(BF)
        for l in range(depth):
            w = weights[l]
            mixed = (
                _retention(xb, w["ret_kv"], w["ret_qg"], ret_log_rate[l], ret_norm_g[l], cos2, sin2),
                _hgrn2(xb, w["hgrn_zv"], w["hgrn_q"], w["lb"], hgrn_norm_g[l]),
                _mlstm(xb, w["ml_kv"], w["ml_q"], w["ml_gates"], mlstm_i_bias[l], mlstm_f_bias[l],
                       mlstm_norm_g[l]),
                _ssd(xb, w["w_z"], w["w_xbc"], w["w_dt"], ssd_conv_w[l], ssd_conv_b[l], ssd_a_log[l],
                     ssd_dt_bias[l], ssd_d[l], ssd_norm_g[l]),
            )
            x, xb = _out_layer(x, p[l], mixed, w["w_out"], ln_g[l], ln_b[l], w["w_gate"], w["w_proj"], alpha)
        return x

    return trunk(x_prompt, p_prompt), trunk(x_sample, p_sample)
```

```python
import functools
import math

import numpy as np
import jax
import jax.numpy as jnp
from jax import lax
from jax.experimental import pallas as pl
from jax.experimental.pallas import tpu as pltpu

BF = jnp.bfloat16
F32 = jnp.float32

CHUNK = 128
TBLK = 1024
SUB = 512
HEAD = 128
BRANCH = 512
SSD_HEADDIM = 64
SSD_HEADS = 8
SSD_GROUPS = 2
SSD_CONV = 5
HALO = 16
ROPE_BASE = 10000.0
NORM_EPS = 1e-5
OUT_ROWS = 512
OUT_PARTS = 4
VMEM_LIMIT = 56 * 1024 * 1024


def _dot(a, b):
    return jnp.dot(a, b, preferred_element_type=F32)


def _dot_nt(a, b):
    return lax.dot_general(a, b, (((1,), (1,)), ((), ())), preferred_element_type=F32)


def _dot_tn(a, b):
    return lax.dot_general(a, b, (((0,), (0,)), ((), ())), preferred_element_type=F32)


def _split3(x):
    hi = x.astype(BF)
    r = x - hi.astype(F32)
    mid = r.astype(BF)
    lo = (r - mid.astype(F32)).astype(BF)
    return hi, mid, lo


def _sel_dot(sel, x):
    hi, mid, lo = _split3(x)
    return _dot(sel, hi) + _dot(sel, mid) + _dot(sel, lo)


def _dot_sel(x, sel):
    hi, mid, lo = _split3(x)
    return _dot(hi, sel) + _dot(mid, sel) + _dot(lo, sel)


def _sigmoid(x):
    return 1.0 / (1.0 + jnp.exp(-x))


def _silu(x):
    return x * _sigmoid(x)


def _softplus(x):
    return jnp.maximum(x, 0.0) + jnp.log1p(jnp.exp(-jnp.abs(x)))


def _log_sigmoid(x):
    return jnp.minimum(x, 0.0) - jnp.log1p(jnp.exp(-jnp.abs(x)))


def _iota2(shape, axis):
    return lax.broadcasted_iota(jnp.int32, shape, axis)


def _tri_masks(c):
    row = _iota2((c, c), 0)
    col = _iota2((c, c), 1)
    return col <= row, col >= row


def _prefix_suffix_rows(x):
    pre = _sel_dot(_tri_masks(x.shape[0])[0].astype(BF), x)
    return pre, pre[x.shape[0] - 1:, :] - pre + x


def _prefix_suffix_lanes(x):
    n = x.shape[1]
    pre = _dot_sel(x, _tri_masks(n)[1].astype(BF))
    return pre, pre[:, n - 1:] - pre + x


def _group_norm(o, center):
    if center:
        o = o - jnp.mean(o, axis=-1, keepdims=True)
    return o * lax.rsqrt(jnp.mean(o * o, axis=-1, keepdims=True) + NORM_EPS)


def _blk_index(phase, c, nblk):
    return jnp.where(phase == 0, nblk - 1 - c, c)


def _sweep(nblk, bwd_proj, bwd_chunk, fwd_proj, fwd_chunk, bwd_init, fwd_init, lookahead):
    phase = pl.program_id(1)
    c = pl.program_id(2)
    per = SUB // CHUNK
    nsub = TBLK // SUB

    def walk(order, base, first_chunk, proj, chunk):
        if not lookahead:
            for j in order:
                chunk(base + j * CHUNK, first_chunk + j, proj(base + j * CHUNK))
            return
        ahead = proj(base + order[0] * CHUNK)
        for n, j in enumerate(order):
            cur = ahead
            if n + 1 < per:
                ahead = proj(base + order[n + 1] * CHUNK)
            chunk(base + j * CHUNK, first_chunk + j, cur)

    @pl.when(phase == 0)
    def _():
        pl.when(c == 0)(bwd_init)
        blk = nblk - 1 - c

        def body(i, carry):
            sub = nsub - 1 - i
            walk(list(reversed(range(per))), pl.multiple_of(sub * SUB, SUB), (blk * nsub + sub) * per,
                 bwd_proj, bwd_chunk)
            return carry

        lax.fori_loop(0, nsub, body, 0)

    @pl.when(phase == 1)
    def _():
        pl.when(c == 0)(fwd_init)

        def body(sub, carry):
            walk(list(range(per)), pl.multiple_of(sub * SUB, SUB), (c * nsub + sub) * per, fwd_proj, fwd_chunk)
            return carry

        lax.fori_loop(0, nsub, body, 0)


def _mixer_call(body, b, t, in_arrays, in_specs, scratch, name):
    nblk = t // TBLK
    return pl.pallas_call(
        functools.partial(body, nblk=nblk),
        grid=(b, 2, nblk),
        in_specs=in_specs,
        out_specs=pl.BlockSpec((1, TBLK, BRANCH),
                               lambda bi, ph, c: (bi, jnp.where(ph == 0, 0, c), 0)),
        out_shape=jax.ShapeDtypeStruct((b, t, BRANCH), BF),
        scratch_shapes=scratch,
        compiler_params=pltpu.CompilerParams(
            dimension_semantics=("arbitrary", "arbitrary", "arbitrary"),
            vmem_limit_bytes=VMEM_LIMIT),
        name=name,
    )(*in_arrays)


def _x_spec(nblk, d):
    return pl.BlockSpec((1, TBLK, d), lambda bi, ph, c: (bi, _blk_index(ph, c, nblk), 0))


def _full_spec(shape):
    zeros = (0,) * len(shape)
    return pl.BlockSpec(shape, lambda bi, ph, c: zeros)


def _rows(ci):
    return pl.ds(pl.multiple_of(ci * CHUNK, CHUNK), CHUNK)


def _rope(x, cos2, sin2):
    return x * cos2 + pltpu.roll(x, HEAD // 2, axis=1) * sin2


def _ret_body(lg_ref, x_ref, cos_ref, sin_ref, wkv_ref, wqg_ref, ng_ref, o_ref,
              sf_ref, sb_ref, sbs_ref, m_ref, kv_ref, *, nblk):
    c_sz = CHUNK
    nh = BRANCH // HEAD
    scale = HEAD ** -0.5

    def bwd_init():
        sb_ref[...] = jnp.zeros_like(sb_ref)

    def fwd_init():
        sf_ref[...] = jnp.zeros_like(sf_ref)
        lower, upper = _tri_masks(c_sz)
        dist = (_iota2((c_sz, c_sz), 0) - _iota2((c_sz, c_sz), 1)).astype(F32)
        for h in range(nh):
            fwd = jnp.where(lower, jnp.exp(lg_ref[0, h] * dist), 0.0)
            bwd = jnp.where(upper, jnp.exp(-lg_ref[1, h] * dist), 0.0)
            m_ref[h] = (fwd + bwd) * scale

    def bwd_proj(r0):
        return _dot(x_ref[0, pl.ds(r0, c_sz), :], wkv_ref[...])

    def fwd_proj(r0):
        return _dot(x_ref[0, pl.ds(r0, c_sz), :], wqg_ref[...])

    def bwd_chunk(r0, ci, kv):
        rows = pl.ds(r0, c_sz)
        pos = _iota2((c_sz, HEAD), 0).astype(F32)
        span = jnp.full((1, HEAD), float(c_sz), F32)
        cos2 = cos_ref[rows, :]
        sin2 = sin_ref[rows, :]
        heads = range(nh)
        k = [_rope(kv[:, h * HEAD:(h + 1) * HEAD], cos2, sin2) for h in heads]
        v = [kv[:, BRANCH + h * HEAD:BRANCH + (h + 1) * HEAD].astype(BF) for h in heads]
        for h in heads:
            kv_ref[_rows(ci), h * HEAD:(h + 1) * HEAD] = k[h].astype(BF)
            kv_ref[_rows(ci), BRANCH + h * HEAD:BRANCH + (h + 1) * HEAD] = v[h]
            sbs_ref[ci, h] = sb_ref[h].astype(BF)
        k_in = [(k[h] * (jnp.exp(lg_ref[1, h] * pos) * scale)).astype(BF) for h in heads]
        for h in heads:
            sb_ref[h] = jnp.exp(lg_ref[1, h] * span) * sb_ref[h] + _dot_tn(k_in[h], v[h])

    def fwd_chunk(r0, ci, qg):
        rows = pl.ds(r0, c_sz)
        pos = _iota2((c_sz, HEAD), 0).astype(F32)
        span = jnp.full((1, HEAD), float(c_sz), F32)
        cos2 = cos_ref[rows, :]
        sin2 = sin_ref[rows, :]
        heads = range(nh)
        hs = [slice(h * HEAD, (h + 1) * HEAD) for h in heads]
        q = [_rope(qg[:, hs[h]], cos2, sin2).astype(BF) for h in heads]
        k = [kv_ref[_rows(ci), hs[h]] for h in heads]
        v = [kv_ref[_rows(ci), BRANCH + h * HEAD:BRANCH + (h + 1) * HEAD] for h in heads]
        p = [(_dot_nt(q[h], k[h]) * m_ref[h]).astype(BF) for h in heads]
        r = [_dot(q[h], jnp.concatenate([sf_ref[h].astype(BF), sbs_ref[ci, h]], axis=1)) for h in heads]
        o = [_dot(p[h], v[h]) for h in heads]
        o = [o[h] + jnp.exp(lg_ref[0, h] * (pos + 1.0)) * r[h][:, :HEAD] +
             jnp.exp(lg_ref[1, h] * (c_sz - pos)) * r[h][:, HEAD:] for h in heads]
        k_in = [(k[h].astype(F32) * (jnp.exp(lg_ref[0, h] * (c_sz - 1.0 - pos)) * scale)).astype(BF) for h in heads]
        for h in heads:
            sf_ref[h] = jnp.exp(lg_ref[0, h] * span) * sf_ref[h] + _dot_tn(k_in[h], v[h])
        for h in heads:
            y = _group_norm(o[h], True) * ng_ref[:, hs[h]] * _silu(qg[:, BRANCH + h * HEAD:BRANCH + (h + 1) * HEAD])
            o_ref[0, rows, hs[h]] = y.astype(o_ref.dtype)

    _sweep(nblk, bwd_proj, bwd_chunk, fwd_proj, fwd_chunk, bwd_init, fwd_init, lookahead=True)


def _retention(x, w_kv, w_qg, log_rate, norm_g, cos2, sin2):
    b, t, d = x.shape
    nblk = t // TBLK
    nc = t // CHUNK
    nh = BRANCH // HEAD
    lg = -jnp.exp(log_rate.astype(F32))
    in_specs = [
        pl.BlockSpec(memory_space=pltpu.SMEM),
        _x_spec(nblk, d),
        pl.BlockSpec((TBLK, HEAD), lambda bi, ph, c: (_blk_index(ph, c, nblk), 0)),
        pl.BlockSpec((TBLK, HEAD), lambda bi, ph, c: (_blk_index(ph, c, nblk), 0)),
        _full_spec(w_kv.shape),
        _full_spec(w_qg.shape),
        _full_spec((1, BRANCH)),
    ]
    scratch = [
        pltpu.VMEM((nh, HEAD, HEAD), F32),
        pltpu.VMEM((nh, HEAD, HEAD), F32),
        pltpu.VMEM((nc, nh, HEAD, HEAD), BF),
        pltpu.VMEM((nh, CHUNK, CHUNK), F32),
        pltpu.VMEM((t, 2 * BRANCH), BF),
    ]
    return _mixer_call(_ret_body, b, t, (lg, x, cos2, sin2, w_kv, w_qg, norm_g.reshape(1, BRANCH)),
                       in_specs, scratch, "retention")


HGRN_LEVELS = int(math.log2(CHUNK))


def _hgrn_masks():
    c = CHUNK
    t = np.arange(c)
    mask = np.zeros((2, HGRN_LEVELS + 1, c, c), np.float32)
    sign = np.zeros((2, HGRN_LEVELS, c), np.float32)
    for l in range(HGRN_LEVELS):
        s = 1 << l
        hi = (t % (2 * s)) >= s
        is_hi = hi[:, None]
        same = (t[:, None] // (2 * s)) == (t[None, :] // (2 * s))
        mask[0, l] = same & is_hi & ~is_hi.T
        mask[1, l] = same & ~is_hi & is_hi.T
        lower = 0.0 if s == 1 else -1.0
        sign[0, l] = np.where(hi, 1.0, lower)
        sign[1, l] = np.where(hi, lower, 1.0)
    mask[0, HGRN_LEVELS] = np.eye(c)
    sign = np.broadcast_to((sign * math.log2(math.e))[..., None], sign.shape + (HEAD,))
    return mask, np.ascontiguousarray(sign)


def _split_delta(cum, log_f, l, reverse):
    c, w = cum.shape
    s = 1 << l
    if s == 1:
        return log_f
    edge = s if reverse else s - 1
    if 2 * s >= 8:
        blocks = cum.reshape(c // (2 * s), 2 * s, w)
        at_split = jnp.broadcast_to(blocks[:, edge:edge + 1, :], blocks.shape).reshape(c, w)
    else:
        tiles = cum.reshape(c // 8, 8, w)
        sub = _iota2(tiles.shape, 1)
        at_split = None
        for blk in range(8 // (2 * s)):
            r = blk * 2 * s + edge
            row = jnp.broadcast_to(tiles[:, r:r + 1, :], tiles.shape)
            at_split = row if at_split is None else jnp.where(sub >= blk * 2 * s, row, at_split)
        at_split = at_split.reshape(c, w)
    return cum - at_split


def _hgrn_gates(z, lb):
    s = _sigmoid(z)
    log_f = jnp.log(lb + (1.0 - lb) * s)
    k = (1.0 - lb) * (1.0 - s)
    return k, log_f


def _hgrn_body(x_ref, wzv_ref, wq_ref, lb_ref, ng_ref, mask_ref, sign_ref, o_ref,
               sf_ref, sb_ref, sbs_ref, v_ref, *, nblk):
    c_sz = CHUNK
    nh = BRANCH // HEAD
    nl = HGRN_LEVELS

    def bwd_init():
        sb_ref[...] = jnp.zeros_like(sb_ref)

    def fwd_init():
        sf_ref[...] = jnp.zeros_like(sf_ref)

    def x_chunk(r0):
        return x_ref[0, pl.ds(r0, c_sz), :]

    def bwd_chunk(r0, ci, xb):
        zv = _dot(xb, wzv_ref[...])
        kb, lfb = _hgrn_gates(zv[:, :BRANCH], lb_ref[1:2, :])
        _, suf = _prefix_suffix_rows(lfb)
        total = suf[0:1, :]
        k_in = (kb * jnp.exp(total - suf)).astype(BF)
        dec = jnp.exp(total)
        v = zv[:, BRANCH:].astype(BF)
        v_ref[_rows(ci), :] = v
        for h in range(nh):
            sl = slice(h * HEAD, (h + 1) * HEAD)
            sbs_ref[ci, h] = sb_ref[h].astype(BF)
            sb_ref[h] = dec[:, sl] * sb_ref[h] + _dot_tn(v[:, sl], k_in[:, sl])

    def fwd_chunk(r0, ci, xb):
        rows = pl.ds(r0, c_sz)
        proj = _dot(xb, wq_ref[...])
        q = proj[:, :BRANCH]
        kf, lff = _hgrn_gates(proj[:, BRANCH:2 * BRANCH], lb_ref[0:1, :])
        kb, lfb = _hgrn_gates(proj[:, 2 * BRANCH:3 * BRANCH], lb_ref[1:2, :])
        g = proj[:, 3 * BRANCH:]
        v = v_ref[_rows(ci), :]

        scores = [None] * nh
        ksum = (kf + kb).astype(BF)
        qb = q.astype(BF)
        for h in range(nh):
            sl = slice(h * HEAD, (h + 1) * HEAD)
            scores[h] = _dot_nt(qb[:, sl], ksum[:, sl]).astype(BF) * mask_ref[0, nl]
        pre, _ = _prefix_suffix_rows(lff)
        _, suf = _prefix_suffix_rows(lfb)
        for dr, (kd, lfd, cum) in enumerate(((kf, lff, pre), (kb, lfb, suf))):
            for l in range(nl):
                sgn = jnp.concatenate([sign_ref[dr, l]] * nh, axis=1)
                e = jnp.exp2(_split_delta(cum, lfd, l, dr == 1) * sgn)
                u = (jnp.where(sgn > 0.0, q, kd) * e).astype(BF)
                for h in range(nh):
                    sl = slice(h * HEAD, (h + 1) * HEAD)
                    scores[h] = scores[h] + _dot_nt(u[:, sl], u[:, sl]).astype(BF) * mask_ref[dr, l]

        q_f = (q * jnp.exp(pre)).astype(BF)
        q_b = (q * jnp.exp(suf)).astype(BF)
        total = pre[c_sz - 1:c_sz, :]
        k_in = (kf * jnp.exp(total - pre)).astype(BF)
        dec = jnp.exp(total)
        for h in range(nh):
            sl = slice(h * HEAD, (h + 1) * HEAD)
            o = _dot(scores[h], v[:, sl])
            q_cat = jnp.concatenate([q_f[:, sl], q_b[:, sl]], axis=1)
            s_cat = jnp.concatenate([sf_ref[h].astype(BF), sbs_ref[ci, h]], axis=1)
            o = o + _dot_nt(q_cat, s_cat)
            sf_ref[h] = dec[:, sl] * sf_ref[h] + _dot_tn(v[:, sl], k_in[:, sl])
            y = _group_norm(o, False) * ng_ref[:, sl] * _silu(g[:, sl])
            o_ref[0, rows, sl] = y.astype(o_ref.dtype)

    _sweep(nblk, x_chunk, bwd_chunk, x_chunk, fwd_chunk, bwd_init, fwd_init, lookahead=False)


def _hgrn2(x, w_zv, w_q, lb, norm_g):
    b, t, d = x.shape
    nblk = t // TBLK
    nc = t // CHUNK
    nh = BRANCH // HEAD
    mask, sign = _hgrn_masks()
    mask = jnp.asarray(mask, BF)
    sign = jnp.asarray(sign, F32)
    in_specs = [
        _x_spec(nblk, d),
        _full_spec(w_zv.shape),
        _full_spec(w_q.shape),
        _full_spec((2, BRANCH)),
        _full_spec((1, BRANCH)),
        _full_spec(mask.shape),
        _full_spec(sign.shape),
    ]
    scratch = [
        pltpu.VMEM((nh, HEAD, HEAD), F32),
        pltpu.VMEM((nh, HEAD, HEAD), F32),
        pltpu.VMEM((nc, nh, HEAD, HEAD), BF),
        pltpu.VMEM((t, BRANCH), BF),
    ]
    return _mixer_call(_hgrn_body, b, t, (x, w_zv, w_q, lb, norm_g.reshape(1, BRANCH), mask, sign),
                       in_specs, scratch, "hgrn2")


ML_HEADS = 4
ML_AUG = 2 * HEAD


def _cummax_rows(a, reverse):
    n = a.shape[0]
    row = _iota2(a.shape, 0)
    k = 1
    while k < n:
        if reverse:
            shifted = pltpu.roll(a, n - k, axis=0)
            a = jnp.where(row < n - k, jnp.maximum(a, shifted), a)
        else:
            shifted = pltpu.roll(a, k, axis=0)
            a = jnp.where(row >= k, jnp.maximum(a, shifted), a)
        k *= 2
    return a


def _ml_body(x_ref, wkv_ref, wq_ref, wgc_ref, wgr_ref, bc_ref, br_ref, ng_ref, o_ref,
             cf_ref, cb_ref, cbs_ref, m_ref, mbs_ref, kv_ref, *, nblk):
    c_sz = CHUNK
    nh = ML_HEADS

    def gates(xb):
        is_fwd = _iota2((c_sz, HEAD), 1) < nh
        gc = _dot(xb, wgc_ref[...]) + bc_ref[...]
        ipre = gc[:, :HEAD]
        pre, suf = _prefix_suffix_rows(_log_sigmoid(gc[:, HEAD:]))
        cum = jnp.where(is_fwd, pre, suf)
        a = ipre - cum
        run_max = jnp.where(is_fwd, _cummax_rows(a, False), _cummax_rows(a, True))
        return a, cum, run_max

    def edge(arr):
        return jnp.where(_iota2((1, HEAD), 1) < nh, arr[c_sz - 1:c_sz, :], arr[0:1, :])

    def ones_col():
        return (_iota2((c_sz, HEAD), 1) == 0).astype(BF)

    def update_state(state_ref, lane0, k_all, v_aug, wk, sc):
        for h in range(nh):
            l = lane0 + h
            kw = (k_all[:, h * HEAD:(h + 1) * HEAD] * wk[:, l:l + 1]).astype(BF)
            state_ref[h] = sc[:, l:l + 1] * state_ref[h] + _dot_tn(kw, v_aug[h])

    def bwd_init():
        cb_ref[...] = jnp.zeros_like(cb_ref)
        m_ref[...] = jnp.zeros_like(m_ref)

    def fwd_init():
        cf_ref[...] = jnp.zeros_like(cf_ref)
        m_ref[...] = jnp.zeros_like(m_ref)

    def x_chunk(r0):
        return x_ref[0, pl.ds(r0, c_sz), :]

    def bwd_chunk(r0, ci, xb):
        a, cum, run_max = gates(xb)
        m_prev = m_ref[...]
        mbs_ref[ci] = m_prev
        for h in range(nh):
            cbs_ref[ci, h] = cb_ref[h].astype(BF)
        mu_e = edge(jnp.maximum(m_prev, run_max))
        kv = _dot(xb, wkv_ref[...])
        kv_ref[_rows(ci), :] = kv.astype(BF)
        v_aug = [jnp.concatenate([kv[:, BRANCH + h * HEAD:BRANCH + (h + 1) * HEAD].astype(BF), ones_col()],
                                 axis=1) for h in range(nh)]
        update_state(cb_ref, nh, kv[:, :BRANCH], v_aug, jnp.exp(a - mu_e), jnp.exp(m_prev - mu_e))
        m_ref[...] = edge(cum) + mu_e

    def fwd_chunk(r0, ci, xb):
        rows = pl.ds(r0, c_sz)
        a, cum, run_max = gates(xb)
        lower, upper = _tri_masks(c_sz)
        m_prev = jnp.where(_iota2((1, HEAD), 1) < nh, m_ref[...], mbs_ref[ci])
        mu = jnp.maximum(m_prev, run_max)
        s_inter = jnp.exp(m_prev - mu)
        thr = jnp.exp(-(cum + mu))

        gr = _dot_nt(wgr_ref[...], xb) + br_ref[...]
        ipre_r = gr[:2 * nh]
        pre_r, suf_r = _prefix_suffix_lanes(_log_sigmoid(gr[2 * nh:]))
        a_r = ipre_r - jnp.where(_iota2((2 * nh, c_sz), 0) < nh, pre_r, suf_r)

        proj = _dot(xb, wq_ref[...])
        kvc = kv_ref[_rows(ci), :]
        v_aug = [jnp.concatenate([kvc[:, BRANCH + h * HEAD:BRANCH + (h + 1) * HEAD], ones_col()], axis=1)
                 for h in range(nh)]
        heads = range(nh)
        qs = [proj[:, h * HEAD:(h + 1) * HEAD].astype(BF) for h in heads]
        qk = [_dot_nt(qs[h], kvc[:, h * HEAD:(h + 1) * HEAD]) for h in heads]
        rs = [_dot(qs[h], jnp.concatenate([cf_ref[h].astype(BF), cbs_ref[ci, h]], axis=1)) for h in heads]
        pairs = [(dr, h) for h in heads for dr in range(2)]
        wgt = {}
        for dr, h in pairs:
            l = dr * nh + h
            wgt[dr, h] = jnp.where(lower if dr == 0 else upper, jnp.exp(a_r[l:l + 1, :] - mu[:, l:l + 1]), 0.0)
        pv = {p: _dot((qk[p[1]] * wgt[p]).astype(BF), v_aug[p[1]]) for p in pairs}
        outs = {}
        for dr, h in pairs:
            l = dr * nh + h
            tot = s_inter[:, l:l + 1] * rs[h][:, dr * ML_AUG:(dr + 1) * ML_AUG] + pv[dr, h]
            den = jnp.maximum(jnp.abs(tot[:, HEAD:HEAD + 1]), thr[:, l:l + 1])
            outs[dr, h] = tot[:, :HEAD] / den
        for h in heads:
            sl = slice(h * HEAD, (h + 1) * HEAD)
            o_gate = proj[:, BRANCH + h * HEAD:BRANCH + (h + 1) * HEAD]
            g = proj[:, 2 * BRANCH + h * HEAD:2 * BRANCH + (h + 1) * HEAD]
            y = _group_norm(_sigmoid(o_gate) * (outs[0, h] + outs[1, h]), True) * ng_ref[:, sl] * _silu(g)
            o_ref[0, rows, sl] = y.astype(o_ref.dtype)

        mu_e = edge(mu)
        update_state(cf_ref, 0, kvc[:, :BRANCH].astype(F32), v_aug, jnp.exp(a - mu_e), jnp.exp(m_prev - mu_e))
        m_ref[...] = edge(cum) + mu_e

    _sweep(nblk, x_chunk, bwd_chunk, x_chunk, fwd_chunk, bwd_init, fwd_init, lookahead=False)


def _mlstm(x, w_kv, w_q, w_gates, i_bias, f_bias, norm_g):
    b, t, d = x.shape
    nblk = t // TBLK
    nc = t // CHUNK
    nh = ML_HEADS
    pad = HEAD - 2 * nh
    w_i = w_gates[:, :2 * nh]
    w_f = w_gates[:, 2 * nh:]
    wgc = jnp.concatenate([jnp.pad(w_i, ((0, 0), (0, pad))), jnp.pad(w_f, ((0, 0), (0, pad)))], axis=1)
    wgr = w_gates.T
    bias = jnp.concatenate([i_bias.reshape(-1), f_bias.reshape(-1)]).astype(F32)
    bc = jnp.concatenate([jnp.pad(bias[:2 * nh], (0, pad)), jnp.pad(bias[2 * nh:], (0, pad))]).reshape(1, 2 * HEAD)
    br = bias.reshape(4 * nh, 1)
    in_specs = [
        _x_spec(nblk, d),
        _full_spec(w_kv.shape),
        _full_spec(w_q.shape),
        _full_spec(wgc.shape),
        _full_spec(wgr.shape),
        _full_spec(bc.shape),
        _full_spec(br.shape),
        _full_spec((1, BRANCH)),
    ]
    scratch = [
        pltpu.VMEM((nh, HEAD, ML_AUG), F32),
        pltpu.VMEM((nh, HEAD, ML_AUG), F32),
        pltpu.VMEM((nc, nh, HEAD, ML_AUG), BF),
        pltpu.VMEM((1, HEAD), F32),
        pltpu.VMEM((nc, 1, HEAD), F32),
        pltpu.VMEM((t, 2 * BRANCH), BF),
    ]
    return _mixer_call(_ml_body, b, t,
                       (x, w_kv, w_q, wgc.astype(BF), wgr.astype(BF), bc, br, norm_g.reshape(1, BRANCH)),
                       in_specs, scratch, "mlstm")


SSD_XBC = BRANCH + 2 * SSD_GROUPS * HEAD
SSD_GW = BRANCH // SSD_GROUPS


def _ssd_body(x_ref, xp_ref, xn_ref, wz_ref, wx_ref, wdc_ref, wdr_ref, dbc_ref, dbr_ref,
              ac_ref, ar_ref, cw_ref, cb_ref, dsk_ref, ng_ref, ex_ref, o_ref,
              xe_ref, sf_ref, sb_ref, sbs_ref, act_ref, *, nblk):
    c_sz = CHUNK
    nh = SSD_HEADS
    per = TBLK // CHUNK

    def dt_columns(xb):
        is_fwd = _iota2((c_sz, HEAD), 1) < nh
        dt = _softplus(_dot(xb, wdc_ref[...]) + dbc_ref[...])
        pre, suf = _prefix_suffix_rows(dt * ac_ref[...])
        return dt, jnp.where(is_fwd, pre, suf)

    def expand(a, dr):
        hi, mid, _ = _split3(a)
        e = ex_ref[:, dr * BRANCH:(dr + 1) * BRANCH]
        return _dot(hi, e) + _dot(mid, e)

    def update_state(state_ref, dr, xs, bm, dt, cum):
        cum_e = jnp.where(_iota2((1, HEAD), 1) < nh, cum[c_sz - 1:c_sz, :], cum[0:1, :])
        w_state = jnp.exp(cum_e - cum) * dt
        dec = jnp.exp(jnp.broadcast_to(cum_e, (8, HEAD)))
        xw = (xs * expand(w_state, dr)).astype(BF)
        d512 = expand(dec, dr)[0:1, :]
        for g in range(SSD_GROUPS):
            gs = slice(g * SSD_GW, (g + 1) * SSD_GW)
            state_ref[g] = d512[:, gs] * state_ref[g] + _dot_tn(bm[:, g * HEAD:(g + 1) * HEAD], xw[:, gs])

    def bwd_init():
        sb_ref[...] = jnp.zeros_like(sb_ref)

    def fwd_init():
        sf_ref[...] = jnp.zeros_like(sf_ref)

    def project_block():
        blk = nblk - 1 - pl.program_id(2)

        for j in range(per):
            xe_ref[pl.ds(HALO + j * c_sz, c_sz), :] = _dot(x_ref[0, pl.ds(j * c_sz, c_sz), :], wx_ref[...])
        halo = jnp.concatenate([xp_ref[0, 0], xn_ref[0, 0]], axis=0)
        ph = _dot(halo, wx_ref[...])
        xe_ref[0:HALO, :] = ph[:HALO] * jnp.where(blk > 0, 1.0, 0.0)
        xe_ref[HALO + TBLK:, :] = ph[HALO:] * jnp.where(blk < nblk - 1, 1.0, 0.0)

    def x_chunk(r0):
        return x_ref[0, pl.ds(r0, c_sz), :]

    def bwd_chunk(r0, ci, xb):
        half = (SSD_CONV - 1) // 2
        win = xe_ref[pl.ds(r0, c_sz + 2 * HALO), :]
        conv = cb_ref[...]
        for k in range(SSD_CONV):
            conv = conv + cw_ref[k:k + 1, :] * win[HALO - half + k:HALO - half + k + c_sz, :]
        act = _silu(conv)
        act_ref[_rows(ci), :] = act.astype(BF)
        dt, cum = dt_columns(xb)
        for g in range(SSD_GROUPS):
            sbs_ref[ci, g] = sb_ref[g].astype(BF)
        update_state(sb_ref, 1, act[:, :BRANCH], act[:, BRANCH:BRANCH + SSD_GROUPS * HEAD].astype(BF), dt, cum)

    def fwd_chunk(r0, ci, xb):
        rows = pl.ds(r0, c_sz)
        act = act_ref[_rows(ci), :]
        xs = act[:, :BRANCH].astype(F32)
        bm = act[:, BRANCH:BRANCH + SSD_GROUPS * HEAD]
        cm = act[:, BRANCH + SSD_GROUPS * HEAD:]
        dt, cum = dt_columns(xb)
        dt_r = _softplus(_dot_nt(wdr_ref[...], xb) + dbr_ref[...])
        pre_r, suf_r = _prefix_suffix_lanes(dt_r * ar_ref[...])
        cum_r = jnp.where(_iota2((2 * nh, c_sz), 0) < nh, pre_r, suf_r)
        shifted_r = cum_r - jnp.log(dt_r)
        below = _iota2((c_sz, c_sz), 1) < _iota2((c_sz, c_sz), 0)
        diag = _iota2((c_sz, c_sz), 1) == _iota2((c_sz, c_sz), 0)
        z = _dot(xb, wz_ref[...])
        lane_half = _iota2((c_sz, HEAD), 1) < SSD_HEADDIM
        ecum = jnp.exp(cum)
        ecum_f = expand(ecum, 0)
        ecum_b = expand(ecum, 1)
        hpg = SSD_HEADS // SSD_GROUPS
        groups = range(SSD_GROUPS)
        heads = range(nh)
        cg = [cm[:, g * HEAD:(g + 1) * HEAD] for g in groups]
        gmat = [_dot_nt(cg[g], bm[:, g * HEAD:(g + 1) * HEAD]) for g in groups]
        from_f = [_dot(cg[g], sf_ref[g].astype(BF)) for g in groups]
        from_b = [_dot(cg[g], sbs_ref[ci, g]) for g in groups]
        m = [jnp.exp(jnp.where(below, cum[:, h:h + 1] - shifted_r[h:h + 1, :],
                               cum[:, nh + h:nh + h + 1] - shifted_r[nh + h:nh + h + 1, :])) +
             jnp.where(diag, dt_r[h:h + 1, :], 0.0) for h in heads]
        xh = [jnp.where(lane_half if h % 2 == 0 else jnp.logical_not(lane_half),
                        xs[:, (h // 2) * HEAD:(h // 2 + 1) * HEAD], 0.0).astype(BF) for h in heads]
        part = [_dot((gmat[h // hpg] * m[h]).astype(BF), xh[h]) for h in heads]
        ys = []
        for g in groups:
            gs = slice(g * SSD_GW, (g + 1) * SSD_GW)
            pieces = [part[g * hpg + 2 * pair] + part[g * hpg + 2 * pair + 1] for pair in range(hpg // 2)]
            ys.append(jnp.concatenate(pieces, axis=1) + ecum_f[:, gs] * from_f[g] + ecum_b[:, gs] * from_b[g])
        y = jnp.concatenate(ys, axis=1) + dsk_ref[...] * xs
        y = y * _silu(z)
        for g in range(SSD_GROUPS):
            gs = slice(g * SSD_GW, (g + 1) * SSD_GW)
            o_ref[0, rows, gs] = (_group_norm(y[:, gs], False) * ng_ref[:, gs]).astype(o_ref.dtype)
        update_state(sf_ref, 0, xs, bm, dt, cum)

    pl.when(pl.program_id(1) == 0)(project_block)
    _sweep(nblk, x_chunk, bwd_chunk, x_chunk, fwd_chunk, bwd_init, fwd_init, lookahead=False)


def _ssd(x, w_z, w_xbc, w_dt, conv_w, conv_b, a_log, dt_bias, d_skip, norm_g):
    b, t, d = x.shape
    nblk = t // TBLK
    nc = t // CHUNK
    nh = SSD_HEADS
    pad = HEAD - 2 * nh
    wdc = jnp.pad(w_dt, ((0, 0), (0, pad)))
    wdr = w_dt.T
    db = dt_bias.reshape(-1).astype(F32)
    a = (-jnp.exp(a_log.astype(F32))).reshape(-1)
    dbc = jnp.pad(db, (0, pad)).reshape(1, HEAD)
    ac = jnp.pad(a, (0, pad)).reshape(1, HEAD)
    dbr = db.reshape(2 * nh, 1)
    ar = a.reshape(2 * nh, 1)
    cw = jnp.pad(conv_w.astype(F32), ((0, 8 - SSD_CONV), (0, 0)))
    dsk = jnp.repeat(d_skip.astype(F32), SSD_HEADDIM).reshape(1, BRANCH)
    ex = np.zeros((HEAD, 2 * BRANCH), np.float32)
    for l in range(2 * nh):
        ex[l, l * SSD_HEADDIM:(l + 1) * SSD_HEADDIM] = 1.0
    ex = jnp.asarray(ex, BF)
    xh = x.reshape(b, t // HALO, HALO, d)
    per = TBLK // HALO
    last = t // HALO - 1
    in_specs = [
        _x_spec(nblk, d),
        pl.BlockSpec((1, 1, HALO, d),
                     lambda bi, ph, c: (bi, jnp.maximum(_blk_index(ph, c, nblk) * per - 1, 0), 0, 0)),
        pl.BlockSpec((1, 1, HALO, d),
                     lambda bi, ph, c: (bi, jnp.minimum((_blk_index(ph, c, nblk) + 1) * per, last), 0, 0)),
        _full_spec(w_z.shape),
        _full_spec(w_xbc.shape),
        _full_spec(wdc.shape),
        _full_spec(wdr.shape),
        _full_spec(dbc.shape),
        _full_spec(dbr.shape),
        _full_spec(ac.shape),
        _full_spec(ar.shape),
        _full_spec(cw.shape),
        _full_spec((1, SSD_XBC)),
        _full_spec((1, BRANCH)),
        _full_spec((1, BRANCH)),
        _full_spec(ex.shape),
    ]
    scratch = [
        pltpu.VMEM((TBLK + 2 * HALO, SSD_XBC), F32),
        pltpu.VMEM((SSD_GROUPS, HEAD, SSD_GW), F32),
        pltpu.VMEM((SSD_GROUPS, HEAD, SSD_GW), F32),
        pltpu.VMEM((nc, SSD_GROUPS, HEAD, SSD_GW), BF),
        pltpu.VMEM((t, SSD_XBC), BF),
    ]
    return _mixer_call(_ssd_body, b, t,
                       (x, xh, xh, w_z, w_xbc, wdc.astype(BF), wdr.astype(BF), dbc, dbr, ac, ar, cw,
                        conv_b.astype(F32).reshape(1, SSD_XBC), dsk, norm_g.reshape(1, BRANCH), ex),
                       in_specs, scratch, "ssd")


def _out_body(x_ref, p_ref, m0_ref, m1_ref, m2_ref, m3_ref, wo_ref, lg_ref, lb_ref, wg_ref, wp_ref,
              o_ref, ob_ref, *, alpha):
    step = OUT_ROWS // OUT_PARTS
    parts = [slice(i * step, (i + 1) * step) for i in range(OUT_PARTS)]

    def mix(rows):
        y = None
        for i, m_ref in enumerate((m0_ref, m1_ref, m2_ref, m3_ref)):
            part = _dot(m_ref[0, rows, :], wo_ref[i * BRANCH:(i + 1) * BRANCH, :])
            y = part if y is None else y + part
        return y

    def norm(rows, y):
        r = alpha * x_ref[0, rows, :] + y
        mu = jnp.mean(r, axis=-1, keepdims=True)
        rc = r - mu
        var = jnp.mean(rc * rc, axis=-1, keepdims=True)
        return rc * lax.rsqrt(var + NORM_EPS) * lg_ref[...] + lb_ref[...]

    def finish(rows, xn):
        gate = _sigmoid(_dot(xn.astype(BF), wg_ref[...]))
        out = xn + gate * _dot(p_ref[0, rows, :].astype(BF), wp_ref[...])
        o_ref[0, rows, :] = out
        ob_ref[0, rows, :] = out.astype(BF)

    y = mix(parts[0])
    xn = None
    for i in range(OUT_PARTS):
        y_next = mix(parts[i + 1]) if i + 1 < OUT_PARTS else None
        xn_i = norm(parts[i], y)
        if xn is not None:
            finish(parts[i - 1], xn)
        xn = xn_i
        y = y_next
    finish(parts[OUT_PARTS - 1], xn)


def _out_layer(x, p, mixed, w_out, ln_g, ln_b, w_gate, w_proj, alpha):
    b, t, d = x.shape
    dp = p.shape[-1]
    rows = OUT_ROWS
    row_spec = lambda width: pl.BlockSpec((1, rows, width), lambda bi, r: (bi, r, 0))
    full = lambda shape: pl.BlockSpec(shape, lambda bi, r: (0,) * len(shape))
    return pl.pallas_call(
        functools.partial(_out_body, alpha=alpha),
        grid=(b, t // rows),
        in_specs=[row_spec(d), row_spec(dp)] + [row_spec(BRANCH)] * 4 +
                 [full(w_out.shape), full((1, d)), full((1, d)), full(w_gate.shape), full(w_proj.shape)],
        out_specs=(row_spec(d), row_spec(d)),
        out_shape=(jax.ShapeDtypeStruct((b, t, d), F32), jax.ShapeDtypeStruct((b, t, d), BF)),
        compiler_params=pltpu.CompilerParams(
            dimension_semantics=("arbitrary", "arbitrary"), vmem_limit_bytes=VMEM_LIMIT),
        name="out_proj",
    )(x, p, *mixed, w_out, ln_g.reshape(1, d), ln_b.reshape(1, d), w_gate, w_proj)


def _rope_tables(t):
    inv = ROPE_BASE ** (-jnp.arange(0, HEAD, 2, dtype=F32) / HEAD)
    ang = jnp.arange(t, dtype=F32)[:, None] * inv[None, :]
    cos = jnp.cos(ang)
    sin = jnp.sin(ang)
    return jnp.concatenate([cos, cos], axis=1), jnp.concatenate([-sin, sin], axis=1)


def _layer_weights(l, w_in, w_out, hgrn_lb_logits, w_ple_gate, w_ple_proj):
    wl = w_in[l]
    nb = BRANCH
    o1 = 4 * nb
    o2 = o1 + 5 * nb
    o3 = o2 + 5 * nb + 4 * ML_HEADS

    def cols(*ranges):
        return jnp.concatenate([wl[:, a:b] for a, b in ranges], axis=1).astype(BF)

    ret_kv = cols((nb, 3 * nb))
    ret_qg = cols((0, nb), (3 * nb, 4 * nb))
    hg = o1
    hgrn_zv = cols((hg + 2 * nb, hg + 4 * nb))
    hgrn_q = cols((hg, hg + 3 * nb), (hg + 4 * nb, hg + 5 * nb))
    ml = o2
    ml_kv = jnp.concatenate([wl[:, ml + nb:ml + 2 * nb] * (HEAD ** -0.5), wl[:, ml + 2 * nb:ml + 3 * nb]],
                            axis=1).astype(BF)
    ml_q = cols((ml, ml + nb), (ml + 3 * nb, ml + 5 * nb))
    ml_gates = wl[:, ml + 5 * nb:o3]
    w_z = cols((o3, o3 + nb))
    w_xbc = cols((o3 + nb, o3 + nb + SSD_XBC))
    w_dt = wl[:, o3 + nb + SSD_XBC:]
    lb_w = jax.nn.softmax(hgrn_lb_logits.astype(F32), axis=0)
    lb = (jnp.cumsum(lb_w, axis=0) - lb_w[0:1])[l]
    return dict(ret_kv=ret_kv, ret_qg=ret_qg, hgrn_zv=hgrn_zv, hgrn_q=hgrn_q, ml_kv=ml_kv, ml_q=ml_q,
                ml_gates=ml_gates, w_z=w_z, w_xbc=w_xbc, w_dt=w_dt, lb=lb, w_out=w_out[l].astype(BF),
                w_gate=w_ple_gate[l].astype(BF), w_proj=w_ple_proj[l].astype(BF))


def kernel(x_prompt, x_sample, p_prompt, p_sample, w_in, w_out, ln_g, ln_b, ret_log_rate, ret_norm_g,
           hgrn_lb_logits, hgrn_norm_g, mlstm_i_bias, mlstm_f_bias, mlstm_norm_g, ssd_conv_w, ssd_conv_b,
           ssd_a_log, ssd_dt_bias, ssd_d, ssd_norm_g, w_ple_gate, w_ple_proj):
    depth = w_in.shape[0]
    alpha = (2 * depth) ** 0.25
    weights = [_layer_weights(l, w_in, w_out, hgrn_lb_logits, w_ple_gate, w_ple_proj) for l in range(depth)]

    def trunk(x, p):
        cos2, sin2 = _rope_tables(x.shape[1])
        xb = x.astype(BF)
        for l in range(depth):
            w = weights[l]
            mixed = (
                _retention(xb, w["ret_kv"], w["ret_qg"], ret_log_rate[l], ret_norm_g[l], cos2, sin2),
                _hgrn2(xb, w["hgrn_zv"], w["hgrn_q"], w["lb"], hgrn_norm_g[l]),
                _mlstm(xb, w["ml_kv"], w["ml_q"], w["ml_gates"], mlstm_i_bias[l], mlstm_f_bias[l],
                       mlstm_norm_g[l]),
                _ssd(xb, w["w_z"], w["w_xbc"], w["w_dt"], ssd_conv_w[l], ssd_conv_b[l], ssd_a_log[l],
                     ssd_dt_bias[l], ssd_d[l], ssd_norm_g[l]),
            )
            x, xb = _out_layer(x, p[l], mixed, w["w_out"], ln_g[l], ln_b[l], w["w_gate"], w["w_proj"], alpha)
        return x

    return trunk(x_prompt, p_prompt), trunk(x_sample, p_sample)
```

```python
import functools
import math

import numpy as np
import jax
import jax.numpy as jnp
from jax import lax
from jax.experimental import pallas as pl
from jax.experimental.pallas import tpu as pltpu

BF = jnp.bfloat16
F32 = jnp.float32

CHUNK = 128
TBLK = 1024
SUB = 512
HEAD = 128
BRANCH = 512
SSD_HEADDIM = 64
SSD_HEADS = 8
SSD_GROUPS = 2
SSD_CONV = 5
HALO = 16
ROPE_BASE = 10000.0
NORM_EPS = 1e-5
OUT_ROWS = 512
OUT_PARTS = 4
VMEM_LIMIT = 56 * 1024 * 1024


def _dot(a, b):
    return jnp.dot(a, b, preferred_element_type=F32)


def _dot_nt(a, b):
    return lax.dot_general(a, b, (((1,), (1,)), ((), ())), preferred_element_type=F32)


def _dot_tn(a, b):
    return lax.dot_general(a, b, (((0,), (0,)), ((), ())), preferred_element_type=F32)


def _split3(x):
    hi = x.astype(BF)
    r = x - hi.astype(F32)
    mid = r.astype(BF)
    lo = (r - mid.astype(F32)).astype(BF)
    return hi, mid, lo


def _sel_dot(sel, x):
    hi, mid, lo = _split3(x)
    return _dot(sel, hi) + _dot(sel, mid) + _dot(sel, lo)


def _dot_sel(x, sel):
    hi, mid, lo = _split3(x)
    return _dot(hi, sel) + _dot(mid, sel) + _dot(lo, sel)


def _sigmoid(x):
    return 1.0 / (1.0 + jnp.exp(-x))


def _silu(x):
    return x * _sigmoid(x)


def _softplus(x):
    return jnp.maximum(x, 0.0) + jnp.log1p(jnp.exp(-jnp.abs(x)))


def _log_sigmoid(x):
    return jnp.minimum(x, 0.0) - jnp.log1p(jnp.exp(-jnp.abs(x)))


def _iota2(shape, axis):
    return lax.broadcasted_iota(jnp.int32, shape, axis)


def _tri_masks(c):
    row = _iota2((c, c), 0)
    col = _iota2((c, c), 1)
    return col <= row, col >= row


def _prefix_suffix_rows(x):
    pre = _sel_dot(_tri_masks(x.shape[0])[0].astype(BF), x)
    return pre, pre[x.shape[0] - 1:, :] - pre + x


def _prefix_suffix_lanes(x):
    n = x.shape[1]
    pre = _dot_sel(x, _tri_masks(n)[1].astype(BF))
    return pre, pre[:, n - 1:] - pre + x


def _group_norm(o, center):
    if center:
        o = o - jnp.mean(o, axis=-1, keepdims=True)
    return o * lax.rsqrt(jnp.mean(o * o, axis=-1, keepdims=True) + NORM_EPS)


def _blk_index(phase, c, nblk):
    return jnp.where(phase == 0, nblk - 1 - c, c)


def _sweep(nblk, bwd_proj, bwd_chunk, fwd_proj, fwd_chunk, bwd_init, fwd_init, lookahead):
    phase = pl.program_id(1)
    c = pl.program_id(2)
    per = SUB // CHUNK
    nsub = TBLK // SUB

    def walk(order, base, first_chunk, proj, chunk):
        if not lookahead:
            for j in order:
                chunk(base + j * CHUNK, first_chunk + j, proj(base + j * CHUNK))
            return
        ahead = proj(base + order[0] * CHUNK)
        for n, j in enumerate(order):
            cur = ahead
            if n + 1 < per:
                ahead = proj(base + order[n + 1] * CHUNK)
            chunk(base + j * CHUNK, first_chunk + j, cur)

    @pl.when(phase == 0)
    def _():
        pl.when(c == 0)(bwd_init)
        blk = nblk - 1 - c

        def body(i, carry):
            sub = nsub - 1 - i
            walk(list(reversed(range(per))), pl.multiple_of(sub * SUB, SUB), (blk * nsub + sub) * per,
                 bwd_proj, bwd_chunk)
            return carry

        lax.fori_loop(0, nsub, body, 0)

    @pl.when(phase == 1)
    def _():
        pl.when(c == 0)(fwd_init)

        def body(sub, carry):
            walk(list(range(per)), pl.multiple_of(sub * SUB, SUB), (c * nsub + sub) * per, fwd_proj, fwd_chunk)
            return carry

        lax.fori_loop(0, nsub, body, 0)


def _mixer_call(body, b, t, in_arrays, in_specs, scratch, name):
    nblk = t // TBLK
    return pl.pallas_call(
        functools.partial(body, nblk=nblk),
        grid=(b, 2, nblk),
        in_specs=in_specs,
        out_specs=pl.BlockSpec((1, TBLK, BRANCH),
                               lambda bi, ph, c: (bi, jnp.where(ph == 0, 0, c), 0)),
        out_shape=jax.ShapeDtypeStruct((b, t, BRANCH), BF),
        scratch_shapes=scratch,
        compiler_params=pltpu.CompilerParams(
            dimension_semantics=("arbitrary", "arbitrary", "arbitrary"),
            vmem_limit_bytes=VMEM_LIMIT),
        name=name,
    )(*in_arrays)


def _x_spec(nblk, d):
    return pl.BlockSpec((1, TBLK, d), lambda bi, ph, c: (bi, _blk_index(ph, c, nblk), 0))


def _full_spec(shape):
    zeros = (0,) * len(shape)
    return pl.BlockSpec(shape, lambda bi, ph, c: zeros)


def _rows(ci):
    return pl.ds(pl.multiple_of(ci * CHUNK, CHUNK), CHUNK)


def _rope(x, cos2, sin2):
    return x * cos2 + pltpu.roll(x, HEAD // 2, axis=1) * sin2


def _ret_body(lg_ref, x_ref, cos_ref, sin_ref, wkv_ref, wqg_ref, ng_ref, o_ref,
              sf_ref, sb_ref, sbs_ref, m_ref, kv_ref, *, nblk):
    c_sz = CHUNK
    nh = BRANCH // HEAD
    scale = HEAD ** -0.5

    def bwd_init():
        sb_ref[...] = jnp.zeros_like(sb_ref)

    def fwd_init():
        sf_ref[...] = jnp.zeros_like(sf_ref)
        lower, upper = _tri_masks(c_sz)
        dist = (_iota2((c_sz, c_sz), 0) - _iota2((c_sz, c_sz), 1)).astype(F32)
        for h in range(nh):
            fwd = jnp.where(lower, jnp.exp(lg_ref[0, h] * dist), 0.0)
            bwd = jnp.where(upper, jnp.exp(-lg_ref[1, h] * dist), 0.0)
            m_ref[h] = (fwd + bwd) * scale

    def bwd_proj(r0):
        return _dot(x_ref[0, pl.ds(r0, c_sz), :], wkv_ref[...])

    def fwd_proj(r0):
        return _dot(x_ref[0, pl.ds(r0, c_sz), :], wqg_ref[...])

    def bwd_chunk(r0, ci, kv):
        rows = pl.ds(r0, c_sz)
        pos = _iota2((c_sz, HEAD), 0).astype(F32)
        span = jnp.full((1, HEAD), float(c_sz), F32)
        cos2 = cos_ref[rows, :]
        sin2 = sin_ref[rows, :]
        heads = range(nh)
        k = [_rope(kv[:, h * HEAD:(h + 1) * HEAD], cos2, sin2) for h in heads]
        v = [kv[:, BRANCH + h * HEAD:BRANCH + (h + 1) * HEAD].astype(BF) for h in heads]
        for h in heads:
            kv_ref[_rows(ci), h * HEAD:(h + 1) * HEAD] = k[h].astype(BF)
            kv_ref[_rows(ci), BRANCH + h * HEAD:BRANCH + (h + 1) * HEAD] = v[h]
            sbs_ref[ci, h] = sb_ref[h].astype(BF)
        k_in = [(k[h] * (jnp.exp(lg_ref[1, h] * pos) * scale)).astype(BF) for h in heads]
        for h in heads:
            sb_ref[h] = jnp.exp(lg_ref[1, h] * span) * sb_ref[h] + _dot_tn(k_in[h], v[h])

    def fwd_chunk(r0, ci, qg):
        rows = pl.ds(r0, c_sz)
        pos = _iota2((c_sz, HEAD), 0).astype(F32)
        span = jnp.full((1, HEAD), float(c_sz), F32)
        cos2 = cos_ref[rows, :]
        sin2 = sin_ref[rows, :]
        heads = range(nh)
        hs = [slice(h * HEAD, (h + 1) * HEAD) for h in heads]
        q = [_rope(qg[:, hs[h]], cos2, sin2).astype(BF) for h in heads]
        k = [kv_ref[_rows(ci), hs[h]] for h in heads]
        v = [kv_ref[_rows(ci), BRANCH + h * HEAD:BRANCH + (h + 1) * HEAD] for h in heads]
        p = [(_dot_nt(q[h], k[h]) * m_ref[h]).astype(BF) for h in heads]
        r = [_dot(q[h], jnp.concatenate([sf_ref[h].astype(BF), sbs_ref[ci, h]], axis=1)) for h in heads]
        o = [_dot(p[h], v[h]) for h in heads]
        o = [o[h] + jnp.exp(lg_ref[0, h] * (pos + 1.0)) * r[h][:, :HEAD] +
             jnp.exp(lg_ref[1, h] * (c_sz - pos)) * r[h][:, HEAD:] for h in heads]
        k_in = [(k[h].astype(F32) * (jnp.exp(lg_ref[0, h] * (c_sz - 1.0 - pos)) * scale)).astype(BF) for h in heads]
        for h in heads:
            sf_ref[h] = jnp.exp(lg_ref[0, h] * span) * sf_ref[h] + _dot_tn(k_in[h], v[h])
        for h in heads:
            y = _group_norm(o[h], True) * ng_ref[:, hs[h]] * _silu(qg[:, BRANCH + h * HEAD:BRANCH + (h + 1) * HEAD])
            o_ref[0, rows, hs[h]] = y.astype(o_ref.dtype)

    _sweep(nblk, bwd_proj, bwd_chunk, fwd_proj, fwd_chunk, bwd_init, fwd_init, lookahead=True)


def _retention(x, w_kv, w_qg, log_rate, norm_g, cos2, sin2):
    b, t, d = x.shape
    nblk = t // TBLK
    nc = t // CHUNK
    nh = BRANCH // HEAD
    lg = -jnp.exp(log_rate.astype(F32))
    in_specs = [
        pl.BlockSpec(memory_space=pltpu.SMEM),
        _x_spec(nblk, d),
        pl.BlockSpec((TBLK, HEAD), lambda bi, ph, c: (_blk_index(ph, c, nblk), 0)),
        pl.BlockSpec((TBLK, HEAD), lambda bi, ph, c: (_blk_index(ph, c, nblk), 0)),
        _full_spec(w_kv.shape),
        _full_spec(w_qg.shape),
        _full_spec((1, BRANCH)),
    ]
    scratch = [
        pltpu.VMEM((nh, HEAD, HEAD), F32),
        pltpu.VMEM((nh, HEAD, HEAD), F32),
        pltpu.VMEM((nc, nh, HEAD, HEAD), BF),
        pltpu.VMEM((nh, CHUNK, CHUNK), F32),
        pltpu.VMEM((t, 2 * BRANCH), BF),
    ]
    return _mixer_call(_ret_body, b, t, (lg, x, cos2, sin2, w_kv, w_qg, norm_g.reshape(1, BRANCH)),
                       in_specs, scratch, "retention")


HGRN_LEVELS = int(math.log2(CHUNK))


def _hgrn_masks():
    c = CHUNK
    t = np.arange(c)
    mask = np.zeros((2, HGRN_LEVELS + 1, c, c), np.float32)
    sign = np.zeros((2, HGRN_LEVELS, c), np.float32)
    for l in range(HGRN_LEVELS):
        s = 1 << l
        hi = (t % (2 * s)) >= s
        is_hi = hi[:, None]
        same = (t[:, None] // (2 * s)) == (t[None, :] // (2 * s))
        mask[0, l] = same & is_hi & ~is_hi.T
        mask[1, l] = same & ~is_hi & is_hi.T
        lower = 0.0 if s == 1 else -1.0
        sign[0, l] = np.where(hi, 1.0, lower)
        sign[1, l] = np.where(hi, lower, 1.0)
    mask[0, HGRN_LEVELS] = np.eye(c)
    sign = np.broadcast_to((sign * math.log2(math.e))[..., None], sign.shape + (HEAD,))
    return mask, np.ascontiguousarray(sign)


def _split_delta(cum, log_f, l, reverse):
    c, w = cum.shape
    s = 1 << l
    if s == 1:
        return log_f
    edge = s if reverse else s - 1
    if 2 * s >= 8:
        blocks = cum.reshape(c // (2 * s), 2 * s, w)
        at_split = jnp.broadcast_to(blocks[:, edge:edge + 1, :], blocks.shape).reshape(c, w)
    else:
        tiles = cum.reshape(c // 8, 8, w)
        sub = _iota2(tiles.shape, 1)
        at_split = None
        for blk in range(8 // (2 * s)):
            r = blk * 2 * s + edge
            row = jnp.broadcast_to(tiles[:, r:r + 1, :], tiles.shape)
            at_split = row if at_split is None else jnp.where(sub >= blk * 2 * s, row, at_split)
        at_split = at_split.reshape(c, w)
    return cum - at_split


def _hgrn_gates(z, lb):
    s = _sigmoid(z)
    log_f = jnp.log(lb + (1.0 - lb) * s)
    k = (1.0 - lb) * (1.0 - s)
    return k, log_f


def _hgrn_body(x_ref, wzv_ref, wq_ref, lb_ref, ng_ref, mask_ref, sign_ref, o_ref,
               sf_ref, sb_ref, sbs_ref, v_ref, *, nblk):
    c_sz = CHUNK
    nh = BRANCH // HEAD
    nl = HGRN_LEVELS

    def bwd_init():
        sb_ref[...] = jnp.zeros_like(sb_ref)

    def fwd_init():
        sf_ref[...] = jnp.zeros_like(sf_ref)

    def x_chunk(r0):
        return x_ref[0, pl.ds(r0, c_sz), :]

    def bwd_chunk(r0, ci, xb):
        zv = _dot(xb, wzv_ref[...])
        kb, lfb = _hgrn_gates(zv[:, :BRANCH], lb_ref[1:2, :])
        _, suf = _prefix_suffix_rows(lfb)
        total = suf[0:1, :]
        k_in = (kb * jnp.exp(total - suf)).astype(BF)
        dec = jnp.exp(total)
        v = zv[:, BRANCH:].astype(BF)
        v_ref[_rows(ci), :] = v
        for h in range(nh):
            sl = slice(h * HEAD, (h + 1) * HEAD)
            sbs_ref[ci, h] = sb_ref[h].astype(BF)
            sb_ref[h] = dec[:, sl] * sb_ref[h] + _dot_tn(v[:, sl], k_in[:, sl])

    def fwd_chunk(r0, ci, xb):
        rows = pl.ds(r0, c_sz)
        proj = _dot(xb, wq_ref[...])
        q = proj[:, :BRANCH]
        kf, lff = _hgrn_gates(proj[:, BRANCH:2 * BRANCH], lb_ref[0:1, :])
        kb, lfb = _hgrn_gates(proj[:, 2 * BRANCH:3 * BRANCH], lb_ref[1:2, :])
        g = proj[:, 3 * BRANCH:]
        v = v_ref[_rows(ci), :]

        scores = [None] * nh
        ksum = (kf + kb).astype(BF)
        qb = q.astype(BF)
        for h in range(nh):
            sl = slice(h * HEAD, (h + 1) * HEAD)
            scores[h] = _dot_nt(qb[:, sl], ksum[:, sl]).astype(BF) * mask_ref[0, nl]
        pre, _ = _prefix_suffix_rows(lff)
        _, suf = _prefix_suffix_rows(lfb)
        for dr, (kd, lfd, cum) in enumerate(((kf, lff, pre), (kb, lfb, suf))):
            for l in range(nl):
                sgn = jnp.concatenate([sign_ref[dr, l]] * nh, axis=1)
                e = jnp.exp2(_split_delta(cum, lfd, l, dr == 1) * sgn)
                u = (jnp.where(sgn > 0.0, q, kd) * e).astype(BF)
                for h in range(nh):
                    sl = slice(h * HEAD, (h + 1) * HEAD)
                    scores[h] = scores[h] + _dot_nt(u[:, sl], u[:, sl]).astype(BF) * mask_ref[dr, l]

        q_f = (q * jnp.exp(pre)).astype(BF)
        q_b = (q * jnp.exp(suf)).astype(BF)
        total = pre[c_sz - 1:c_sz, :]
        k_in = (kf * jnp.exp(total - pre)).astype(BF)
        dec = jnp.exp(total)
        for h in range(nh):
            sl = slice(h * HEAD, (h + 1) * HEAD)
            o = _dot(scores[h], v[:, sl])
            q_cat = jnp.concatenate([q_f[:, sl], q_b[:, sl]], axis=1)
            s_cat = jnp.concatenate([sf_ref[h].astype(BF), sbs_ref[ci, h]], axis=1)
            o = o + _dot_nt(q_cat, s_cat)
            sf_ref[h] = dec[:, sl] * sf_ref[h] + _dot_tn(v[:, sl], k_in[:, sl])
            y = _group_norm(o, False) * ng_ref[:, sl] * _silu(g[:, sl])
            o_ref[0, rows, sl] = y.astype(o_ref.dtype)

    _sweep(nblk, x_chunk, bwd_chunk, x_chunk, fwd_chunk, bwd_init, fwd_init, lookahead=False)


def _hgrn2(x, w_zv, w_q, lb, norm_g):
    b, t, d = x.shape
    nblk = t // TBLK
    nc = t // CHUNK
    nh = BRANCH // HEAD
    mask, sign = _hgrn_masks()
    mask = jnp.asarray(mask, BF)
    sign = jnp.asarray(sign, F32)
    in_specs = [
        _x_spec(nblk, d),
        _full_spec(w_zv.shape),
        _full_spec(w_q.shape),
        _full_spec((2, BRANCH)),
        _full_spec((1, BRANCH)),
        _full_spec(mask.shape),
        _full_spec(sign.shape),
    ]
    scratch = [
        pltpu.VMEM((nh, HEAD, HEAD), F32),
        pltpu.VMEM((nh, HEAD, HEAD), F32),
        pltpu.VMEM((nc, nh, HEAD, HEAD), BF),
        pltpu.VMEM((t, BRANCH), BF),
    ]
    return _mixer_call(_hgrn_body, b, t, (x, w_zv, w_q, lb, norm_g.reshape(1, BRANCH), mask, sign),
                       in_specs, scratch, "hgrn2")


ML_HEADS = 4
ML_AUG = 2 * HEAD


def _cummax_rows(a, reverse):
    n = a.shape[0]
    row = _iota2(a.shape, 0)
    k = 1
    while k < n:
        if reverse:
            shifted = pltpu.roll(a, n - k, axis=0)
            a = jnp.where(row < n - k, jnp.maximum(a, shifted), a)
        else:
            shifted = pltpu.roll(a, k, axis=0)
            a = jnp.where(row >= k, jnp.maximum(a, shifted), a)
        k *= 2
    return a


def _ml_body(x_ref, wkv_ref, wq_ref, wgc_ref, wgr_ref, bc_ref, br_ref, ng_ref, o_ref,
             cf_ref, cb_ref, cbs_ref, m_ref, mbs_ref, kv_ref, *, nblk):
    c_sz = CHUNK
    nh = ML_HEADS

    def gates(xb):
        is_fwd = _iota2((c_sz, HEAD), 1) < nh
        gc = _dot(xb, wgc_ref[...]) + bc_ref[...]
        ipre = gc[:, :HEAD]
        pre, suf = _prefix_suffix_rows(_log_sigmoid(gc[:, HEAD:]))
        cum = jnp.where(is_fwd, pre, suf)
        a = ipre - cum
        run_max = jnp.where(is_fwd, _cummax_rows(a, False), _cummax_rows(a, True))
        return a, cum, run_max

    def edge(arr):
        return jnp.where(_iota2((1, HEAD), 1) < nh, arr[c_sz - 1:c_sz, :], arr[0:1, :])

    def ones_col():
        return (_iota2((c_sz, HEAD), 1) == 0).astype(BF)

    def update_state(state_ref, lane0, k_all, v_aug, wk, sc):
        for h in range(nh):
            l = lane0 + h
            kw = (k_all[:, h * HEAD:(h + 1) * HEAD] * wk[:, l:l + 1]).astype(BF)
            state_ref[h] = sc[:, l:l + 1] * state_ref[h] + _dot_tn(kw, v_aug[h])

    def bwd_init():
        cb_ref[...] = jnp.zeros_like(cb_ref)
        m_ref[...] = jnp.zeros_like(m_ref)

    def fwd_init():
        cf_ref[...] = jnp.zeros_like(cf_ref)
        m_ref[...] = jnp.zeros_like(m_ref)

    def x_chunk(r0):
        return x_ref[0, pl.ds(r0, c_sz), :]

    def bwd_chunk(r0, ci, xb):
        a, cum, run_max = gates(xb)
        m_prev = m_ref[...]
        mbs_ref[ci] = m_prev
        for h in range(nh):
            cbs_ref[ci, h] = cb_ref[h].astype(BF)
        mu_e = edge(jnp.maximum(m_prev, run_max))
        kv = _dot(xb, wkv_ref[...])
        kv_ref[_rows(ci), :] = kv.astype(BF)
        v_aug = [jnp.concatenate([kv[:, BRANCH + h * HEAD:BRANCH + (h + 1) * HEAD].astype(BF), ones_col()],
                                 axis=1) for h in range(nh)]
        update_state(cb_ref, nh, kv[:, :BRANCH], v_aug, jnp.exp(a - mu_e), jnp.exp(m_prev - mu_e))
        m_ref[...] = edge(cum) + mu_e

    def fwd_chunk(r0, ci, xb):
        rows = pl.ds(r0, c_sz)
        a, cum, run_max = gates(xb)
        lower, upper = _tri_masks(c_sz)
        m_prev = jnp.where(_iota2((1, HEAD), 1) < nh, m_ref[...], mbs_ref[ci])
        mu = jnp.maximum(m_prev, run_max)
        s_inter = jnp.exp(m_prev - mu)
        thr = jnp.exp(-(cum + mu))

        gr = _dot_nt(wgr_ref[...], xb) + br_ref[...]
        ipre_r = gr[:2 * nh]
        pre_r, suf_r = _prefix_suffix_lanes(_log_sigmoid(gr[2 * nh:]))
        a_r = ipre_r - jnp.where(_iota2((2 * nh, c_sz), 0) < nh, pre_r, suf_r)

        proj = _dot(xb, wq_ref[:, :BRANCH])
        kvc = kv_ref[_rows(ci), :]
        v_aug = [jnp.concatenate([kvc[:, BRANCH + h * HEAD:BRANCH + (h + 1) * HEAD], ones_col()], axis=1)
                 for h in range(nh)]
        heads = range(nh)
        qs = [proj[:, h * HEAD:(h + 1) * HEAD].astype(BF) for h in heads]
        qk = [_dot_nt(qs[h], kvc[:, h * HEAD:(h + 1) * HEAD]) for h in heads]
        rs = [_dot(qs[h], jnp.concatenate([cf_ref[h].astype(BF), cbs_ref[ci, h]], axis=1)) for h in heads]
        pairs = [(dr, h) for h in heads for dr in range(2)]
        wgt = {}
        for dr, h in pairs:
            l = dr * nh + h
            wgt[dr, h] = jnp.where(lower if dr == 0 else upper, jnp.exp(a_r[l:l + 1, :] - mu[:, l:l + 1]), 0.0)
        pv = {p: _dot((qk[p[1]] * wgt[p]).astype(BF), v_aug[p[1]]) for p in pairs}
        outs = {}
        for dr, h in pairs:
            l = dr * nh + h
            tot = s_inter[:, l:l + 1] * rs[h][:, dr * ML_AUG:(dr + 1) * ML_AUG] + pv[dr, h]
            den = jnp.maximum(jnp.abs(tot[:, HEAD:HEAD + 1]), thr[:, l:l + 1])
            outs[dr, h] = tot[:, :HEAD] / den
        gates_og = _dot(xb, wq_ref[:, BRANCH:])
        for h in heads:
            sl = slice(h * HEAD, (h + 1) * HEAD)
            o_gate = gates_og[:, h * HEAD:(h + 1) * HEAD]
            g = gates_og[:, BRANCH + h * HEAD:BRANCH + (h + 1) * HEAD]
            y = _group_norm(_sigmoid(o_gate) * (outs[0, h] + outs[1, h]), True) * ng_ref[:, sl] * _silu(g)
            o_ref[0, rows, sl] = y.astype(o_ref.dtype)

        mu_e = edge(mu)
        update_state(cf_ref, 0, kvc[:, :BRANCH].astype(F32), v_aug, jnp.exp(a - mu_e), jnp.exp(m_prev - mu_e))
        m_ref[...] = edge(cum) + mu_e

    _sweep(nblk, x_chunk, bwd_chunk, x_chunk, fwd_chunk, bwd_init, fwd_init, lookahead=False)


def _mlstm(x, w_kv, w_q, w_gates, i_bias, f_bias, norm_g):
    b, t, d = x.shape
    nblk = t // TBLK
    nc = t // CHUNK
    nh = ML_HEADS
    pad = HEAD - 2 * nh
    w_i = w_gates[:, :2 * nh]
    w_f = w_gates[:, 2 * nh:]
    wgc = jnp.concatenate([jnp.pad(w_i, ((0, 0), (0, pad))), jnp.pad(w_f, ((0, 0), (0, pad)))], axis=1)
    wgr = w_gates.T
    bias = jnp.concatenate([i_bias.reshape(-1), f_bias.reshape(-1)]).astype(F32)
    bc = jnp.concatenate([jnp.pad(bias[:2 * nh], (0, pad)), jnp.pad(bias[2 * nh:], (0, pad))]).reshape(1, 2 * HEAD)
    br = bias.reshape(4 * nh, 1)
    in_specs = [
        _x_spec(nblk, d),
        _full_spec(w_kv.shape),
        _full_spec(w_q.shape),
        _full_spec(wgc.shape),
        _full_spec(wgr.shape),
        _full_spec(bc.shape),
        _full_spec(br.shape),
        _full_spec((1, BRANCH)),
    ]
    scratch = [
        pltpu.VMEM((nh, HEAD, ML_AUG), F32),
        pltpu.VMEM((nh, HEAD, ML_AUG), F32),
        pltpu.VMEM((nc, nh, HEAD, ML_AUG), BF),
        pltpu.VMEM((1, HEAD), F32),
        pltpu.VMEM((nc, 1, HEAD), F32),
        pltpu.VMEM((t, 2 * BRANCH), BF),
    ]
    return _mixer_call(_ml_body, b, t,
                       (x, w_kv, w_q, wgc.astype(BF), wgr.astype(BF), bc, br, norm_g.reshape(1, BRANCH)),
                       in_specs, scratch, "mlstm")


SSD_XBC = BRANCH + 2 * SSD_GROUPS * HEAD
SSD_GW = BRANCH // SSD_GROUPS


def _ssd_body(x_ref, xp_ref, xn_ref, wz_ref, wx_ref, wdc_ref, wdr_ref, dbc_ref, dbr_ref,
              ac_ref, ar_ref, cw_ref, cb_ref, dsk_ref, ng_ref, ex_ref, o_ref,
              xe_ref, sf_ref, sb_ref, sbs_ref, act_ref, *, nblk):
    c_sz = CHUNK
    nh = SSD_HEADS
    per = TBLK // CHUNK

    def dt_columns(xb):
        is_fwd = _iota2((c_sz, HEAD), 1) < nh
        dt = _softplus(_dot(xb, wdc_ref[...]) + dbc_ref[...])
        pre, suf = _prefix_suffix_rows(dt * ac_ref[...])
        return dt, jnp.where(is_fwd, pre, suf)

    def expand(a, dr):
        hi, mid, _ = _split3(a)
        e = ex_ref[:, dr * BRANCH:(dr + 1) * BRANCH]
        return _dot(hi, e) + _dot(mid, e)

    def update_state(state_ref, dr, xs, bm, dt, cum):
        cum_e = jnp.where(_iota2((1, HEAD), 1) < nh, cum[c_sz - 1:c_sz, :], cum[0:1, :])
        w_state = jnp.exp(cum_e - cum) * dt
        dec = jnp.exp(jnp.broadcast_to(cum_e, (8, HEAD)))
        xw = (xs * expand(w_state, dr)).astype(BF)
        d512 = expand(dec, dr)[0:1, :]
        for g in range(SSD_GROUPS):
            gs = slice(g * SSD_GW, (g + 1) * SSD_GW)
            state_ref[g] = d512[:, gs] * state_ref[g] + _dot_tn(bm[:, g * HEAD:(g + 1) * HEAD], xw[:, gs])

    def bwd_init():
        sb_ref[...] = jnp.zeros_like(sb_ref)

    def fwd_init():
        sf_ref[...] = jnp.zeros_like(sf_ref)

    def project_block():
        blk = nblk - 1 - pl.program_id(2)

        for j in range(per):
            xe_ref[pl.ds(HALO + j * c_sz, c_sz), :] = _dot(x_ref[0, pl.ds(j * c_sz, c_sz), :], wx_ref[...])
        halo = jnp.concatenate([xp_ref[0, 0], xn_ref[0, 0]], axis=0)
        ph = _dot(halo, wx_ref[...])
        xe_ref[0:HALO, :] = ph[:HALO] * jnp.where(blk > 0, 1.0, 0.0)
        xe_ref[HALO + TBLK:, :] = ph[HALO:] * jnp.where(blk < nblk - 1, 1.0, 0.0)

    def x_chunk(r0):
        return x_ref[0, pl.ds(r0, c_sz), :]

    def bwd_chunk(r0, ci, xb):
        half = (SSD_CONV - 1) // 2
        win = xe_ref[pl.ds(r0, c_sz + 2 * HALO), :]
        conv = cb_ref[...]
        for k in range(SSD_CONV):
            conv = conv + cw_ref[k:k + 1, :] * win[HALO - half + k:HALO - half + k + c_sz, :]
        act = _silu(conv)
        act_ref[_rows(ci), :] = act.astype(BF)
        dt, cum = dt_columns(xb)
        for g in range(SSD_GROUPS):
            sbs_ref[ci, g] = sb_ref[g].astype(BF)
        update_state(sb_ref, 1, act[:, :BRANCH], act[:, BRANCH:BRANCH + SSD_GROUPS * HEAD].astype(BF), dt, cum)

    def fwd_chunk(r0, ci, xb):
        rows = pl.ds(r0, c_sz)
        act = act_ref[_rows(ci), :]
        xs = act[:, :BRANCH].astype(F32)
        bm = act[:, BRANCH:BRANCH + SSD_GROUPS * HEAD]
        cm = act[:, BRANCH + SSD_GROUPS * HEAD:]
        dt, cum = dt_columns(xb)
        dt_r = _softplus(_dot_nt(wdr_ref[...], xb) + dbr_ref[...])
        pre_r, suf_r = _prefix_suffix_lanes(dt_r * ar_ref[...])
        cum_r = jnp.where(_iota2((2 * nh, c_sz), 0) < nh, pre_r, suf_r)
        shifted_r = cum_r - jnp.log(dt_r)
        below = _iota2((c_sz, c_sz), 1) < _iota2((c_sz, c_sz), 0)
        diag = _iota2((c_sz, c_sz), 1) == _iota2((c_sz, c_sz), 0)
        z = _dot(xb, wz_ref[...])
        lane_half = _iota2((c_sz, HEAD), 1) < SSD_HEADDIM
        ecum = jnp.exp(cum)
        ecum_f = expand(ecum, 0)
        ecum_b = expand(ecum, 1)
        hpg = SSD_HEADS // SSD_GROUPS
        groups = range(SSD_GROUPS)
        heads = range(nh)
        cg = [cm[:, g * HEAD:(g + 1) * HEAD] for g in groups]
        gmat = [_dot_nt(cg[g], bm[:, g * HEAD:(g + 1) * HEAD]) for g in groups]
        from_f = [_dot(cg[g], sf_ref[g].astype(BF)) for g in groups]
        from_b = [_dot(cg[g], sbs_ref[ci, g]) for g in groups]
        m = [jnp.exp(jnp.where(below, cum[:, h:h + 1] - shifted_r[h:h + 1, :],
                               cum[:, nh + h:nh + h + 1] - shifted_r[nh + h:nh + h + 1, :])) +
             jnp.where(diag, dt_r[h:h + 1, :], 0.0) for h in heads]
        xh = [jnp.where(lane_half if h % 2 == 0 else jnp.logical_not(lane_half),
                        xs[:, (h // 2) * HEAD:(h // 2 + 1) * HEAD], 0.0).astype(BF) for h in heads]
        part = [_dot((gmat[h // hpg] * m[h]).astype(BF), xh[h]) for h in heads]
        ys = []
        for g in groups:
            gs = slice(g * SSD_GW, (g + 1) * SSD_GW)
            pieces = [part[g * hpg + 2 * pair] + part[g * hpg + 2 * pair + 1] for pair in range(hpg // 2)]
            ys.append(jnp.concatenate(pieces, axis=1) + ecum_f[:, gs] * from_f[g] + ecum_b[:, gs] * from_b[g])
        y = jnp.concatenate(ys, axis=1) + dsk_ref[...] * xs
        y = y * _silu(z)
        for g in range(SSD_GROUPS):
            gs = slice(g * SSD_GW, (g + 1) * SSD_GW)
            o_ref[0, rows, gs] = (_group_norm(y[:, gs], False) * ng_ref[:, gs]).astype(o_ref.dtype)
        update_state(sf_ref, 0, xs, bm, dt, cum)

    pl.when(pl.program_id(1) == 0)(project_block)
    _sweep(nblk, x_chunk, bwd_chunk, x_chunk, fwd_chunk, bwd_init, fwd_init, lookahead=False)


def _ssd(x, w_z, w_xbc, w_dt, conv_w, conv_b, a_log, dt_bias, d_skip, norm_g):
    b, t, d = x.shape
    nblk = t // TBLK
    nc = t // CHUNK
    nh = SSD_HEADS
    pad = HEAD - 2 * nh
    wdc = jnp.pad(w_dt, ((0, 0), (0, pad)))
    wdr = w_dt.T
    db = dt_bias.reshape(-1).astype(F32)
    a = (-jnp.exp(a_log.astype(F32))).reshape(-1)
    dbc = jnp.pad(db, (0, pad)).reshape(1, HEAD)
    ac = jnp.pad(a, (0, pad)).reshape(1, HEAD)
    dbr = db.reshape(2 * nh, 1)
    ar = a.reshape(2 * nh, 1)
    cw = jnp.pad(conv_w.astype(F32), ((0, 8 - SSD_CONV), (0, 0)))
    dsk = jnp.repeat(d_skip.astype(F32), SSD_HEADDIM).reshape(1, BRANCH)
    ex = np.zeros((HEAD, 2 * BRANCH), np.float32)
    for l in range(2 * nh):
        ex[l, l * SSD_HEADDIM:(l + 1) * SSD_HEADDIM] = 1.0
    ex = jnp.asarray(ex, BF)
    xh = x.reshape(b, t // HALO, HALO, d)
    per = TBLK // HALO
    last = t // HALO - 1
    in_specs = [
        _x_spec(nblk, d),
        pl.BlockSpec((1, 1, HALO, d),
                     lambda bi, ph, c: (bi, jnp.maximum(_blk_index(ph, c, nblk) * per - 1, 0), 0, 0)),
        pl.BlockSpec((1, 1, HALO, d),
                     lambda bi, ph, c: (bi, jnp.minimum((_blk_index(ph, c, nblk) + 1) * per, last), 0, 0)),
        _full_spec(w_z.shape),
        _full_spec(w_xbc.shape),
        _full_spec(wdc.shape),
        _full_spec(wdr.shape),
        _full_spec(dbc.shape),
        _full_spec(dbr.shape),
        _full_spec(ac.shape),
        _full_spec(ar.shape),
        _full_spec(cw.shape),
        _full_spec((1, SSD_XBC)),
        _full_spec((1, BRANCH)),
        _full_spec((1, BRANCH)),
        _full_spec(ex.shape),
    ]
    scratch = [
        pltpu.VMEM((TBLK + 2 * HALO, SSD_XBC), F32),
        pltpu.VMEM((SSD_GROUPS, HEAD, SSD_GW), F32),
        pltpu.VMEM((SSD_GROUPS, HEAD, SSD_GW), F32),
        pltpu.VMEM((nc, SSD_GROUPS, HEAD, SSD_GW), BF),
        pltpu.VMEM((t, SSD_XBC), BF),
    ]
    return _mixer_call(_ssd_body, b, t,
                       (x, xh, xh, w_z, w_xbc, wdc.astype(BF), wdr.astype(BF), dbc, dbr, ac, ar, cw,
                        conv_b.astype(F32).reshape(1, SSD_XBC), dsk, norm_g.reshape(1, BRANCH), ex),
                       in_specs, scratch, "ssd")


def _out_body(x_ref, p_ref, m0_ref, m1_ref, m2_ref, m3_ref, wo_ref, lg_ref, lb_ref, wg_ref, wp_ref,
              o_ref, ob_ref, *, alpha):
    step = OUT_ROWS // OUT_PARTS
    parts = [slice(i * step, (i + 1) * step) for i in range(OUT_PARTS)]

    def mix(rows):
        y = None
        for i, m_ref in enumerate((m0_ref, m1_ref, m2_ref, m3_ref)):
            part = _dot(m_ref[0, rows, :], wo_ref[i * BRANCH:(i + 1) * BRANCH, :])
            y = part if y is None else y + part
        return y

    def norm(rows, y):
        r = alpha * x_ref[0, rows, :] + y
        mu = jnp.mean(r, axis=-1, keepdims=True)
        rc = r - mu
        var = jnp.mean(rc * rc, axis=-1, keepdims=True)
        return rc * lax.rsqrt(var + NORM_EPS) * lg_ref[...] + lb_ref[...]

    def finish(rows, xn):
        gate = _sigmoid(_dot(xn.astype(BF), wg_ref[...]))
        out = xn + gate * _dot(p_ref[0, rows, :].astype(BF), wp_ref[...])
        o_ref[0, rows, :] = out
        ob_ref[0, rows, :] = out.astype(BF)

    y = mix(parts[0])
    xn = None
    for i in range(OUT_PARTS):
        y_next = mix(parts[i + 1]) if i + 1 < OUT_PARTS else None
        xn_i = norm(parts[i], y)
        if xn is not None:
            finish(parts[i - 1], xn)
        xn = xn_i
        y = y_next
    finish(parts[OUT_PARTS - 1], xn)


def _out_layer(x, p, mixed, w_out, ln_g, ln_b, w_gate, w_proj, alpha):
    b, t, d = x.shape
    dp = p.shape[-1]
    rows = OUT_ROWS
    row_spec = lambda width: pl.BlockSpec((1, rows, width), lambda bi, r: (bi, r, 0))
    full = lambda shape: pl.BlockSpec(shape, lambda bi, r: (0,) * len(shape))
    return pl.pallas_call(
        functools.partial(_out_body, alpha=alpha),
        grid=(b, t // rows),
        in_specs=[row_spec(d), row_spec(dp)] + [row_spec(BRANCH)] * 4 +
                 [full(w_out.shape), full((1, d)), full((1, d)), full(w_gate.shape), full(w_proj.shape)],
        out_specs=(row_spec(d), row_spec(d)),
        out_shape=(jax.ShapeDtypeStruct((b, t, d), F32), jax.ShapeDtypeStruct((b, t, d), BF)),
        compiler_params=pltpu.CompilerParams(
            dimension_semantics=("arbitrary", "arbitrary"), vmem_limit_bytes=VMEM_LIMIT),
        name="out_proj",
    )(x, p, *mixed, w_out, ln_g.reshape(1, d), ln_b.reshape(1, d), w_gate, w_proj)


def _rope_tables(t):
    inv = ROPE_BASE ** (-jnp.arange(0, HEAD, 2, dtype=F32) / HEAD)
    ang = jnp.arange(t, dtype=F32)[:, None] * inv[None, :]
    cos = jnp.cos(ang)
    sin = jnp.sin(ang)
    return jnp.concatenate([cos, cos], axis=1), jnp.concatenate([-sin, sin], axis=1)


def _layer_weights(l, w_in, w_out, hgrn_lb_logits, w_ple_gate, w_ple_proj):
    wl = w_in[l]
    nb = BRANCH
    o1 = 4 * nb
    o2 = o1 + 5 * nb
    o3 = o2 + 5 * nb + 4 * ML_HEADS

    def cols(*ranges):
        return jnp.concatenate([wl[:, a:b] for a, b in ranges], axis=1).astype(BF)

    ret_kv = cols((nb, 3 * nb))
    ret_qg = cols((0, nb), (3 * nb, 4 * nb))
    hg = o1
    hgrn_zv = cols((hg + 2 * nb, hg + 4 * nb))
    hgrn_q = cols((hg, hg + 3 * nb), (hg + 4 * nb, hg + 5 * nb))
    ml = o2
    ml_kv = jnp.concatenate([wl[:, ml + nb:ml + 2 * nb] * (HEAD ** -0.5), wl[:, ml + 2 * nb:ml + 3 * nb]],
                            axis=1).astype(BF)
    ml_q = cols((ml, ml + nb), (ml + 3 * nb, ml + 5 * nb))
    ml_gates = wl[:, ml + 5 * nb:o3]
    w_z = cols((o3, o3 + nb))
    w_xbc = cols((o3 + nb, o3 + nb + SSD_XBC))
    w_dt = wl[:, o3 + nb + SSD_XBC:]
    lb_w = jax.nn.softmax(hgrn_lb_logits.astype(F32), axis=0)
    lb = (jnp.cumsum(lb_w, axis=0) - lb_w[0:1])[l]
    return dict(ret_kv=ret_kv, ret_qg=ret_qg, hgrn_zv=hgrn_zv, hgrn_q=hgrn_q, ml_kv=ml_kv, ml_q=ml_q,
                ml_gates=ml_gates, w_z=w_z, w_xbc=w_xbc, w_dt=w_dt, lb=lb, w_out=w_out[l].astype(BF),
                w_gate=w_ple_gate[l].astype(BF), w_proj=w_ple_proj[l].astype(BF))


def kernel(x_prompt, x_sample, p_prompt, p_sample, w_in, w_out, ln_g, ln_b, ret_log_rate, ret_norm_g,
           hgrn_lb_logits, hgrn_norm_g, mlstm_i_bias, mlstm_f_bias, mlstm_norm_g, ssd_conv_w, ssd_conv_b,
           ssd_a_log, ssd_dt_bias, ssd_d, ssd_norm_g, w_ple_gate, w_ple_proj):
    depth = w_in.shape[0]
    alpha = (2 * depth) ** 0.25
    weights = [_layer_weights(l, w_in, w_out, hgrn_lb_logits, w_ple_gate, w_ple_proj) for l in range(depth)]

    def trunk(x, p):
        cos2, sin2 = _rope_tables(x.shape[1])
        xb = x.astype(BF)
        for l in range(depth):
            w = weights[l]
            mixed = (
                _retention(xb, w["ret_kv"], w["ret_qg"], ret_log_rate[l], ret_norm_g[l], cos2, sin2),
                _hgrn2(xb, w["hgrn_zv"], w["hgrn_q"], w["lb"], hgrn_norm_g[l]),
                _mlstm(xb, w["ml_kv"], w["ml_q"], w["ml_gates"], mlstm_i_bias[l], mlstm_f_bias[l],
                       mlstm_norm_g[l]),
                _ssd(xb, w["w_z"], w["w_xbc"], w["w_dt"], ssd_conv_w[l], ssd_conv_b[l], ssd_a_log[l],
                     ssd_dt_bias[l], ssd_d[l], ssd_norm_g[l]),
            )
            x, xb = _out_layer(x, p[l], mixed, w["w_out"], ln_g[l], ln_b[l], w["w_gate"], w["w_proj"], alpha)
        return x

    return trunk(x_prompt, p_prompt), trunk(x_sample, p_sample)
```
